```python
import jax
import jax.numpy as jnp
from jax import lax
import numpy as np


D_MODEL = 2048
BATCH = 4
SEQ = 4096
DEPTH = 1

MEM_LEN = 256
BLOCK = 128
EPS = 1e-6
NEG_INF = -1e30

SWA_Q_HEADS = 16
SWA_KV_HEADS = 2
SWA_HEAD_DIM = 64
WINDOW = 128

MLA_HEADS = 4
MLA_Q_RANK = 512
MLA_KV_RANK = 512
MLA_NOPE_DIM = 128
MLA_ROPE_DIM = 64
MLA_V_DIM = 128
ROPE_THETA = 10000.0

MEM_HEADS = 4
MEM_HEAD_DIM = 128

MIX_WIDTH = SWA_Q_HEADS * SWA_HEAD_DIM + MLA_HEADS * MLA_V_DIM + MEM_HEADS * MEM_HEAD_DIM
IN_SIZES = (SWA_Q_HEADS * SWA_HEAD_DIM, SWA_KV_HEADS * SWA_HEAD_DIM, SWA_KV_HEADS * SWA_HEAD_DIM,
            MLA_Q_RANK, MLA_KV_RANK, MLA_ROPE_DIM, MEM_HEADS * MEM_HEAD_DIM)
IN_WIDTH = sum(IN_SIZES)
IN_SPLITS = tuple(int(v) for v in np.cumsum(IN_SIZES)[:-1])

D_FF = ((8 * D_MODEL + 3 * 256 - 1) // (3 * 256)) * 256

kernel_name = "hymba_swa_sink_mla_memory_swiglu"


def rms_norm(x, g):
    xf = x.astype(jnp.float32)
    y = xf * lax.rsqrt(jnp.mean(xf * xf, axis=-1, keepdims=True) + EPS)
    return (y * g.astype(jnp.float32)).astype(x.dtype)


def alibi_slopes(n):
    return 2.0 ** (-8.0 * jnp.arange(1, n + 1, dtype=jnp.float32) / n)


def apply_rope(x, cos, sin):
    x1, x2 = jnp.split(x.astype(jnp.float32), 2, axis=-1)
    return jnp.concatenate([x1 * cos - x2 * sin, x1 * sin + x2 * cos], axis=-1).astype(x.dtype)


def swa_sink_attention(q, k, v, pos, sinks):
    b, s, _, d = q.shape
    nb = s // BLOCK
    g = SWA_Q_HEADS // SWA_KV_HEADS
    qb = q.reshape(b, nb, BLOCK, SWA_KV_HEADS, g, d)

    def with_prev(t):
        tb = t.reshape((b, nb, BLOCK) + t.shape[2:])
        prev = jnp.concatenate([jnp.zeros_like(tb[:, :1]), tb[:, :-1]], axis=1)
        return jnp.concatenate([prev, tb], axis=2)

    kb, vb, pk = with_prev(k), with_prev(v), with_prev(pos)
    pq = pos.reshape(b, nb, BLOCK)
    scores = jnp.einsum('bnqhgd,bnkhd->bnhgqk', qb, kb,
                        preferred_element_type=jnp.float32) * (d ** -0.5)
    dist = jnp.abs(pq[:, :, :, None] - pk[:, :, None, :]).astype(jnp.float32)
    slopes = alibi_slopes(SWA_Q_HEADS).reshape(SWA_KV_HEADS, g)
    scores = scores - slopes[None, None, :, :, None, None] * dist[:, :, None, None]
    qi = jnp.arange(BLOCK)[:, None] + BLOCK
    ki = jnp.arange(2 * BLOCK)[None, :]
    band = (ki <= qi) & (qi - ki < WINDOW)
    not_first = jnp.arange(nb)[:, None, None] > 0
    valid = band[None] & (not_first | (ki >= BLOCK)[None])
    scores = jnp.where(valid[None, :, None, None], scores, NEG_INF)
    sink = sinks.astype(jnp.float32).reshape(SWA_KV_HEADS, g)[None, None, :, :, None, None]
    m = jnp.maximum(jnp.max(scores, axis=-1, keepdims=True), sink)
    p = jnp.exp(scores - m)
    p = p / (jnp.sum(p, axis=-1, keepdims=True) + jnp.exp(sink - m))
    out = jnp.einsum('bnhgqk,bnkhd->bnqhgd', p.astype(v.dtype), vb)
    return out.reshape(b, s, SWA_Q_HEADS * d)


def mla_causal_attention(q_nope, q_rope, k_nope, k_rope, v):
    b, s, h, _ = q_nope.shape
    nb = s // BLOCK
    scale = (MLA_NOPE_DIM + MLA_ROPE_DIM) ** -0.5
    k_idx = jnp.arange(s)

    def to_blocks(t):
        return jnp.moveaxis(t.reshape((b, nb, BLOCK) + t.shape[2:]), 1, 0)

    def one_block(args):
        qn, qr, i = args
        sc = (jnp.einsum('bqhd,bkhd->bhqk', qn, k_nope, preferred_element_type=jnp.float32)
              + jnp.einsum('bqhd,bkd->bhqk', qr, k_rope, preferred_element_type=jnp.float32)) * scale
        q_idx = i * BLOCK + jnp.arange(BLOCK)
        sc = jnp.where(k_idx[None, :] <= q_idx[:, None], sc, NEG_INF)
        p = jax.nn.softmax(sc, axis=-1).astype(v.dtype)
        return jnp.einsum('bhqk,bkhd->bqhd', p, v)

    out = lax.map(one_block, (to_blocks(q_nope), to_blocks(q_rope), jnp.arange(nb)))
    return jnp.moveaxis(out, 0, 1).reshape(b, s, h * MLA_V_DIM)


def memory_cross_attention(q, k, v):
    b, s, h, d = q.shape
    sc = jnp.einsum('bshd,bmhd->bhsm', q, k, preferred_element_type=jnp.float32) * (d ** -0.5)
    p = jax.nn.softmax(sc, axis=-1).astype(v.dtype)
    return jnp.einsum('bhsm,bmhd->bshd', p, v).reshape(b, s, h * d)


def setup_inputs(seed: int = 0) -> dict:
    key = jax.random.key(seed)
    ks = iter(jax.random.split(key, 32))

    def nrm(shape, fan_in):
        return jax.random.normal(next(ks), shape, jnp.float32) * (fan_in ** -0.5)

    def gain(n):
        return 1.0 + 0.02 * jax.random.normal(next(ks), (DEPTH, n), jnp.float32)

    x = jax.random.normal(next(ks), (BATCH, SEQ, D_MODEL), jnp.float32)
    mem = jax.random.normal(next(ks), (BATCH, MEM_LEN, D_MODEL), jnp.float32)
    offsets = jax.random.randint(next(ks), (BATCH, 1), 0, 1024, dtype=jnp.int32)
    positions = offsets + jnp.arange(SEQ, dtype=jnp.int32)[None, :]
    return {
        "x": x,
        "mem": mem,
        "positions": positions,
        "attn_norm_g": gain(D_MODEL),
        "w_in": nrm((DEPTH, D_MODEL, IN_WIDTH), D_MODEL),
        "swa_q_norm_g": gain(SWA_HEAD_DIM),
        "swa_k_norm_g": gain(SWA_HEAD_DIM),
        "swa_sinks": 0.5 * jax.random.normal(next(ks), (DEPTH, SWA_Q_HEADS), jnp.float32),
        "mla_cq_norm_g": gain(MLA_Q_RANK),
        "mla_ckv_norm_g": gain(MLA_KV_RANK),
        "w_uq": nrm((DEPTH, MLA_Q_RANK, MLA_HEADS * (MLA_NOPE_DIM + MLA_ROPE_DIM)), MLA_Q_RANK),
        "w_ukv": nrm((DEPTH, MLA_KV_RANK, MLA_HEADS * (MLA_NOPE_DIM + MLA_V_DIM)), MLA_KV_RANK),
        "mla_qn_norm_g": gain(MLA_NOPE_DIM),
        "mla_qr_norm_g": gain(MLA_ROPE_DIM),
        "mla_kn_norm_g": gain(MLA_NOPE_DIM),
        "mla_kr_norm_g": gain(MLA_ROPE_DIM),
        "mem_norm_g": gain(D_MODEL),
        "w_mem_kv": nrm((DEPTH, D_MODEL, 2 * MEM_HEADS * MEM_HEAD_DIM), D_MODEL),
        "mem_q_norm_g": gain(MEM_HEAD_DIM),
        "mem_k_norm_g": gain(MEM_HEAD_DIM),
        "w_out": nrm((DEPTH, MIX_WIDTH, D_MODEL), MIX_WIDTH),
        "ffn_norm_g": gain(D_MODEL),
        "w_gate": nrm((DEPTH, D_MODEL, D_FF), D_MODEL),
        "w_up": nrm((DEPTH, D_MODEL, D_FF), D_MODEL),
        "w_down": nrm((DEPTH, D_FF, D_MODEL), D_FF),
    }


def reference(x, mem, positions, attn_norm_g, w_in, swa_q_norm_g, swa_k_norm_g, swa_sinks,
              mla_cq_norm_g, mla_ckv_norm_g, w_uq, w_ukv, mla_qn_norm_g, mla_qr_norm_g,
              mla_kn_norm_g, mla_kr_norm_g, mem_norm_g, w_mem_kv, mem_q_norm_g, mem_k_norm_g,
              w_out, ffn_norm_g, w_gate, w_up, w_down):
    b, s, _ = x.shape
    m_len = mem.shape[1]
    inv_freq = ROPE_THETA ** (-jnp.arange(0, MLA_ROPE_DIM, 2, dtype=jnp.float32) / MLA_ROPE_DIM)
    ang = positions.astype(jnp.float32)[..., None] * inv_freq
    cos, sin = jnp.cos(ang), jnp.sin(ang)
    h = x
    for l in range(DEPTH):
        hn = rms_norm(h, attn_norm_g[l])
        proj = hn @ w_in[l]
        q_a, k_a, v_a, c_q, c_kv, k_r, q_m = jnp.split(proj, IN_SPLITS, axis=-1)

        q_a = rms_norm(q_a.reshape(b, s, SWA_Q_HEADS, SWA_HEAD_DIM), swa_q_norm_g[l])
        k_a = rms_norm(k_a.reshape(b, s, SWA_KV_HEADS, SWA_HEAD_DIM), swa_k_norm_g[l])
        v_a = v_a.reshape(b, s, SWA_KV_HEADS, SWA_HEAD_DIM)
        y_a = swa_sink_attention(q_a, k_a, v_a, positions, swa_sinks[l])

        q_b = (rms_norm(c_q, mla_cq_norm_g[l]) @ w_uq[l]).reshape(
            b, s, MLA_HEADS, MLA_NOPE_DIM + MLA_ROPE_DIM)
        kv_b = (rms_norm(c_kv, mla_ckv_norm_g[l]) @ w_ukv[l]).reshape(
            b, s, MLA_HEADS, MLA_NOPE_DIM + MLA_V_DIM)
        q_nope = rms_norm(q_b[..., :MLA_NOPE_DIM], mla_qn_norm_g[l])
        q_rope = apply_rope(rms_norm(q_b[..., MLA_NOPE_DIM:], mla_qr_norm_g[l]),
                            cos[:, :, None], sin[:, :, None])
        k_nope = rms_norm(kv_b[..., :MLA_NOPE_DIM], mla_kn_norm_g[l])
        v_b = kv_b[..., MLA_NOPE_DIM:]
        k_rope = apply_rope(rms_norm(k_r, mla_kr_norm_g[l]), cos, sin)
        y_b = mla_causal_attention(q_nope, q_rope, k_nope, k_rope, v_b)

        q_m = rms_norm(q_m.reshape(b, s, MEM_HEADS, MEM_HEAD_DIM), mem_q_norm_g[l])
        kv_m = (rms_norm(mem, mem_norm_g[l]) @ w_mem_kv[l]).reshape(
            b, m_len, 2, MEM_HEADS, MEM_HEAD_DIM)
        k_m = rms_norm(kv_m[:, :, 0], mem_k_norm_g[l])
        v_m = kv_m[:, :, 1]
        y_m = memory_cross_attention(q_m, k_m, v_m)

        h = h + jnp.concatenate([y_a, y_b, y_m], axis=-1) @ w_out[l]

        fn = rms_norm(h, ffn_norm_g[l])
        h = h + (jax.nn.silu(fn @ w_gate[l]) * (fn @ w_up[l])) @ w_down[l]
    return h
```

```python
import functools

import jax
import jax.numpy as jnp
from jax import lax
from jax.experimental import pallas as pl
from jax.experimental.pallas import tpu as pltpu

EPS = 1e-6
NEG_INF = -1e30
BLOCK = 128
WINDOW = 128

SWA_Q_HEADS = 16
SWA_KV_HEADS = 2
SWA_HEAD_DIM = 64

MLA_HEADS = 4
MLA_NOPE_DIM = 128
MLA_ROPE_DIM = 64
MLA_V_DIM = 128
ROPE_THETA = 10000.0

MEM_HEADS = 4
MEM_HEAD_DIM = 128

LANES = 128
MXU_DIM = 256
VMEM_LIMIT_BYTES = 56 * 1024 * 1024

F32 = jnp.float32
BF16 = jnp.bfloat16

_SP_SWA_Q, _SP_SWA_K, _SP_QN, _SP_QR, _SP_KN, _SP_KR, _SP_MEMQ, _SP_FREQ, _SP_SIGN = range(9)
_SP_ROWS = 16


def _dot(a, b):
    return jnp.dot(a, b, preferred_element_type=F32)


def _dot_nt(a, b):
    return lax.dot_general(a, b, (((1,), (1,)), ((), ())), preferred_element_type=F32)


def _rms(x, g):
    return x * lax.rsqrt(jnp.mean(x * x, axis=-1, keepdims=True) + EPS) * g


def _const_spec(shape):
    nd = len(shape)
    return pl.BlockSpec(shape, lambda *_: (0,) * nd, pipeline_mode=pl.Buffered(1))


def _mem_kv_kernel(mem_ref, g_ref, w_ref, gk_ref, k_ref, v_ref):
    xn = _rms(mem_ref[...], g_ref[...]).astype(BF16)
    kv = _dot(xn, w_ref[...])
    hd = MEM_HEADS * MEM_HEAD_DIM
    for h in range(MEM_HEADS):
        sl = slice(h * MEM_HEAD_DIM, (h + 1) * MEM_HEAD_DIM)
        k_ref[:, sl] = _rms(kv[:, sl], gk_ref[...]).astype(BF16)
    v_ref[...] = kv[:, hd:].astype(BF16)


def _mem_kv(mem2d, g, w, gk, batch, m_len):
    d = mem2d.shape[1]
    hd = MEM_HEADS * MEM_HEAD_DIM
    return pl.pallas_call(
        _mem_kv_kernel,
        grid=(batch,),
        in_specs=[pl.BlockSpec((m_len, d), lambda b: (b, 0)),
                  _const_spec((1, d)), _const_spec((d, 2 * hd)), _const_spec((1, MEM_HEAD_DIM))],
        out_specs=[pl.BlockSpec((m_len, hd), lambda b: (b, 0))] * 2,
        out_shape=[jax.ShapeDtypeStruct((batch * m_len, hd), BF16)] * 2,
        compiler_params=pltpu.CompilerParams(dimension_semantics=("arbitrary",),
                                             vmem_limit_bytes=VMEM_LIMIT_BYTES),
        name="mem_kv",
    )(mem2d, g, w, gk)


_C_QA = (0, 1024)
_C_KVA = (1024, 1280)
_C_CQ = (1280, 1792)
_C_CKV = (1792, 2304)
_C_KR = (2304, 2432)
_C_QM = (2432, 2944)
_W1_COLS = 2944


def _in_proj_kernel(x_ref, pos_ref, gattn_ref, sp_ref, w1_ref, wq_ref, wkv_ref, gcq_ref, gckv_ref,
                    bd_ref, qa_ref, ka_ref, va_ref, krd_ref, qm_ref, qb_ref, kn_ref, vb_ref):
    sp = sp_ref[...]
    row = lambda r, n=LANES: sp[r:r + 1, :n]
    hn = _rms(x_ref[...], gattn_ref[...]).astype(BF16)

    def proj(cols):
        return _dot(hn, w1_ref[:, cols[0]:cols[1]])

    def head64_rms(blk, g):
        ss = _dot((blk * blk).astype(BF16), bd_ref[...])
        return blk * lax.rsqrt(ss * (1.0 / SWA_HEAD_DIM) + EPS) * g

    ang = pos_ref[...].astype(F32) * row(_SP_FREQ)
    cos = jnp.cos(ang)
    sin_signed = jnp.sin(ang) * row(_SP_SIGN)

    def rope_dup(y):
        return y * cos + pltpu.roll(y, MLA_ROPE_DIM // 2, 1) * sin_signed

    qa = proj(_C_QA)
    for c in range(_C_QA[1] // MXU_DIM):
        sl = slice(c * MXU_DIM, (c + 1) * MXU_DIM)
        qa_ref[:, sl] = (head64_rms(qa[:, sl], row(_SP_SWA_Q, MXU_DIM))
                         * (SWA_HEAD_DIM ** -0.5)).astype(BF16)
    kva = proj(_C_KVA)
    ka_ref[...] = head64_rms(kva, row(_SP_SWA_K, MXU_DIM))[:, :LANES].astype(BF16)
    va_ref[...] = kva[:, LANES:].astype(BF16)

    kr = proj(_C_KR)
    krd_ref[...] = rope_dup(_rms(kr, row(_SP_KR))).astype(BF16)

    qm = proj(_C_QM)
    for h in range(MEM_HEADS):
        sl = slice(h * MEM_HEAD_DIM, (h + 1) * MEM_HEAD_DIM)
        qm_ref[:, sl] = (_rms(qm[:, sl], row(_SP_MEMQ)) * (MEM_HEAD_DIM ** -0.5)).astype(BF16)

    scale = (MLA_NOPE_DIM + MLA_ROPE_DIM) ** -0.5
    cqn = _rms(proj(_C_CQ), gcq_ref[...]).astype(BF16)
    qb = _dot(cqn, wq_ref[...])
    for h in range(MLA_HEADS):
        o = h * 2 * LANES
        qb_ref[:, o:o + LANES] = (_rms(qb[:, o:o + LANES], row(_SP_QN)) * scale).astype(BF16)
        qr = rope_dup(_rms(qb[:, o + LANES:o + 2 * LANES], row(_SP_QR)))
        qb_ref[:, o + LANES:o + 2 * LANES] = (qr * (0.5 * scale)).astype(BF16)
    ckvn = _rms(proj(_C_CKV), gckv_ref[...]).astype(BF16)
    kvb = _dot(ckvn, wkv_ref[...])
    hn_cols = MLA_HEADS * MLA_NOPE_DIM
    for h in range(MLA_HEADS):
        sl = slice(h * MLA_NOPE_DIM, (h + 1) * MLA_NOPE_DIM)
        kn_ref[:, sl] = _rms(kvb[:, sl], row(_SP_KN)).astype(BF16)
    vb_ref[...] = kvb[:, hn_cols:].astype(BF16)


def _in_proj(x2d, pos_col, gattn, sp, w1, wq, wkv, gcq, gckv, bd, tm):
    t, d = x2d.shape
    widths = (1024, 128, 128, 128, 512, 1024, 512, 512)
    rows = lambda w: pl.BlockSpec((tm, w), lambda i: (i, 0))
    return pl.pallas_call(
        _in_proj_kernel,
        grid=(t // tm,),
        in_specs=[rows(d), rows(1), _const_spec(gattn.shape), _const_spec(sp.shape),
                  _const_spec(w1.shape), _const_spec(wq.shape), _const_spec(wkv.shape),
                  _const_spec(gcq.shape), _const_spec(gckv.shape), _const_spec(bd.shape)],
        out_specs=[rows(w) for w in widths],
        out_shape=[jax.ShapeDtypeStruct((t, w), BF16) for w in widths],
        compiler_params=pltpu.CompilerParams(dimension_semantics=("arbitrary",),
                                             vmem_limit_bytes=VMEM_LIMIT_BYTES),
        name="in_proj",
    )(x2d, pos_col, gattn, sp, w1, wq, wkv, gcq, gckv, bd)


def _swa_kernel(ss_ref, q_ref, kc_ref, kp_ref, vc_ref, vp_ref, pq_ref, pkc_ref, pkp_ref, o_ref, *, tq):
    i = pl.program_id(1)
    g = SWA_Q_HEADS // SWA_KV_HEADS
    pairs = g // 2
    lane = lax.broadcasted_iota(jnp.int32, (1, LANES), 1)
    lo = lane < SWA_HEAD_DIM
    kcat = jnp.concatenate([kp_ref[...], kc_ref[...]], axis=0)
    vcat = jnp.concatenate([vp_ref[...], vc_ref[...]], axis=0)
    pk = jnp.concatenate([pkp_ref[...], pkc_ref[...]], axis=1)
    r = lax.broadcasted_iota(jnp.int32, (BLOCK, 2 * BLOCK), 0)
    c = lax.broadcasted_iota(jnp.int32, (BLOCK, 2 * BLOCK), 1)
    band = (c > r) & (c <= r + BLOCK)
    zero = jnp.zeros((), BF16)

    for j in range(tq // BLOCK):
        rows = slice(j * BLOCK, (j + 1) * BLOCK)
        keys = slice(j * BLOCK, (j + 2) * BLOCK)
        kj, vj = kcat[keys], vcat[keys]
        if j == 0:
            valid = band & (c >= jnp.where(i == 0, BLOCK, 0))
        else:
            valid = band
        dist = jnp.abs(pq_ref[rows, :] - pk[:, keys]).astype(F32)
        for kv in range(SWA_KV_HEADS):
            mine = lo if kv == 0 else jnp.logical_not(lo)
            k_self = jnp.where(mine, kj, zero)
            v_self = jnp.where(mine, vj, zero)
            k_lo, k_hi = (k_self, pltpu.roll(k_self.astype(F32), SWA_HEAD_DIM, 1).astype(BF16))
            v_lo, v_hi = (v_self, pltpu.roll(v_self.astype(F32), SWA_HEAD_DIM, 1).astype(BF16))
            if kv == 1:
                k_lo, k_hi, v_lo, v_hi = k_hi, k_lo, v_hi, v_lo
            qs = jnp.concatenate(
                [q_ref[rows, (kv * pairs + p) * LANES:(kv * pairs + p + 1) * LANES] for p in range(pairs)],
                axis=0)
            s_par = (_dot_nt(qs, k_lo), _dot_nt(qs, k_hi))
            p_par, rinv = [], []
            for par in range(2):
                ps, rs = [], []
                for p in range(pairs):
                    head = kv * g + 2 * p + par
                    s = s_par[par][p * BLOCK:(p + 1) * BLOCK]
                    s = jnp.where(valid, s - ss_ref[0, head] * dist, NEG_INF)
                    sink = ss_ref[1, head]
                    m = jnp.maximum(jnp.max(s, axis=-1, keepdims=True), sink)
                    e = jnp.exp(s - m)
                    den = jnp.sum(e, axis=-1, keepdims=True) + jnp.exp(sink - m)
                    ps.append(e.astype(BF16))
                    rs.append(1.0 / den)
                p_par.append(jnp.concatenate(ps, axis=0))
                rinv.append(jnp.concatenate(rs, axis=0))
            pcat = jnp.concatenate(p_par, axis=1)
            vst = jnp.concatenate([v_lo, v_hi], axis=0)
            o = _dot(pcat, vst) * jnp.where(lo, rinv[0], rinv[1])
            for p in range(pairs):
                grp = kv * pairs + p
                o_ref[rows, grp * LANES:(grp + 1) * LANES] = o[p * BLOCK:(p + 1) * BLOCK].astype(BF16)


def _swa_attn(slope_sink, qa, ka, va, pos_col, pos_row, batch, seq, tq):
    t = batch * seq
    nq = seq // tq
    nb = tq // BLOCK
    cur = lambda w: pl.BlockSpec((tq, w), lambda b, i: (b * nq + i, 0))
    prev = lambda w: pl.BlockSpec(
        (BLOCK, w), lambda b, i: (b * nq * nb + jnp.maximum(i * nb - 1, 0), 0))
    return pl.pallas_call(
        functools.partial(_swa_kernel, tq=tq),
        grid=(batch, nq),
        in_specs=[pl.BlockSpec(memory_space=pltpu.SMEM),
                  cur(SWA_Q_HEADS * SWA_HEAD_DIM), cur(LANES), prev(LANES), cur(LANES), prev(LANES),
                  cur(1),
                  pl.BlockSpec((None, 1, tq), lambda b, i: (b, 0, i)),
                  pl.BlockSpec((None, 1, BLOCK), lambda b, i: (b, 0, jnp.maximum(i * nb - 1, 0)))],
        out_specs=cur(SWA_Q_HEADS * SWA_HEAD_DIM),
        out_shape=jax.ShapeDtypeStruct((t, SWA_Q_HEADS * SWA_HEAD_DIM), BF16),
        compiler_params=pltpu.CompilerParams(dimension_semantics=("arbitrary", "arbitrary"),
                                             vmem_limit_bytes=VMEM_LIMIT_BYTES),
        name="swa_attn",
    )(slope_sink, qa, ka, ka, va, va, pos_col, pos_row, pos_row)


def _mla_kernel(q_ref, kn_ref, kr_ref, v_ref, o_ref, *, tq):
    qi = pl.program_id(2)
    q = q_ref[...]

    def block(j, carry, diagonal):
        m, l, acc = carry
        k0 = pl.multiple_of(j * tq, tq)
        k = jnp.concatenate([kn_ref[pl.ds(k0, tq), :], kr_ref[pl.ds(k0, tq), :]], axis=1)
        s = _dot_nt(q, k)
        if diagonal:
            r = lax.broadcasted_iota(jnp.int32, (tq, tq), 0)
            c = lax.broadcasted_iota(jnp.int32, (tq, tq), 1)
            s = jnp.where(c <= r, s, NEG_INF)
        m_new = jnp.maximum(m, jnp.max(s, axis=-1, keepdims=True))
        p = jnp.exp(s - m_new)
        alpha = jnp.exp(m - m_new)
        l = alpha * l + jnp.sum(p, axis=-1, keepdims=True)
        acc = alpha * acc + _dot(p.astype(BF16), v_ref[pl.ds(k0, tq), :])
        return m_new, l, acc

    init = (jnp.full((tq, 1), NEG_INF, F32), jnp.zeros((tq, 1), F32), jnp.zeros((tq, MLA_V_DIM), F32))
    carry = lax.fori_loop(0, qi, lambda j, cr: block(j, cr, False), init)
    _, l, acc = block(qi, carry, True)
    o_ref[...] = (acc / l).astype(BF16)


def _mla_attn(qb, kn, krd, vb, batch, seq, tq):
    t = batch * seq
    nq = seq // tq
    return pl.pallas_call(
        functools.partial(_mla_kernel, tq=tq),
        grid=(batch, MLA_HEADS, nq),
        in_specs=[pl.BlockSpec((tq, 2 * LANES), lambda b, h, i: (b * nq + i, h)),
                  pl.BlockSpec((seq, MLA_NOPE_DIM), lambda b, h, i: (b, h)),
                  pl.BlockSpec((seq, LANES), lambda b, h, i: (b, 0)),
                  pl.BlockSpec((seq, MLA_V_DIM), lambda b, h, i: (b, h))],
        out_specs=pl.BlockSpec((tq, MLA_V_DIM), lambda b, h, i: (b * nq + i, h)),
        out_shape=jax.ShapeDtypeStruct((t, MLA_HEADS * MLA_V_DIM), BF16),
        compiler_params=pltpu.CompilerParams(dimension_semantics=("arbitrary",) * 3,
                                             vmem_limit_bytes=VMEM_LIMIT_BYTES),
        name="mla_attn",
    )(qb, kn, krd, vb)


def _mem_attn_kernel(q_ref, k_ref, v_ref, o_ref):
    for h in range(MEM_HEADS):
        sl = slice(h * MEM_HEAD_DIM, (h + 1) * MEM_HEAD_DIM)
        s = _dot_nt(q_ref[:, sl], k_ref[:, sl])
        e = jnp.exp(s - jnp.max(s, axis=-1, keepdims=True))
        o = _dot(e.astype(BF16), v_ref[:, sl]) / jnp.sum(e, axis=-1, keepdims=True)
        o_ref[:, sl] = o.astype(BF16)


def _mem_attn(qm, km, vm, batch, seq, m_len, tq):
    t = batch * seq
    nq = seq // tq
    hd = MEM_HEADS * MEM_HEAD_DIM
    return pl.pallas_call(
        _mem_attn_kernel,
        grid=(batch, nq),
        in_specs=[pl.BlockSpec((tq, hd), lambda b, i: (b * nq + i, 0)),
                  pl.BlockSpec((m_len, hd), lambda b, i: (b, 0)),
                  pl.BlockSpec((m_len, hd), lambda b, i: (b, 0))],
        out_specs=pl.BlockSpec((tq, hd), lambda b, i: (b * nq + i, 0)),
        out_shape=jax.ShapeDtypeStruct((t, hd), BF16),
        compiler_params=pltpu.CompilerParams(dimension_semantics=("arbitrary", "arbitrary"),
                                             vmem_limit_bytes=VMEM_LIMIT_BYTES),
        name="mem_attn",
    )(qm, km, vm)


def _out_proj_kernel(x_ref, ya_ref, yb_ref, ym_ref, w_ref, h_ref):
    na = ya_ref.shape[1]
    nb = na + yb_ref.shape[1]
    h_ref[...] = (x_ref[...] + _dot(ya_ref[...], w_ref[0:na, :]) + _dot(yb_ref[...], w_ref[na:nb, :])
                  + _dot(ym_ref[...], w_ref[nb:, :]))


def _out_proj(x2d, ya, yb, ym, w_out, tm):
    t, d = x2d.shape
    rows = lambda w: pl.BlockSpec((tm, w), lambda i: (i, 0))
    return pl.pallas_call(
        _out_proj_kernel,
        grid=(t // tm,),
        in_specs=[rows(d), rows(ya.shape[1]), rows(yb.shape[1]), rows(ym.shape[1]),
                  _const_spec(w_out.shape)],
        out_specs=rows(d),
        out_shape=jax.ShapeDtypeStruct((t, d), F32),
        compiler_params=pltpu.CompilerParams(dimension_semantics=("arbitrary",),
                                             vmem_limit_bytes=VMEM_LIMIT_BYTES),
        name="out_proj",
    )(x2d, ya, yb, ym, w_out)


def _ffn_kernel(h_ref, g_ref, wg_ref, wu_ref, wd_ref, o_ref, fn_ref):
    j = pl.program_id(1)

    @pl.when(j == 0)
    def _():
        h = h_ref[...]
        fn_ref[...] = _rms(h, g_ref[...]).astype(BF16)
        o_ref[...] = h

    fn = fn_ref[...]
    gate = _dot(fn, wg_ref[...])
    up = _dot(fn, wu_ref[...])
    act = (gate * jax.nn.sigmoid(gate) * up).astype(BF16)
    o_ref[...] += _dot(act, wd_ref[...])


def _ffn(h, g, wg, wu, wd, tm, tf):
    t, d = h.shape
    dff = wg.shape[1]
    return pl.pallas_call(
        _ffn_kernel,
        grid=(t // tm, dff // tf),
        in_specs=[pl.BlockSpec((tm, d), lambda i, j: (i, 0), pipeline_mode=pl.Buffered(1)),
                  _const_spec((1, d)),
                  pl.BlockSpec((d, tf), lambda i, j: (0, j)),
                  pl.BlockSpec((d, tf), lambda i, j: (0, j)),
                  pl.BlockSpec((tf, d), lambda i, j: (j, 0))],
        out_specs=pl.BlockSpec((tm, d), lambda i, j: (i, 0)),
        out_shape=jax.ShapeDtypeStruct((t, d), F32),
        scratch_shapes=[pltpu.VMEM((tm, d), BF16)],
        compiler_params=pltpu.CompilerParams(dimension_semantics=("arbitrary", "arbitrary"),
                                             vmem_limit_bytes=VMEM_LIMIT_BYTES),
        name="ffn",
    )(h, g, wg, wu, wd)


def _tile_row(v, width):
    v = v.astype(F32).reshape(-1)
    return jnp.tile(v, width // v.shape[0])


def _layer(h, mem2d, pos_col, pos_row, batch, seq, m_len, p):
    (attn_norm_g, w_in, swa_q_g, swa_k_g, swa_sinks, cq_g, ckv_g, w_uq, w_ukv, qn_g, qr_g, kn_g, kr_g,
     mem_g, w_mem_kv, memq_g, memk_g, w_out, ffn_g, w_gate, w_up, w_down) = p
    width = 2 * LANES
    inv_freq = ROPE_THETA ** (-jnp.arange(0, MLA_ROPE_DIM, 2, dtype=F32) / MLA_ROPE_DIM)
    sign = jnp.concatenate([-jnp.ones((MLA_ROPE_DIM // 2,), F32), jnp.ones((MLA_ROPE_DIM // 2,), F32)])
    rows = [swa_q_g, swa_k_g, qn_g, qr_g, kn_g, kr_g, memq_g, inv_freq, sign]
    sp = jnp.stack([_tile_row(r, width) for r in rows]
                   + [jnp.zeros((width,), F32)] * (_SP_ROWS - len(rows)))

    kr0 = _C_KR[0]
    w1 = jnp.concatenate([w_in[:, :kr0 + MLA_ROPE_DIM], w_in[:, kr0:kr0 + MLA_ROPE_DIM],
                          w_in[:, kr0 + MLA_ROPE_DIM:]], axis=1).astype(BF16)
    qd = MLA_NOPE_DIM + MLA_ROPE_DIM
    wq3 = w_uq.reshape(w_uq.shape[0], MLA_HEADS, qd)
    wq = jnp.concatenate([wq3, wq3[:, :, MLA_NOPE_DIM:]], axis=2).reshape(w_uq.shape[0], -1).astype(BF16)
    wkv3 = w_ukv.reshape(w_ukv.shape[0], MLA_HEADS, MLA_NOPE_DIM + MLA_V_DIM)
    wkv = jnp.concatenate([wkv3[:, :, :MLA_NOPE_DIM].reshape(w_ukv.shape[0], -1),
                           wkv3[:, :, MLA_NOPE_DIM:].reshape(w_ukv.shape[0], -1)], axis=1).astype(BF16)
    idx = jnp.arange(MXU_DIM) // SWA_HEAD_DIM
    bd = (idx[:, None] == idx[None, :]).astype(BF16)
    slopes = 2.0 ** (-8.0 * jnp.arange(1, SWA_Q_HEADS + 1, dtype=F32) / SWA_Q_HEADS)
    slope_sink = jnp.stack([slopes, swa_sinks.astype(F32)])

    km, vm = _mem_kv(mem2d, mem_g.reshape(1, -1), w_mem_kv.astype(BF16), memk_g.reshape(1, -1),
                     batch, m_len)
    qa, ka, va, krd, qm, qb, kn, vb = _in_proj(
        h, pos_col, attn_norm_g.reshape(1, -1), sp, w1, wq, wkv, cq_g.reshape(1, -1),
        ckv_g.reshape(1, -1), bd, tm=512)
    ya = _swa_attn(slope_sink, qa, ka, va, pos_col, pos_row, batch, seq, tq=512)
    yb = _mla_attn(qb, kn, krd, vb, batch, seq, tq=512)
    ym = _mem_attn(qm, km, vm, batch, seq, m_len, tq=1024)
    h = _out_proj(h, ya, yb, ym, w_out.astype(BF16), tm=512)
    return _ffn(h, ffn_g.reshape(1, -1), w_gate.astype(BF16), w_up.astype(BF16), w_down.astype(BF16),
                tm=1024, tf=512)


def kernel(x, mem, positions, attn_norm_g, w_in, swa_q_norm_g, swa_k_norm_g, swa_sinks, mla_cq_norm_g,
           mla_ckv_norm_g, w_uq, w_ukv, mla_qn_norm_g, mla_qr_norm_g, mla_kn_norm_g, mla_kr_norm_g,
           mem_norm_g, w_mem_kv, mem_q_norm_g, mem_k_norm_g, w_out, ffn_norm_g, w_gate, w_up, w_down):
    batch, seq, d = x.shape
    m_len = mem.shape[1]
    stacked = (attn_norm_g, w_in, swa_q_norm_g, swa_k_norm_g, swa_sinks, mla_cq_norm_g, mla_ckv_norm_g,
               w_uq, w_ukv, mla_qn_norm_g, mla_qr_norm_g, mla_kn_norm_g, mla_kr_norm_g, mem_norm_g,
               w_mem_kv, mem_q_norm_g, mem_k_norm_g, w_out, ffn_norm_g, w_gate, w_up, w_down)
    h = x.reshape(batch * seq, d)
    mem2d = mem.reshape(batch * m_len, d)
    pos_col = positions.reshape(batch * seq, 1)
    pos_row = positions.reshape(batch, 1, seq)
    for l in range(attn_norm_g.shape[0]):
        h = _layer(h, mem2d, pos_col, pos_row, batch, seq, m_len, tuple(a[l] for a in stacked))
    return h.reshape(batch, seq, d)
```

```python
import functools

import jax
import jax.numpy as jnp
from jax import lax
from jax.experimental import pallas as pl
from jax.experimental.pallas import tpu as pltpu

EPS = 1e-6
NEG_INF = -1e30
LOG2E = 1.4426950408889634
BLOCK = 128
WINDOW = 128

SWA_Q_HEADS = 16
SWA_KV_HEADS = 2
SWA_HEAD_DIM = 64

MLA_HEADS = 4
MLA_NOPE_DIM = 128
MLA_ROPE_DIM = 64
MLA_V_DIM = 128
ROPE_THETA = 10000.0

MEM_HEADS = 4
MEM_HEAD_DIM = 128

LANES = 128
MXU_DIM = 256
VMEM_LIMIT_BYTES = 56 * 1024 * 1024

F32 = jnp.float32
BF16 = jnp.bfloat16

_SP_SWA_Q, _SP_SWA_K, _SP_QN, _SP_QR, _SP_KN, _SP_KR, _SP_MEMQ, _SP_FREQ, _SP_SIGN = range(9)
_SP_ROWS = 16


def _dot(a, b):
    return jnp.dot(a, b, preferred_element_type=F32)


def _dot_nt(a, b):
    return lax.dot_general(a, b, (((1,), (1,)), ((), ())), preferred_element_type=F32)


def _rms(x, g):
    return x * lax.rsqrt(jnp.mean(x * x, axis=-1, keepdims=True) + EPS) * g


def _const_spec(shape):
    nd = len(shape)
    return pl.BlockSpec(shape, lambda *_: (0,) * nd, pipeline_mode=pl.Buffered(1))


def _mem_kv_kernel(mem_ref, g_ref, w_ref, gk_ref, k_ref, v_ref):
    xn = _rms(mem_ref[...], g_ref[...]).astype(BF16)
    kv = _dot(xn, w_ref[...])
    hd = MEM_HEADS * MEM_HEAD_DIM
    for h in range(MEM_HEADS):
        sl = slice(h * MEM_HEAD_DIM, (h + 1) * MEM_HEAD_DIM)
        k_ref[:, sl] = _rms(kv[:, sl], gk_ref[...]).astype(BF16)
    v_ref[...] = kv[:, hd:].astype(BF16)


def _mem_kv(mem2d, g, w, gk, batch, m_len):
    d = mem2d.shape[1]
    hd = MEM_HEADS * MEM_HEAD_DIM
    return pl.pallas_call(
        _mem_kv_kernel,
        grid=(batch,),
        in_specs=[pl.BlockSpec((m_len, d), lambda b: (b, 0)),
                  _const_spec((1, d)), _const_spec((d, 2 * hd)), _const_spec((1, MEM_HEAD_DIM))],
        out_specs=[pl.BlockSpec((m_len, hd), lambda b: (b, 0))] * 2,
        out_shape=[jax.ShapeDtypeStruct((batch * m_len, hd), BF16)] * 2,
        compiler_params=pltpu.CompilerParams(dimension_semantics=("arbitrary",),
                                             vmem_limit_bytes=VMEM_LIMIT_BYTES),
        name="mem_kv",
    )(mem2d, g, w, gk)


_C_QA = (0, 1024)
_C_KVA = (1024, 1280)
_C_CQ = (1280, 1792)
_C_CKV = (1792, 2304)
_C_KR = (2304, 2432)
_C_QM = (2432, 2944)
_W1_COLS = 2944


def _in_proj_kernel(x_ref, pos_ref, gattn_ref, sp_ref, w1_ref, wq_ref, wkv_ref, gcq_ref, gckv_ref,
                    bd_ref, qa_ref, ka_ref, va_ref, krd_ref, qm_ref, qb_ref, kn_ref, vb_ref):
    sp = sp_ref[...]
    row = lambda r, n=LANES: sp[r:r + 1, :n]
    hn = _rms(x_ref[...], gattn_ref[...]).astype(BF16)

    def proj(cols):
        return _dot(hn, w1_ref[:, cols[0]:cols[1]])

    def head64_rms(blk, g):
        ss = _dot((blk * blk).astype(BF16), bd_ref[...])
        return blk * lax.rsqrt(ss * (1.0 / SWA_HEAD_DIM) + EPS) * g

    ang = pos_ref[...].astype(F32) * row(_SP_FREQ)
    cos = jnp.cos(ang)
    sin_signed = jnp.sin(ang) * row(_SP_SIGN)

    def rope_dup(y):
        return y * cos + pltpu.roll(y, MLA_ROPE_DIM // 2, 1) * sin_signed

    qa = proj(_C_QA)
    for c in range(_C_QA[1] // MXU_DIM):
        sl = slice(c * MXU_DIM, (c + 1) * MXU_DIM)
        qa_ref[:, sl] = (head64_rms(qa[:, sl], row(_SP_SWA_Q, MXU_DIM))
                         * (LOG2E * SWA_HEAD_DIM ** -0.5)).astype(BF16)
    kva = proj(_C_KVA)
    lo = lax.broadcasted_iota(jnp.int32, (1, LANES), 1) < SWA_HEAD_DIM
    for src, dst in ((head64_rms(kva, row(_SP_SWA_K, MXU_DIM))[:, :LANES], ka_ref), (kva[:, LANES:], va_ref)):
        swapped = pltpu.roll(src, SWA_HEAD_DIM, 1)
        dst[:, :LANES] = jnp.where(lo, src, swapped).astype(BF16)
        dst[:, LANES:] = jnp.where(lo, swapped, src).astype(BF16)

    kr = proj(_C_KR)
    krd_ref[...] = rope_dup(_rms(kr, row(_SP_KR))).astype(BF16)

    qm = proj(_C_QM)
    for h in range(MEM_HEADS):
        sl = slice(h * MEM_HEAD_DIM, (h + 1) * MEM_HEAD_DIM)
        qm_ref[:, sl] = (_rms(qm[:, sl], row(_SP_MEMQ)) * (MEM_HEAD_DIM ** -0.5)).astype(BF16)

    scale = LOG2E * (MLA_NOPE_DIM + MLA_ROPE_DIM) ** -0.5
    cqn = _rms(proj(_C_CQ), gcq_ref[...]).astype(BF16)
    qb = _dot(cqn, wq_ref[...])
    for h in range(MLA_HEADS):
        o = h * 2 * LANES
        qb_ref[:, o:o + LANES] = (_rms(qb[:, o:o + LANES], row(_SP_QN)) * scale).astype(BF16)
        qr = rope_dup(_rms(qb[:, o + LANES:o + 2 * LANES], row(_SP_QR)))
        qb_ref[:, o + LANES:o + 2 * LANES] = (qr * (0.5 * scale)).astype(BF16)
    ckvn = _rms(proj(_C_CKV), gckv_ref[...]).astype(BF16)
    kvb = _dot(ckvn, wkv_ref[...])
    hn_cols = MLA_HEADS * MLA_NOPE_DIM
    for h in range(MLA_HEADS):
        sl = slice(h * MLA_NOPE_DIM, (h + 1) * MLA_NOPE_DIM)
        kn_ref[:, sl] = _rms(kvb[:, sl], row(_SP_KN)).astype(BF16)
        vb_ref[:, 2 * h * LANES:(2 * h + 1) * LANES] = kvb[:, hn_cols + h * MLA_V_DIM:
                                                          hn_cols + (h + 1) * MLA_V_DIM].astype(BF16)
        vb_ref[:, (2 * h + 1) * LANES:(2 * h + 2) * LANES] = jnp.ones((kvb.shape[0], LANES), BF16)


def _in_proj(x2d, pos_col, gattn, sp, w1, wq, wkv, gcq, gckv, bd, tm):
    t, d = x2d.shape
    widths = (1024, 256, 256, 128, 512, 1024, 512, 1024)
    rows = lambda w: pl.BlockSpec((tm, w), lambda i: (i, 0))
    return pl.pallas_call(
        _in_proj_kernel,
        grid=(t // tm,),
        in_specs=[rows(d), rows(1), _const_spec(gattn.shape), _const_spec(sp.shape),
                  _const_spec(w1.shape), _const_spec(wq.shape), _const_spec(wkv.shape),
                  _const_spec(gcq.shape), _const_spec(gckv.shape), _const_spec(bd.shape)],
        out_specs=[rows(w) for w in widths],
        out_shape=[jax.ShapeDtypeStruct((t, w), BF16) for w in widths],
        compiler_params=pltpu.CompilerParams(dimension_semantics=("arbitrary",),
                                             vmem_limit_bytes=VMEM_LIMIT_BYTES),
        name="in_proj",
    )(x2d, pos_col, gattn, sp, w1, wq, wkv, gcq, gckv, bd)


_SWA_MASK_DIST = -NEG_INF * 2.0 ** 8


def _swa_kernel(ss_ref, q_ref, kc_ref, kp_ref, vc_ref, vp_ref, pq_ref, pkc_ref, pkp_ref, o_ref, *, tq):
    i = pl.program_id(1)
    g = SWA_Q_HEADS // SWA_KV_HEADS
    kk = lax.broadcasted_iota(jnp.int32, (BLOCK, BLOCK), 0)
    qq = lax.broadcasted_iota(jnp.int32, (BLOCK, BLOCK), 1)
    from_prev = kk > qq
    prev_w = jnp.where(from_prev, 1.0, 0.0).astype(BF16)
    cur_w = jnp.where(from_prev, 0.0, 1.0).astype(BF16)
    lane_lo = lax.broadcasted_iota(jnp.int32, (BLOCK, LANES), 1) < SWA_HEAD_DIM
    lo_w = jnp.where(lane_lo, 1.0, 0.0).astype(BF16)
    hi_w = jnp.where(lane_lo, 0.0, 1.0).astype(BF16)

    for j in range(tq // BLOCK):
        rows = slice(j * BLOCK, (j + 1) * BLOCK)
        prows = slice((j - 1) * BLOCK, j * BLOCK)
        k_prev, v_prev, pk_prev = ((kp_ref[...], vp_ref[...], pkp_ref[...]) if j == 0 else
                                   (kc_ref[prows, :], vc_ref[prows, :], pkc_ref[prows, :]))
        kj = jnp.concatenate([k_prev, kc_ref[rows, :]], axis=0)
        vj = jnp.concatenate([v_prev, vc_ref[rows, :]], axis=0)
        pq = pq_ref[:, rows]
        dist = jnp.where(from_prev, jnp.abs(pk_prev - pq), jnp.abs(pkc_ref[rows, :] - pq)).astype(F32)
        if j == 0:
            dist = jnp.where(from_prev, jnp.maximum(dist, jnp.where(i == 0, _SWA_MASK_DIST, 0.0)), dist)
        for kv in range(SWA_KV_HEADS):
            kvl = slice(kv * LANES, (kv + 1) * LANES)
            qm = jnp.concatenate(
                [q_ref[rows, (kv * g + hh) // 2 * LANES:((kv * g + hh) // 2 + 1) * LANES]
                 * (lo_w if hh % 2 == 0 else hi_w) for hh in range(g)], axis=0)
            st = _dot_nt(kj[:, kvl], qm)
            pts = []
            for hh in range(g):
                head = kv * g + hh
                cols = slice(hh * BLOCK, (hh + 1) * BLOCK)
                t = jnp.where(from_prev, st[:BLOCK, cols], st[BLOCK:, cols]) - ss_ref[0, head] * dist
                sink = ss_ref[1, head]
                m = jnp.maximum(jnp.max(t, axis=0, keepdims=True), sink)
                e = jnp.exp2(t - m)
                den = jnp.sum(e, axis=0, keepdims=True) + jnp.exp2(sink - m)
                p = (e * (1.0 / den)).astype(BF16)
                pts.append(jnp.concatenate([p * prev_w, p * cur_w], axis=0))
            pt = jnp.concatenate(pts, axis=1)
            o = lax.dot_general(pt, vj[:, kvl], (((0,), (0,)), ((), ())), preferred_element_type=F32)
            for pr in range(g // 2):
                grp = kv * (g // 2) + pr
                even = o[(2 * pr) * BLOCK:(2 * pr + 1) * BLOCK]
                odd = o[(2 * pr + 1) * BLOCK:(2 * pr + 2) * BLOCK]
                o_ref[rows, grp * LANES:(grp + 1) * LANES] = jnp.where(lane_lo, even, odd).astype(BF16)


def _swa_attn(slope_sink, qa, ka, va, pos_col, pos_row, batch, seq, tq):
    t = batch * seq
    nq = seq // tq
    nb = tq // BLOCK
    cur = lambda w: pl.BlockSpec((tq, w), lambda b, i: (b * nq + i, 0))
    prev = lambda w: pl.BlockSpec(
        (BLOCK, w), lambda b, i: (b * nq * nb + jnp.maximum(i * nb - 1, 0), 0))
    return pl.pallas_call(
        functools.partial(_swa_kernel, tq=tq),
        grid=(batch, nq),
        in_specs=[pl.BlockSpec(memory_space=pltpu.SMEM),
                  cur(SWA_Q_HEADS * SWA_HEAD_DIM), cur(2 * LANES), prev(2 * LANES), cur(2 * LANES),
                  prev(2 * LANES),
                  pl.BlockSpec((None, 1, tq), lambda b, i: (b, 0, i)),
                  cur(1), prev(1)],
        out_specs=cur(SWA_Q_HEADS * SWA_HEAD_DIM),
        out_shape=jax.ShapeDtypeStruct((t, SWA_Q_HEADS * SWA_HEAD_DIM), BF16),
        compiler_params=pltpu.CompilerParams(dimension_semantics=("arbitrary", "arbitrary"),
                                             vmem_limit_bytes=VMEM_LIMIT_BYTES),
        name="swa_attn",
    )(slope_sink, qa, ka, ka, va, va, pos_row, pos_col, pos_col)


def _mla_kernel(q_ref, kn_ref, kr_ref, v_ref, o_ref, s_sc, m_sc, acc_sc, *, tq, hps):
    qi = pl.program_id(2)
    qw = 2 * LANES

    def scores_to(slot, j):
        k0 = pl.multiple_of(j * tq, tq)
        kr = kr_ref[pl.ds(k0, tq), :]
        for hh in range(hps):
            k = jnp.concatenate([kn_ref[pl.ds(k0, tq), hh * LANES:(hh + 1) * LANES], kr], axis=1)
            s_sc[slot, hh] = _dot_nt(q_ref[:, hh * qw:(hh + 1) * qw], k)

    def update_from(slot, j, diagonal):
        k0 = pl.multiple_of(j * tq, tq)
        for hh in range(hps):
            s = s_sc[slot, hh]
            if diagonal:
                r = lax.broadcasted_iota(jnp.int32, (tq, tq), 0)
                c = lax.broadcasted_iota(jnp.int32, (tq, tq), 1)
                s = jnp.where(c <= r, s, NEG_INF)
            m = m_sc[hh]
            m_new = jnp.maximum(m, jnp.max(s, axis=-1, keepdims=True))
            p = jnp.exp2(s - m_new)
            m_sc[hh] = m_new
            acc_sc[hh] = jnp.exp2(m - m_new) * acc_sc[hh] + _dot(
                p.astype(BF16), v_ref[pl.ds(k0, tq), hh * qw:(hh + 1) * qw])

    m_sc[...] = jnp.full(m_sc.shape, NEG_INF, F32)
    acc_sc[...] = jnp.zeros(acc_sc.shape, F32)
    scores_to(0, 0)

    def pair(p, carry):
        j = 2 * p
        scores_to(1, j + 1)
        update_from(0, j, False)
        scores_to(0, j + 2)
        update_from(1, j + 1, False)
        return carry

    lax.fori_loop(0, qi // 2, pair, 0)
    odd = qi % 2 == 1

    @pl.when(odd)
    def _():
        scores_to(1, qi)
        update_from(0, qi - 1, False)
        update_from(1, qi, True)

    @pl.when(jnp.logical_not(odd))
    def _():
        update_from(0, qi, True)

    for hh in range(hps):
        acc = acc_sc[hh]
        o_ref[:, hh * MLA_V_DIM:(hh + 1) * MLA_V_DIM] = (acc[:, :MLA_V_DIM] / acc[:, MLA_V_DIM:]).astype(BF16)


def _mla_attn(qb, kn, krd, vb, batch, seq, tq, hps):
    t = batch * seq
    nq = seq // tq
    return pl.pallas_call(
        functools.partial(_mla_kernel, tq=tq, hps=hps),
        grid=(batch, MLA_HEADS // hps, nq),
        in_specs=[pl.BlockSpec((tq, 2 * LANES * hps), lambda b, h, i: (b * nq + i, h)),
                  pl.BlockSpec((seq, MLA_NOPE_DIM * hps), lambda b, h, i: (b, h)),
                  pl.BlockSpec((seq, LANES), lambda b, h, i: (b, 0)),
                  pl.BlockSpec((seq, 2 * LANES * hps), lambda b, h, i: (b, h))],
        out_specs=pl.BlockSpec((tq, MLA_V_DIM * hps), lambda b, h, i: (b * nq + i, h)),
        out_shape=jax.ShapeDtypeStruct((t, MLA_HEADS * MLA_V_DIM), BF16),
        scratch_shapes=[pltpu.VMEM((2, hps, tq, tq), F32), pltpu.VMEM((hps, tq, 1), F32),
                        pltpu.VMEM((hps, tq, 2 * LANES), F32)],
        compiler_params=pltpu.CompilerParams(dimension_semantics=("arbitrary",) * 3,
                                             vmem_limit_bytes=VMEM_LIMIT_BYTES),
        name="mla_attn",
    )(qb, kn, krd, vb)


def _mem_attn_kernel(q_ref, k_ref, v_ref, o_ref):
    for h in range(MEM_HEADS):
        sl = slice(h * MEM_HEAD_DIM, (h + 1) * MEM_HEAD_DIM)
        s = _dot_nt(q_ref[:, sl], k_ref[:, sl])
        e = jnp.exp(s - jnp.max(s, axis=-1, keepdims=True))
        o = _dot(e.astype(BF16), v_ref[:, sl]) / jnp.sum(e, axis=-1, keepdims=True)
        o_ref[:, sl] = o.astype(BF16)


def _mem_attn(qm, km, vm, batch, seq, m_len, tq):
    t = batch * seq
    nq = seq // tq
    hd = MEM_HEADS * MEM_HEAD_DIM
    return pl.pallas_call(
        _mem_attn_kernel,
        grid=(batch, nq),
        in_specs=[pl.BlockSpec((tq, hd), lambda b, i: (b * nq + i, 0)),
                  pl.BlockSpec((m_len, hd), lambda b, i: (b, 0)),
                  pl.BlockSpec((m_len, hd), lambda b, i: (b, 0))],
        out_specs=pl.BlockSpec((tq, hd), lambda b, i: (b * nq + i, 0)),
        out_shape=jax.ShapeDtypeStruct((t, hd), BF16),
        compiler_params=pltpu.CompilerParams(dimension_semantics=("arbitrary", "arbitrary"),
                                             vmem_limit_bytes=VMEM_LIMIT_BYTES),
        name="mem_attn",
    )(qm, km, vm)


def _out_proj_kernel(x_ref, ya_ref, yb_ref, ym_ref, w_ref, h_ref):
    na = ya_ref.shape[1]
    nb = na + yb_ref.shape[1]
    h_ref[...] = (x_ref[...] + _dot(ya_ref[...], w_ref[0:na, :]) + _dot(yb_ref[...], w_ref[na:nb, :])
                  + _dot(ym_ref[...], w_ref[nb:, :]))


def _out_proj(x2d, ya, yb, ym, w_out, tm):
    t, d = x2d.shape
    rows = lambda w: pl.BlockSpec((tm, w), lambda i: (i, 0))
    return pl.pallas_call(
        _out_proj_kernel,
        grid=(t // tm,),
        in_specs=[rows(d), rows(ya.shape[1]), rows(yb.shape[1]), rows(ym.shape[1]),
                  _const_spec(w_out.shape)],
        out_specs=rows(d),
        out_shape=jax.ShapeDtypeStruct((t, d), F32),
        compiler_params=pltpu.CompilerParams(dimension_semantics=("arbitrary",),
                                             vmem_limit_bytes=VMEM_LIMIT_BYTES),
        name="out_proj",
    )(x2d, ya, yb, ym, w_out)


def _ffn_kernel(h_ref, g_ref, wg_ref, wu_ref, wd_ref, o_ref, fn_ref):
    j = pl.program_id(1)

    @pl.when(j == 0)
    def _():
        h = h_ref[...]
        fn_ref[...] = _rms(h, g_ref[...]).astype(BF16)
        o_ref[...] = h

    fn = fn_ref[...]
    gate = _dot(fn, wg_ref[...])
    up = _dot(fn, wu_ref[...])
    act = (gate * jax.nn.sigmoid(gate) * up).astype(BF16)
    o_ref[...] += _dot(act, wd_ref[...])


def _ffn(h, g, wg, wu, wd, tm, tf):
    t, d = h.shape
    dff = wg.shape[1]
    return pl.pallas_call(
        _ffn_kernel,
        grid=(t // tm, dff // tf),
        in_specs=[pl.BlockSpec((tm, d), lambda i, j: (i, 0), pipeline_mode=pl.Buffered(1)),
                  _const_spec((1, d)),
                  pl.BlockSpec((d, tf), lambda i, j: (0, j)),
                  pl.BlockSpec((d, tf), lambda i, j: (0, j)),
                  pl.BlockSpec((tf, d), lambda i, j: (j, 0))],
        out_specs=pl.BlockSpec((tm, d), lambda i, j: (i, 0)),
        out_shape=jax.ShapeDtypeStruct((t, d), F32),
        scratch_shapes=[pltpu.VMEM((tm, d), BF16)],
        compiler_params=pltpu.CompilerParams(dimension_semantics=("arbitrary", "arbitrary"),
                                             vmem_limit_bytes=VMEM_LIMIT_BYTES),
        name="ffn",
    )(h, g, wg, wu, wd)


def _tile_row(v, width):
    v = v.astype(F32).reshape(-1)
    return jnp.tile(v, width // v.shape[0])


def _layer(h, mem2d, pos_col, pos_row, batch, seq, m_len, p):
    (attn_norm_g, w_in, swa_q_g, swa_k_g, swa_sinks, cq_g, ckv_g, w_uq, w_ukv, qn_g, qr_g, kn_g, kr_g,
     mem_g, w_mem_kv, memq_g, memk_g, w_out, ffn_g, w_gate, w_up, w_down) = p
    width = 2 * LANES
    inv_freq = ROPE_THETA ** (-jnp.arange(0, MLA_ROPE_DIM, 2, dtype=F32) / MLA_ROPE_DIM)
    sign = jnp.concatenate([-jnp.ones((MLA_ROPE_DIM // 2,), F32), jnp.ones((MLA_ROPE_DIM // 2,), F32)])
    rows = [swa_q_g, swa_k_g, qn_g, qr_g, kn_g, kr_g, memq_g, inv_freq, sign]
    sp = jnp.stack([_tile_row(r, width) for r in rows]
                   + [jnp.zeros((width,), F32)] * (_SP_ROWS - len(rows)))

    kr0 = _C_KR[0]
    w1 = jnp.concatenate([w_in[:, :kr0 + MLA_ROPE_DIM], w_in[:, kr0:kr0 + MLA_ROPE_DIM],
                          w_in[:, kr0 + MLA_ROPE_DIM:]], axis=1).astype(BF16)
    qd = MLA_NOPE_DIM + MLA_ROPE_DIM
    wq3 = w_uq.reshape(w_uq.shape[0], MLA_HEADS, qd)
    wq = jnp.concatenate([wq3, wq3[:, :, MLA_NOPE_DIM:]], axis=2).reshape(w_uq.shape[0], -1).astype(BF16)
    wkv3 = w_ukv.reshape(w_ukv.shape[0], MLA_HEADS, MLA_NOPE_DIM + MLA_V_DIM)
    wkv = jnp.concatenate([wkv3[:, :, :MLA_NOPE_DIM].reshape(w_ukv.shape[0], -1),
                           wkv3[:, :, MLA_NOPE_DIM:].reshape(w_ukv.shape[0], -1)], axis=1).astype(BF16)
    idx = jnp.arange(MXU_DIM) // SWA_HEAD_DIM
    bd = (idx[:, None] == idx[None, :]).astype(BF16)
    slopes = 2.0 ** (-8.0 * jnp.arange(1, SWA_Q_HEADS + 1, dtype=F32) / SWA_Q_HEADS)
    slope_sink = jnp.stack([slopes, swa_sinks.astype(F32)]) * LOG2E

    km, vm = _mem_kv(mem2d, mem_g.reshape(1, -1), w_mem_kv.astype(BF16), memk_g.reshape(1, -1),
                     batch, m_len)
    qa, ka, va, krd, qm, qb, kn, vb = _in_proj(
        h, pos_col, attn_norm_g.reshape(1, -1), sp, w1, wq, wkv, cq_g.reshape(1, -1),
        ckv_g.reshape(1, -1), bd, tm=512)
    ya = _swa_attn(slope_sink, qa, ka, va, pos_col, pos_row, batch, seq, tq=512)
    yb = _mla_attn(qb, kn, krd, vb, batch, seq, tq=512, hps=2)
    ym = _mem_attn(qm, km, vm, batch, seq, m_len, tq=1024)
    h = _out_proj(h, ya, yb, ym, w_out.astype(BF16), tm=512)
    return _ffn(h, ffn_g.reshape(1, -1), w_gate.astype(BF16), w_up.astype(BF16), w_down.astype(BF16),
                tm=1024, tf=512)


def kernel(x, mem, positions, attn_norm_g, w_in, swa_q_norm_g, swa_k_norm_g, swa_sinks, mla_cq_norm_g,
           mla_ckv_norm_g, w_uq, w_ukv, mla_qn_norm_g, mla_qr_norm_g, mla_kn_norm_g, mla_kr_norm_g,
           mem_norm_g, w_mem_kv, mem_q_norm_g, mem_k_norm_g, w_out, ffn_norm_g, w_gate, w_up, w_down):
    batch, seq, d = x.shape
    m_len = mem.shape[1]
    stacked = (attn_norm_g, w_in, swa_q_norm_g, swa_k_norm_g, swa_sinks, mla_cq_norm_g, mla_ckv_norm_g,
               w_uq, w_ukv, mla_qn_norm_g, mla_qr_norm_g, mla_kn_norm_g, mla_kr_norm_g, mem_norm_g,
               w_mem_kv, mem_q_norm_g, mem_k_norm_g, w_out, ffn_norm_g, w_gate, w_up, w_down)
    h = x.reshape(batch * seq, d)
    mem2d = mem.reshape(batch * m_len, d)
    pos_col = positions.reshape(batch * seq, 1)
    pos_row = positions.reshape(batch, 1, seq)
    for l in range(attn_norm_g.shape[0]):
        h = _layer(h, mem2d, pos_col, pos_row, batch, seq, m_len, tuple(a[l] for a in stacked))
    return h.reshape(batch, seq, d)
```

```python
import functools

import jax
import jax.numpy as jnp
from jax import lax
from jax.experimental import pallas as pl
from jax.experimental.pallas import tpu as pltpu

EPS = 1e-6
NEG_INF = -1e30
LOG2E = 1.4426950408889634
BLOCK = 128
WINDOW = 128

SWA_Q_HEADS = 16
SWA_KV_HEADS = 2
SWA_HEAD_DIM = 64

MLA_HEADS = 4
MLA_NOPE_DIM = 128
MLA_ROPE_DIM = 64
MLA_V_DIM = 128
ROPE_THETA = 10000.0

MEM_HEADS = 4
MEM_HEAD_DIM = 128

LANES = 128
MXU_DIM = 256
VMEM_LIMIT_BYTES = 56 * 1024 * 1024

F32 = jnp.float32
BF16 = jnp.bfloat16

_SP_SWA_Q, _SP_SWA_K, _SP_QN, _SP_QR, _SP_KN, _SP_KR, _SP_MEMQ, _SP_FREQ, _SP_SIGN = range(9)
_SP_ROWS = 16


def _dot(a, b):
    return jnp.dot(a, b, preferred_element_type=F32)


def _dot_nt(a, b):
    return lax.dot_general(a, b, (((1,), (1,)), ((), ())), preferred_element_type=F32)


def _rms(x, g):
    return x * lax.rsqrt(jnp.mean(x * x, axis=-1, keepdims=True) + EPS) * g


def _const_spec(shape):
    nd = len(shape)
    return pl.BlockSpec(shape, lambda *_: (0,) * nd, pipeline_mode=pl.Buffered(1))


def _mem_kv_kernel(mem_ref, g_ref, w_ref, gk_ref, k_ref, v_ref):
    xn = _rms(mem_ref[...], g_ref[...]).astype(BF16)
    kv = _dot(xn, w_ref[...])
    hd = MEM_HEADS * MEM_HEAD_DIM
    for h in range(MEM_HEADS):
        sl = slice(h * MEM_HEAD_DIM, (h + 1) * MEM_HEAD_DIM)
        k_ref[:, sl] = _rms(kv[:, sl], gk_ref[...]).astype(BF16)
    v_ref[...] = kv[:, hd:].astype(BF16)


def _mem_kv(mem2d, g, w, gk, batch, m_len):
    d = mem2d.shape[1]
    hd = MEM_HEADS * MEM_HEAD_DIM
    return pl.pallas_call(
        _mem_kv_kernel,
        grid=(batch,),
        in_specs=[pl.BlockSpec((m_len, d), lambda b: (b, 0)),
                  _const_spec((1, d)), _const_spec((d, 2 * hd)), _const_spec((1, MEM_HEAD_DIM))],
        out_specs=[pl.BlockSpec((m_len, hd), lambda b: (b, 0))] * 2,
        out_shape=[jax.ShapeDtypeStruct((batch * m_len, hd), BF16)] * 2,
        compiler_params=pltpu.CompilerParams(dimension_semantics=("arbitrary",),
                                             vmem_limit_bytes=VMEM_LIMIT_BYTES),
        name="mem_kv",
    )(mem2d, g, w, gk)


_C_QA = (0, 1024)
_C_KVA = (1024, 1280)
_C_CQ = (1280, 1792)
_C_CKV = (1792, 2304)
_C_KR = (2304, 2432)
_C_QM = (2432, 2944)
_W1_COLS = 2944


def _in_proj_kernel(x_ref, pos_ref, gattn_ref, sp_ref, w1_ref, wq_ref, wkv_ref, gcq_ref, gckv_ref,
                    bd_ref, wg32_ref, wu32_ref, wd32_ref,
                    qa_ref, ka_ref, va_ref, krd_ref, qm_ref, qb_ref, kn_ref, vb_ref,
                    wg16_ref, wu16_ref, wd16_ref):
    wg16_ref[...] = wg32_ref[...].astype(BF16)
    wu16_ref[...] = wu32_ref[...].astype(BF16)
    wd16_ref[...] = wd32_ref[...].astype(BF16)
    sp = sp_ref[...]
    row = lambda r, n=LANES: sp[r:r + 1, :n]
    hn = _rms(x_ref[...], gattn_ref[...]).astype(BF16)

    def proj(cols):
        return _dot(hn, w1_ref[:, cols[0]:cols[1]])

    def head64_rms(blk, g):
        ss = _dot((blk * blk).astype(BF16), bd_ref[...])
        return blk * lax.rsqrt(ss * (1.0 / SWA_HEAD_DIM) + EPS) * g

    ang = pos_ref[...].astype(F32) * row(_SP_FREQ)
    cos = jnp.cos(ang)
    sin_signed = jnp.sin(ang) * row(_SP_SIGN)

    def rope_dup(y):
        return y * cos + pltpu.roll(y, MLA_ROPE_DIM // 2, 1) * sin_signed

    cq = proj(_C_CQ)
    ckv = proj(_C_CKV)
    qa = proj(_C_QA)
    cqn = _rms(cq, gcq_ref[...]).astype(BF16)
    ckvn = _rms(ckv, gckv_ref[...]).astype(BF16)
    qb = _dot(cqn, wq_ref[...])
    kvb = _dot(ckvn, wkv_ref[...])

    for c in range(_C_QA[1] // MXU_DIM):
        sl = slice(c * MXU_DIM, (c + 1) * MXU_DIM)
        qa_ref[:, sl] = (head64_rms(qa[:, sl], row(_SP_SWA_Q, MXU_DIM))
                         * (LOG2E * SWA_HEAD_DIM ** -0.5)).astype(BF16)
    qm = proj(_C_QM)
    kva = proj(_C_KVA)
    kr = proj(_C_KR)

    scale = LOG2E * (MLA_NOPE_DIM + MLA_ROPE_DIM) ** -0.5
    for h in range(MLA_HEADS):
        o = h * 2 * LANES
        qb_ref[:, o:o + LANES] = (_rms(qb[:, o:o + LANES], row(_SP_QN)) * scale).astype(BF16)
        qr = rope_dup(_rms(qb[:, o + LANES:o + 2 * LANES], row(_SP_QR)))
        qb_ref[:, o + LANES:o + 2 * LANES] = (qr * (0.5 * scale)).astype(BF16)
    hn_cols = MLA_HEADS * MLA_NOPE_DIM
    for h in range(MLA_HEADS):
        sl = slice(h * MLA_NOPE_DIM, (h + 1) * MLA_NOPE_DIM)
        kn_ref[:, sl] = _rms(kvb[:, sl], row(_SP_KN)).astype(BF16)
        vb_ref[:, 2 * h * LANES:(2 * h + 1) * LANES] = kvb[:, hn_cols + h * MLA_V_DIM:
                                                          hn_cols + (h + 1) * MLA_V_DIM].astype(BF16)
        vb_ref[:, (2 * h + 1) * LANES:(2 * h + 2) * LANES] = jnp.ones((kvb.shape[0], LANES), BF16)

    for h in range(MEM_HEADS):
        sl = slice(h * MEM_HEAD_DIM, (h + 1) * MEM_HEAD_DIM)
        qm_ref[:, sl] = (_rms(qm[:, sl], row(_SP_MEMQ)) * (MEM_HEAD_DIM ** -0.5)).astype(BF16)
    lo = lax.broadcasted_iota(jnp.int32, (1, LANES), 1) < SWA_HEAD_DIM
    for src, dst in ((head64_rms(kva, row(_SP_SWA_K, MXU_DIM))[:, :LANES], ka_ref), (kva[:, LANES:], va_ref)):
        swapped = pltpu.roll(src, SWA_HEAD_DIM, 1)
        dst[:, :LANES] = jnp.where(lo, src, swapped).astype(BF16)
        dst[:, LANES:] = jnp.where(lo, swapped, src).astype(BF16)
    krd_ref[...] = rope_dup(_rms(kr, row(_SP_KR))).astype(BF16)


def _in_proj(x2d, pos_col, gattn, sp, w1, wq, wkv, gcq, gckv, bd, ffn_w, tm):
    t, d = x2d.shape
    steps = t // tm
    widths = (1024, 256, 256, 128, 512, 1024, 512, 1024)
    rows = lambda w: pl.BlockSpec((tm, w), lambda i: (i, 0))
    slabs = []
    for w in ffn_w:
        assert w.shape[0] % (16 * steps) == 0, (w.shape, steps)
        slabs.append(pl.BlockSpec((w.shape[0] // steps, w.shape[1]), lambda i: (i, 0)))
    return pl.pallas_call(
        _in_proj_kernel,
        grid=(steps,),
        in_specs=[rows(d), rows(1), _const_spec(gattn.shape), _const_spec(sp.shape),
                  _const_spec(w1.shape), _const_spec(wq.shape), _const_spec(wkv.shape),
                  _const_spec(gcq.shape), _const_spec(gckv.shape), _const_spec(bd.shape)] + slabs,
        out_specs=[rows(w) for w in widths] + slabs,
        out_shape=[jax.ShapeDtypeStruct((t, w), BF16) for w in widths]
        + [jax.ShapeDtypeStruct(w.shape, BF16) for w in ffn_w],
        compiler_params=pltpu.CompilerParams(dimension_semantics=("arbitrary",),
                                             vmem_limit_bytes=VMEM_LIMIT_BYTES),
        name="in_proj",
    )(x2d, pos_col, gattn, sp, w1, wq, wkv, gcq, gckv, bd, *ffn_w)


_SWA_MASK_DIST = -NEG_INF * 2.0 ** 8


def _swa_kernel(ss_ref, q_ref, kc_ref, kp_ref, vc_ref, vp_ref, pq_ref, pkc_ref, pkp_ref, o_ref, *, tq):
    i = pl.program_id(1)
    g = SWA_Q_HEADS // SWA_KV_HEADS
    kk = lax.broadcasted_iota(jnp.int32, (BLOCK, BLOCK), 0)
    qq = lax.broadcasted_iota(jnp.int32, (BLOCK, BLOCK), 1)
    from_prev = kk > qq
    prev_w = jnp.where(from_prev, 1.0, 0.0).astype(BF16)
    cur_w = jnp.where(from_prev, 0.0, 1.0).astype(BF16)
    lane_lo = lax.broadcasted_iota(jnp.int32, (BLOCK, LANES), 1) < SWA_HEAD_DIM
    lo_w = jnp.where(lane_lo, 1.0, 0.0).astype(BF16)
    hi_w = jnp.where(lane_lo, 0.0, 1.0).astype(BF16)

    for j in range(tq // BLOCK):
        rows = slice(j * BLOCK, (j + 1) * BLOCK)
        prows = slice((j - 1) * BLOCK, j * BLOCK)
        k_prev, v_prev, pk_prev = ((kp_ref[...], vp_ref[...], pkp_ref[...]) if j == 0 else
                                   (kc_ref[prows, :], vc_ref[prows, :], pkc_ref[prows, :]))
        kj = jnp.concatenate([k_prev, kc_ref[rows, :]], axis=0)
        vj = jnp.concatenate([v_prev, vc_ref[rows, :]], axis=0)
        pq = pq_ref[:, rows]
        dist = jnp.where(from_prev, jnp.abs(pk_prev - pq), jnp.abs(pkc_ref[rows, :] - pq)).astype(F32)
        if j == 0:
            dist = jnp.where(from_prev, jnp.maximum(dist, jnp.where(i == 0, _SWA_MASK_DIST, 0.0)), dist)
        for kv in range(SWA_KV_HEADS):
            kvl = slice(kv * LANES, (kv + 1) * LANES)
            qm = jnp.concatenate(
                [q_ref[rows, (kv * g + hh) // 2 * LANES:((kv * g + hh) // 2 + 1) * LANES]
                 * (lo_w if hh % 2 == 0 else hi_w) for hh in range(g)], axis=0)
            st = _dot_nt(kj[:, kvl], qm)
            pts = []
            for hh in range(g):
                head = kv * g + hh
                cols = slice(hh * BLOCK, (hh + 1) * BLOCK)
                t = jnp.where(from_prev, st[:BLOCK, cols], st[BLOCK:, cols]) - ss_ref[0, head] * dist
                sink = ss_ref[1, head]
                m = jnp.maximum(jnp.max(t, axis=0, keepdims=True), sink)
                e = jnp.exp2(t - m)
                den = jnp.sum(e, axis=0, keepdims=True) + jnp.exp2(sink - m)
                p = (e * (1.0 / den)).astype(BF16)
                pts.append(jnp.concatenate([p * prev_w, p * cur_w], axis=0))
            pt = jnp.concatenate(pts, axis=1)
            o = lax.dot_general(pt, vj[:, kvl], (((0,), (0,)), ((), ())), preferred_element_type=F32)
            for pr in range(g // 2):
                grp = kv * (g // 2) + pr
                even = o[(2 * pr) * BLOCK:(2 * pr + 1) * BLOCK]
                odd = o[(2 * pr + 1) * BLOCK:(2 * pr + 2) * BLOCK]
                o_ref[rows, grp * LANES:(grp + 1) * LANES] = jnp.where(lane_lo, even, odd).astype(BF16)


def _swa_attn(slope_sink, qa, ka, va, pos_col, pos_row, batch, seq, tq):
    t = batch * seq
    nq = seq // tq
    nb = tq // BLOCK
    cur = lambda w: pl.BlockSpec((tq, w), lambda b, i: (b * nq + i, 0))
    prev = lambda w: pl.BlockSpec(
        (BLOCK, w), lambda b, i: (b * nq * nb + jnp.maximum(i * nb - 1, 0), 0))
    return pl.pallas_call(
        functools.partial(_swa_kernel, tq=tq),
        grid=(batch, nq),
        in_specs=[pl.BlockSpec(memory_space=pltpu.SMEM),
                  cur(SWA_Q_HEADS * SWA_HEAD_DIM), cur(2 * LANES), prev(2 * LANES), cur(2 * LANES),
                  prev(2 * LANES),
                  pl.BlockSpec((None, 1, tq), lambda b, i: (b, 0, i)),
                  cur(1), prev(1)],
        out_specs=cur(SWA_Q_HEADS * SWA_HEAD_DIM),
        out_shape=jax.ShapeDtypeStruct((t, SWA_Q_HEADS * SWA_HEAD_DIM), BF16),
        compiler_params=pltpu.CompilerParams(dimension_semantics=("arbitrary", "arbitrary"),
                                             vmem_limit_bytes=VMEM_LIMIT_BYTES),
        name="swa_attn",
    )(slope_sink, qa, ka, ka, va, va, pos_row, pos_col, pos_col)


def _mla_kernel(q_ref, kn_ref, kr_ref, v_ref, o_ref, s_sc, m_sc, acc_sc, *, tq, hps):
    qi = pl.program_id(2)
    qw = 2 * LANES

    def scores_to(slot, j):
        k0 = pl.multiple_of(j * tq, tq)
        kr = kr_ref[pl.ds(k0, tq), :]
        for hh in range(hps):
            k = jnp.concatenate([kn_ref[pl.ds(k0, tq), hh * LANES:(hh + 1) * LANES], kr], axis=1)
            s_sc[slot, hh] = _dot_nt(q_ref[:, hh * qw:(hh + 1) * qw], k)

    def update_from(slot, j, diagonal):
        k0 = pl.multiple_of(j * tq, tq)
        for hh in range(hps):
            s = s_sc[slot, hh]
            if diagonal:
                r = lax.broadcasted_iota(jnp.int32, (tq, tq), 0)
                c = lax.broadcasted_iota(jnp.int32, (tq, tq), 1)
                s = jnp.where(c <= r, s, NEG_INF)
            m = m_sc[hh]
            m_new = jnp.maximum(m, jnp.max(s, axis=-1, keepdims=True))
            p = jnp.exp2(s - m_new)
            m_sc[hh] = m_new
            acc_sc[hh] = jnp.exp2(m - m_new) * acc_sc[hh] + _dot(
                p.astype(BF16), v_ref[pl.ds(k0, tq), hh * qw:(hh + 1) * qw])

    m_sc[...] = jnp.full(m_sc.shape, NEG_INF, F32)
    acc_sc[...] = jnp.zeros(acc_sc.shape, F32)
    scores_to(0, 0)

    def pair(p, carry):
        j = 2 * p
        scores_to(1, j + 1)
        update_from(0, j, False)
        scores_to(0, j + 2)
        update_from(1, j + 1, False)
        return carry

    lax.fori_loop(0, qi // 2, pair, 0)
    odd = qi % 2 == 1

    @pl.when(odd)
    def _():
        scores_to(1, qi)
        update_from(0, qi - 1, False)
        update_from(1, qi, True)

    @pl.when(jnp.logical_not(odd))
    def _():
        update_from(0, qi, True)

    for hh in range(hps):
        acc = acc_sc[hh]
        o_ref[:, hh * MLA_V_DIM:(hh + 1) * MLA_V_DIM] = (acc[:, :MLA_V_DIM] / acc[:, MLA_V_DIM:]).astype(BF16)


def _mla_attn(qb, kn, krd, vb, batch, seq, tq, hps):
    t = batch * seq
    nq = seq // tq
    return pl.pallas_call(
        functools.partial(_mla_kernel, tq=tq, hps=hps),
        grid=(batch, MLA_HEADS // hps, nq),
        in_specs=[pl.BlockSpec((tq, 2 * LANES * hps), lambda b, h, i: (b * nq + i, h)),
                  pl.BlockSpec((seq, MLA_NOPE_DIM * hps), lambda b, h, i: (b, h)),
                  pl.BlockSpec((seq, LANES), lambda b, h, i: (b, 0)),
                  pl.BlockSpec((seq, 2 * LANES * hps), lambda b, h, i: (b, h))],
        out_specs=pl.BlockSpec((tq, MLA_V_DIM * hps), lambda b, h, i: (b * nq + i, h)),
        out_shape=jax.ShapeDtypeStruct((t, MLA_HEADS * MLA_V_DIM), BF16),
        scratch_shapes=[pltpu.VMEM((2, hps, tq, tq), F32), pltpu.VMEM((hps, tq, 1), F32),
                        pltpu.VMEM((hps, tq, 2 * LANES), F32)],
        compiler_params=pltpu.CompilerParams(dimension_semantics=("arbitrary",) * 3,
                                             vmem_limit_bytes=VMEM_LIMIT_BYTES),
        name="mla_attn",
    )(qb, kn, krd, vb)


def _mem_attn_kernel(q_ref, k_ref, v_ref, o_ref):
    for h in range(MEM_HEADS):
        sl = slice(h * MEM_HEAD_DIM, (h + 1) * MEM_HEAD_DIM)
        s = _dot_nt(q_ref[:, sl], k_ref[:, sl])
        e = jnp.exp(s - jnp.max(s, axis=-1, keepdims=True))
        o = _dot(e.astype(BF16), v_ref[:, sl]) / jnp.sum(e, axis=-1, keepdims=True)
        o_ref[:, sl] = o.astype(BF16)


def _mem_attn(qm, km, vm, batch, seq, m_len, tq):
    t = batch * seq
    nq = seq // tq
    hd = MEM_HEADS * MEM_HEAD_DIM
    return pl.pallas_call(
        _mem_attn_kernel,
        grid=(batch, nq),
        in_specs=[pl.BlockSpec((tq, hd), lambda b, i: (b * nq + i, 0)),
                  pl.BlockSpec((m_len, hd), lambda b, i: (b, 0)),
                  pl.BlockSpec((m_len, hd), lambda b, i: (b, 0))],
        out_specs=pl.BlockSpec((tq, hd), lambda b, i: (b * nq + i, 0)),
        out_shape=jax.ShapeDtypeStruct((t, hd), BF16),
        compiler_params=pltpu.CompilerParams(dimension_semantics=("arbitrary", "arbitrary"),
                                             vmem_limit_bytes=VMEM_LIMIT_BYTES),
        name="mem_attn",
    )(qm, km, vm)


def _out_proj_kernel(x_ref, ya_ref, yb_ref, ym_ref, w_ref, h_ref):
    na = ya_ref.shape[1]
    nb = na + yb_ref.shape[1]
    h_ref[...] = (x_ref[...] + _dot(ya_ref[...], w_ref[0:na, :]) + _dot(yb_ref[...], w_ref[na:nb, :])
                  + _dot(ym_ref[...], w_ref[nb:, :]))


def _out_proj(x2d, ya, yb, ym, w_out, tm):
    t, d = x2d.shape
    rows = lambda w: pl.BlockSpec((tm, w), lambda i: (i, 0))
    return pl.pallas_call(
        _out_proj_kernel,
        grid=(t // tm,),
        in_specs=[rows(d), rows(ya.shape[1]), rows(yb.shape[1]), rows(ym.shape[1]),
                  _const_spec(w_out.shape)],
        out_specs=rows(d),
        out_shape=jax.ShapeDtypeStruct((t, d), F32),
        compiler_params=pltpu.CompilerParams(dimension_semantics=("arbitrary",),
                                             vmem_limit_bytes=VMEM_LIMIT_BYTES),
        name="out_proj",
    )(x2d, ya, yb, ym, w_out)


def _ffn_kernel(h_ref, g_ref, wg_ref, wu_ref, wd_ref, o_ref, fn_ref):
    j = pl.program_id(1)

    @pl.when(j == 0)
    def _():
        h = h_ref[...]
        fn_ref[...] = _rms(h, g_ref[...]).astype(BF16)
        o_ref[...] = h

    fn = fn_ref[...]
    gate = _dot(fn, wg_ref[...])
    up = _dot(fn, wu_ref[...])
    act = (gate * jax.nn.sigmoid(gate) * up).astype(BF16)
    o_ref[...] += _dot(act, wd_ref[...])


def _ffn(h, g, wg, wu, wd, tm, tf):
    t, d = h.shape
    dff = wg.shape[1]
    return pl.pallas_call(
        _ffn_kernel,
        grid=(t // tm, dff // tf),
        in_specs=[pl.BlockSpec((tm, d), lambda i, j: (i, 0)),
                  _const_spec((1, d)),
                  pl.BlockSpec((d, tf), lambda i, j: (0, j)),
                  pl.BlockSpec((d, tf), lambda i, j: (0, j)),
                  pl.BlockSpec((tf, d), lambda i, j: (j, 0))],
        out_specs=pl.BlockSpec((tm, d), lambda i, j: (i, 0)),
        out_shape=jax.ShapeDtypeStruct((t, d), F32),
        scratch_shapes=[pltpu.VMEM((tm, d), BF16)],
        compiler_params=pltpu.CompilerParams(dimension_semantics=("arbitrary", "arbitrary"),
                                             vmem_limit_bytes=VMEM_LIMIT_BYTES),
        name="ffn",
    )(h, g, wg, wu, wd)


def _tile_row(v, width):
    v = v.astype(F32).reshape(-1)
    return jnp.tile(v, width // v.shape[0])


def _layer(h, mem2d, pos_col, pos_row, batch, seq, m_len, p):
    (attn_norm_g, w_in, swa_q_g, swa_k_g, swa_sinks, cq_g, ckv_g, w_uq, w_ukv, qn_g, qr_g, kn_g, kr_g,
     mem_g, w_mem_kv, memq_g, memk_g, w_out, ffn_g, w_gate, w_up, w_down) = p
    width = 2 * LANES
    inv_freq = ROPE_THETA ** (-jnp.arange(0, MLA_ROPE_DIM, 2, dtype=F32) / MLA_ROPE_DIM)
    sign = jnp.concatenate([-jnp.ones((MLA_ROPE_DIM // 2,), F32), jnp.ones((MLA_ROPE_DIM // 2,), F32)])
    rows = [swa_q_g, swa_k_g, qn_g, qr_g, kn_g, kr_g, memq_g, inv_freq, sign]
    sp = jnp.stack([_tile_row(r, width) for r in rows]
                   + [jnp.zeros((width,), F32)] * (_SP_ROWS - len(rows)))

    kr0 = _C_KR[0]
    w1 = jnp.concatenate([w_in[:, :kr0 + MLA_ROPE_DIM], w_in[:, kr0:kr0 + MLA_ROPE_DIM],
                          w_in[:, kr0 + MLA_ROPE_DIM:]], axis=1).astype(BF16)
    qd = MLA_NOPE_DIM + MLA_ROPE_DIM
    wq3 = w_uq.reshape(w_uq.shape[0], MLA_HEADS, qd)
    wq = jnp.concatenate([wq3, wq3[:, :, MLA_NOPE_DIM:]], axis=2).reshape(w_uq.shape[0], -1).astype(BF16)
    wkv3 = w_ukv.reshape(w_ukv.shape[0], MLA_HEADS, MLA_NOPE_DIM + MLA_V_DIM)
    wkv = jnp.concatenate([wkv3[:, :, :MLA_NOPE_DIM].reshape(w_ukv.shape[0], -1),
                           wkv3[:, :, MLA_NOPE_DIM:].reshape(w_ukv.shape[0], -1)], axis=1).astype(BF16)
    idx = jnp.arange(MXU_DIM) // SWA_HEAD_DIM
    bd = (idx[:, None] == idx[None, :]).astype(BF16)
    slopes = 2.0 ** (-8.0 * jnp.arange(1, SWA_Q_HEADS + 1, dtype=F32) / SWA_Q_HEADS)
    slope_sink = jnp.stack([slopes, swa_sinks.astype(F32)]) * LOG2E

    km, vm = _mem_kv(mem2d, mem_g.reshape(1, -1), w_mem_kv.astype(BF16), memk_g.reshape(1, -1),
                     batch, m_len)
    qa, ka, va, krd, qm, qb, kn, vb, wg16, wu16, wd16 = _in_proj(
        h, pos_col, attn_norm_g.reshape(1, -1), sp, w1, wq, wkv, cq_g.reshape(1, -1),
        ckv_g.reshape(1, -1), bd, (w_gate, w_up, w_down), tm=512)
    ya = _swa_attn(slope_sink, qa, ka, va, pos_col, pos_row, batch, seq, tq=512)
    yb = _mla_attn(qb, kn, krd, vb, batch, seq, tq=512, hps=2)
    ym = _mem_attn(qm, km, vm, batch, seq, m_len, tq=1024)
    h = _out_proj(h, ya, yb, ym, w_out.astype(BF16), tm=512)
    return _ffn(h, ffn_g.reshape(1, -1), wg16, wu16, wd16, tm=1024, tf=512)


def kernel(x, mem, positions, attn_norm_g, w_in, swa_q_norm_g, swa_k_norm_g, swa_sinks, mla_cq_norm_g,
           mla_ckv_norm_g, w_uq, w_ukv, mla_qn_norm_g, mla_qr_norm_g, mla_kn_norm_g, mla_kr_norm_g,
           mem_norm_g, w_mem_kv, mem_q_norm_g, mem_k_norm_g, w_out, ffn_norm_g, w_gate, w_up, w_down):
    batch, seq, d = x.shape
    m_len = mem.shape[1]
    stacked = (attn_norm_g, w_in, swa_q_norm_g, swa_k_norm_g, swa_sinks, mla_cq_norm_g, mla_ckv_norm_g,
               w_uq, w_ukv, mla_qn_norm_g, mla_qr_norm_g, mla_kn_norm_g, mla_kr_norm_g, mem_norm_g,
               w_mem_kv, mem_q_norm_g, mem_k_norm_g, w_out, ffn_norm_g, w_gate, w_up, w_down)
    h = x.reshape(batch * seq, d)
    mem2d = mem.reshape(batch * m_len, d)
    pos_col = positions.reshape(batch * seq, 1)
    pos_row = positions.reshape(batch, 1, seq)
    for l in range(attn_norm_g.shape[0]):
        h = _layer(h, mem2d, pos_col, pos_row, batch, seq, m_len, tuple(a[l] for a in stacked))
    return h.reshape(batch, seq, d)
```

```python
import functools

import jax
import jax.numpy as jnp
from jax import lax
from jax.experimental import pallas as pl
from jax.experimental.pallas import tpu as pltpu

EPS = 1e-6
NEG_INF = -1e30
LOG2E = 1.4426950408889634
BLOCK = 128
WINDOW = 128

SWA_Q_HEADS = 16
SWA_KV_HEADS = 2
SWA_HEAD_DIM = 64

MLA_HEADS = 4
MLA_NOPE_DIM = 128
MLA_ROPE_DIM = 64
MLA_V_DIM = 128
ROPE_THETA = 10000.0

MEM_HEADS = 4
MEM_HEAD_DIM = 128

LANES = 128
MXU_DIM = 256
VMEM_LIMIT_BYTES = 56 * 1024 * 1024

F32 = jnp.float32
BF16 = jnp.bfloat16

_SP_SWA_Q, _SP_SWA_K, _SP_QN, _SP_QR, _SP_KN, _SP_KR, _SP_MEMQ, _SP_FREQ, _SP_SIGN = range(9)
_SP_ROWS = 16


def _dot(a, b):
    return jnp.dot(a, b, preferred_element_type=F32)


def _dot_nt(a, b):
    return lax.dot_general(a, b, (((1,), (1,)), ((), ())), preferred_element_type=F32)


def _rms(x, g):
    return x * lax.rsqrt(jnp.mean(x * x, axis=-1, keepdims=True) + EPS) * g


def _const_spec(shape):
    nd = len(shape)
    return pl.BlockSpec(shape, lambda *_: (0,) * nd, pipeline_mode=pl.Buffered(1))


def _mem_kv_kernel(mem_ref, g_ref, w32_ref, gk_ref, k_ref, v_ref, w_ref):
    @pl.when(pl.program_id(0) == 0)
    def _():
        w_ref[...] = w32_ref[...].astype(BF16)

    xn = _rms(mem_ref[...], g_ref[...]).astype(BF16)
    kv = _dot(xn, w_ref[...])
    hd = MEM_HEADS * MEM_HEAD_DIM
    for h in range(MEM_HEADS):
        sl = slice(h * MEM_HEAD_DIM, (h + 1) * MEM_HEAD_DIM)
        k_ref[:, sl] = _rms(kv[:, sl], gk_ref[...]).astype(BF16)
    v_ref[...] = kv[:, hd:].astype(BF16)


def _mem_kv(mem2d, g, w, gk, batch, m_len):
    d = mem2d.shape[1]
    hd = MEM_HEADS * MEM_HEAD_DIM
    return pl.pallas_call(
        _mem_kv_kernel,
        grid=(batch,),
        in_specs=[pl.BlockSpec((m_len, d), lambda b: (b, 0)),
                  _const_spec((1, d)), _const_spec((d, 2 * hd)), _const_spec((1, MEM_HEAD_DIM))],
        out_specs=[pl.BlockSpec((m_len, hd), lambda b: (b, 0))] * 2,
        out_shape=[jax.ShapeDtypeStruct((batch * m_len, hd), BF16)] * 2,
        scratch_shapes=[pltpu.VMEM((d, 2 * hd), BF16)],
        compiler_params=pltpu.CompilerParams(dimension_semantics=("arbitrary",),
                                             vmem_limit_bytes=VMEM_LIMIT_BYTES),
        name="mem_kv",
    )(mem2d, g, w, gk)


_C_QA = (0, 1024)
_C_KVA = (1024, 1280)
_C_CQ = (1280, 1792)
_C_CKV = (1792, 2304)
_C_KR = (2304, 2432)
_C_QM = (2432, 2944)


def _in_proj_kernel(x_ref, pos_ref, gattn_ref, sp_ref, w1_ref, wq_ref, wkv_ref, gcq_ref, gckv_ref,
                    bd_ref, wg32_ref, wu32_ref, wd32_ref, wo32_ref,
                    qa_ref, ka_ref, va_ref, krd_ref, qm_ref, qb_ref, kn_ref, vb_ref,
                    wg16_ref, wu16_ref, wd16_ref, wo16_ref):
    wg16_ref[...] = wg32_ref[...].astype(BF16)
    wu16_ref[...] = wu32_ref[...].astype(BF16)
    wd16_ref[...] = wd32_ref[...].astype(BF16)
    wo16_ref[...] = wo32_ref[...].astype(BF16)
    sp = sp_ref[...]
    row = lambda r, n=LANES: sp[r:r + 1, :n]
    hn = _rms(x_ref[...], gattn_ref[...]).astype(BF16)

    def proj(cols):
        return _dot(hn, w1_ref[:, cols[0]:cols[1]])

    def head64_rms(blk, g):
        ss = _dot((blk * blk).astype(BF16), bd_ref[...])
        return blk * lax.rsqrt(ss * (1.0 / SWA_HEAD_DIM) + EPS) * g

    ang = pos_ref[...].astype(F32) * row(_SP_FREQ)
    cos = jnp.cos(ang)
    sin_signed = jnp.sin(ang) * row(_SP_SIGN)

    def rope_dup(y):
        return y * cos + pltpu.roll(y, MLA_ROPE_DIM // 2, 1) * sin_signed

    cq = proj(_C_CQ)
    ckv = proj(_C_CKV)
    qm = proj(_C_QM)
    kva = proj(_C_KVA)
    kr = proj(_C_KR)
    cqn = _rms(cq, gcq_ref[...]).astype(BF16)
    ckvn = _rms(ckv, gckv_ref[...]).astype(BF16)

    for h in range(MEM_HEADS):
        sl = slice(h * MEM_HEAD_DIM, (h + 1) * MEM_HEAD_DIM)
        qm_ref[:, sl] = (_rms(qm[:, sl], row(_SP_MEMQ)) * (MEM_HEAD_DIM ** -0.5)).astype(BF16)
    lo = lax.broadcasted_iota(jnp.int32, (1, LANES), 1) < SWA_HEAD_DIM
    for src, dst in ((head64_rms(kva, row(_SP_SWA_K, MXU_DIM))[:, :LANES], ka_ref), (kva[:, LANES:], va_ref)):
        swapped = pltpu.roll(src, SWA_HEAD_DIM, 1)
        dst[:, :LANES] = jnp.where(lo, src, swapped).astype(BF16)
        dst[:, LANES:] = jnp.where(lo, swapped, src).astype(BF16)
    krd_ref[...] = rope_dup(_rms(kr, row(_SP_KR))).astype(BF16)

    qb = _dot(cqn, wq_ref[...])
    kvb = _dot(ckvn, wkv_ref[...])
    qa = proj(_C_QA)

    scale = LOG2E * (MLA_NOPE_DIM + MLA_ROPE_DIM) ** -0.5
    for h in range(MLA_HEADS):
        o = h * 2 * LANES
        qb_ref[:, o:o + LANES] = (_rms(qb[:, o:o + LANES], row(_SP_QN)) * scale).astype(BF16)
        qr = rope_dup(_rms(qb[:, o + LANES:o + 2 * LANES], row(_SP_QR)))
        qb_ref[:, o + LANES:o + 2 * LANES] = (qr * (0.5 * scale)).astype(BF16)
    hn_cols = MLA_HEADS * MLA_NOPE_DIM
    for h in range(MLA_HEADS):
        sl = slice(h * MLA_NOPE_DIM, (h + 1) * MLA_NOPE_DIM)
        kn_ref[:, sl] = _rms(kvb[:, sl], row(_SP_KN)).astype(BF16)
        vb_ref[:, 2 * h * LANES:(2 * h + 1) * LANES] = kvb[:, hn_cols + h * MLA_V_DIM:
                                                          hn_cols + (h + 1) * MLA_V_DIM].astype(BF16)
        vb_ref[:, (2 * h + 1) * LANES:(2 * h + 2) * LANES] = jnp.ones((kvb.shape[0], LANES), BF16)

    for c in range(_C_QA[1] // MXU_DIM):
        sl = slice(c * MXU_DIM, (c + 1) * MXU_DIM)
        qa_ref[:, sl] = (head64_rms(qa[:, sl], row(_SP_SWA_Q, MXU_DIM))
                         * (LOG2E * SWA_HEAD_DIM ** -0.5)).astype(BF16)


def _in_proj(x2d, pos_col, gattn, sp, w1, wq, wkv, gcq, gckv, bd, late_w, tm):
    t, d = x2d.shape
    steps = t // tm
    widths = (1024, 256, 256, 128, 512, 1024, 512, 1024)
    rows = lambda w: pl.BlockSpec((tm, w), lambda i: (i, 0))
    slabs = []
    for w in late_w:
        assert w.shape[0] % (16 * steps) == 0, (w.shape, steps)
        slabs.append(pl.BlockSpec((w.shape[0] // steps, w.shape[1]), lambda i: (i, 0)))
    return pl.pallas_call(
        _in_proj_kernel,
        grid=(steps,),
        in_specs=[rows(d), rows(1), _const_spec(gattn.shape), _const_spec(sp.shape),
                  _const_spec(w1.shape), _const_spec(wq.shape), _const_spec(wkv.shape),
                  _const_spec(gcq.shape), _const_spec(gckv.shape), _const_spec(bd.shape)] + slabs,
        out_specs=[rows(w) for w in widths] + slabs,
        out_shape=[jax.ShapeDtypeStruct((t, w), BF16) for w in widths]
        + [jax.ShapeDtypeStruct(w.shape, BF16) for w in late_w],
        compiler_params=pltpu.CompilerParams(dimension_semantics=("arbitrary",),
                                             vmem_limit_bytes=VMEM_LIMIT_BYTES),
        name="in_proj",
    )(x2d, pos_col, gattn, sp, w1, wq, wkv, gcq, gckv, bd, *late_w)


_SWA_MASK_DIST = -NEG_INF * 2.0 ** 8


def _swa_kernel(ss_ref, q_ref, kc_ref, kp_ref, vc_ref, vp_ref, pq_ref, pkc_ref, pkp_ref, o_ref, *, tq):
    i = pl.program_id(1)
    g = SWA_Q_HEADS // SWA_KV_HEADS
    kk = lax.broadcasted_iota(jnp.int32, (BLOCK, BLOCK), 0)
    qq = lax.broadcasted_iota(jnp.int32, (BLOCK, BLOCK), 1)
    from_prev = kk > qq
    prev_w = jnp.where(from_prev, 1.0, 0.0).astype(BF16)
    cur_w = jnp.where(from_prev, 0.0, 1.0).astype(BF16)
    lane_lo = lax.broadcasted_iota(jnp.int32, (BLOCK, LANES), 1) < SWA_HEAD_DIM
    lo_w = jnp.where(lane_lo, 1.0, 0.0).astype(BF16)
    hi_w = jnp.where(lane_lo, 0.0, 1.0).astype(BF16)

    for j in range(tq // BLOCK):
        rows = slice(j * BLOCK, (j + 1) * BLOCK)
        prows = slice((j - 1) * BLOCK, j * BLOCK)
        k_prev, v_prev, pk_prev = ((kp_ref[...], vp_ref[...], pkp_ref[...]) if j == 0 else
                                   (kc_ref[prows, :], vc_ref[prows, :], pkc_ref[prows, :]))
        kj = jnp.concatenate([k_prev, kc_ref[rows, :]], axis=0)
        vj = jnp.concatenate([v_prev, vc_ref[rows, :]], axis=0)
        pq = pq_ref[:, rows]
        dist = jnp.where(from_prev, jnp.abs(pk_prev - pq), jnp.abs(pkc_ref[rows, :] - pq)).astype(F32)
        if j == 0:
            dist = jnp.where(from_prev, jnp.maximum(dist, jnp.where(i == 0, _SWA_MASK_DIST, 0.0)), dist)
        for kv in range(SWA_KV_HEADS):
            kvl = slice(kv * LANES, (kv + 1) * LANES)
            qm = jnp.concatenate(
                [q_ref[rows, (kv * g + hh) // 2 * LANES:((kv * g + hh) // 2 + 1) * LANES]
                 * (lo_w if hh % 2 == 0 else hi_w) for hh in range(g)], axis=0)
            st = _dot_nt(kj[:, kvl], qm)
            pts = []
            for hh in range(g):
                head = kv * g + hh
                cols = slice(hh * BLOCK, (hh + 1) * BLOCK)
                t = jnp.where(from_prev, st[:BLOCK, cols], st[BLOCK:, cols]) - ss_ref[0, head] * dist
                sink = ss_ref[1, head]
                m = jnp.maximum(jnp.max(t, axis=0, keepdims=True), sink)
                e = jnp.exp2(t - m)
                den = jnp.sum(e, axis=0, keepdims=True) + jnp.exp2(sink - m)
                p = (e * (1.0 / den)).astype(BF16)
                pts.append(jnp.concatenate([p * prev_w, p * cur_w], axis=0))
            pt = jnp.concatenate(pts, axis=1)
            o = lax.dot_general(pt, vj[:, kvl], (((0,), (0,)), ((), ())), preferred_element_type=F32)
            for pr in range(g // 2):
                grp = kv * (g // 2) + pr
                even = o[(2 * pr) * BLOCK:(2 * pr + 1) * BLOCK]
                odd = o[(2 * pr + 1) * BLOCK:(2 * pr + 2) * BLOCK]
                o_ref[rows, grp * LANES:(grp + 1) * LANES] = jnp.where(lane_lo, even, odd).astype(BF16)


def _swa_attn(slope_sink, qa, ka, va, pos_col, pos_row, batch, seq, tq):
    t = batch * seq
    nq = seq // tq
    nb = tq // BLOCK
    cur = lambda w: pl.BlockSpec((tq, w), lambda b, i: (b * nq + i, 0))
    prev = lambda w: pl.BlockSpec(
        (BLOCK, w), lambda b, i: (b * nq * nb + jnp.maximum(i * nb - 1, 0), 0))
    return pl.pallas_call(
        functools.partial(_swa_kernel, tq=tq),
        grid=(batch, nq),
        in_specs=[pl.BlockSpec(memory_space=pltpu.SMEM),
                  cur(SWA_Q_HEADS * SWA_HEAD_DIM), cur(2 * LANES), prev(2 * LANES), cur(2 * LANES),
                  prev(2 * LANES),
                  pl.BlockSpec((None, 1, tq), lambda b, i: (b, 0, i)),
                  cur(1), prev(1)],
        out_specs=cur(SWA_Q_HEADS * SWA_HEAD_DIM),
        out_shape=jax.ShapeDtypeStruct((t, SWA_Q_HEADS * SWA_HEAD_DIM), BF16),
        compiler_params=pltpu.CompilerParams(dimension_semantics=("arbitrary", "arbitrary"),
                                             vmem_limit_bytes=VMEM_LIMIT_BYTES),
        name="swa_attn",
    )(slope_sink, qa, ka, ka, va, va, pos_row, pos_col, pos_col)


def _mla_kernel(q_ref, kn_ref, kr_ref, v_ref, o_ref, s_sc, m_sc, acc_sc, *, tq, hps):
    qi = pl.program_id(2)
    qw = 2 * LANES

    def scores_to(slot, j):
        k0 = pl.multiple_of(j * tq, tq)
        kr = kr_ref[pl.ds(k0, tq), :]
        for hh in range(hps):
            k = jnp.concatenate([kn_ref[pl.ds(k0, tq), hh * LANES:(hh + 1) * LANES], kr], axis=1)
            s_sc[slot, hh] = _dot_nt(q_ref[:, hh * qw:(hh + 1) * qw], k)

    def update_from(slot, j, diagonal):
        k0 = pl.multiple_of(j * tq, tq)
        for hh in range(hps):
            s = s_sc[slot, hh]
            if diagonal:
                r = lax.broadcasted_iota(jnp.int32, (tq, tq), 0)
                c = lax.broadcasted_iota(jnp.int32, (tq, tq), 1)
                s = jnp.where(c <= r, s, NEG_INF)
            m = m_sc[hh]
            m_new = jnp.maximum(m, jnp.max(s, axis=-1, keepdims=True))
            p = jnp.exp2(s - m_new)
            m_sc[hh] = m_new
            acc_sc[hh] = jnp.exp2(m - m_new) * acc_sc[hh] + _dot(
                p.astype(BF16), v_ref[pl.ds(k0, tq), hh * qw:(hh + 1) * qw])

    m_sc[...] = jnp.full(m_sc.shape, NEG_INF, F32)
    acc_sc[...] = jnp.zeros(acc_sc.shape, F32)
    scores_to(0, 0)

    def pair(p, carry):
        j = 2 * p
        scores_to(1, j + 1)
        update_from(0, j, False)
        scores_to(0, j + 2)
        update_from(1, j + 1, False)
        return carry

    lax.fori_loop(0, qi // 2, pair, 0)
    odd = qi % 2 == 1

    @pl.when(odd)
    def _():
        scores_to(1, qi)
        update_from(0, qi - 1, False)
        update_from(1, qi, True)

    @pl.when(jnp.logical_not(odd))
    def _():
        update_from(0, qi, True)

    for hh in range(hps):
        acc = acc_sc[hh]
        o_ref[:, hh * MLA_V_DIM:(hh + 1) * MLA_V_DIM] = (acc[:, :MLA_V_DIM] / acc[:, MLA_V_DIM:]).astype(BF16)


def _mla_attn(qb, kn, krd, vb, batch, seq, tq, hps):
    t = batch * seq
    nq = seq // tq
    return pl.pallas_call(
        functools.partial(_mla_kernel, tq=tq, hps=hps),
        grid=(batch, MLA_HEADS // hps, nq),
        in_specs=[pl.BlockSpec((tq, 2 * LANES * hps), lambda b, h, i: (b * nq + i, h)),
                  pl.BlockSpec((seq, MLA_NOPE_DIM * hps), lambda b, h, i: (b, h)),
                  pl.BlockSpec((seq, LANES), lambda b, h, i: (b, 0)),
                  pl.BlockSpec((seq, 2 * LANES * hps), lambda b, h, i: (b, h))],
        out_specs=pl.BlockSpec((tq, MLA_V_DIM * hps), lambda b, h, i: (b * nq + i, h)),
        out_shape=jax.ShapeDtypeStruct((t, MLA_HEADS * MLA_V_DIM), BF16),
        scratch_shapes=[pltpu.VMEM((2, hps, tq, tq), F32), pltpu.VMEM((hps, tq, 1), F32),
                        pltpu.VMEM((hps, tq, 2 * LANES), F32)],
        compiler_params=pltpu.CompilerParams(dimension_semantics=("arbitrary",) * 3,
                                             vmem_limit_bytes=VMEM_LIMIT_BYTES),
        name="mla_attn",
    )(qb, kn, krd, vb)


def _mem_attn_kernel(q_ref, k_ref, v_ref, o_ref):
    for h in range(MEM_HEADS):
        sl = slice(h * MEM_HEAD_DIM, (h + 1) * MEM_HEAD_DIM)
        s = _dot_nt(q_ref[:, sl], k_ref[:, sl])
        e = jnp.exp(s - jnp.max(s, axis=-1, keepdims=True))
        o = _dot(e.astype(BF16), v_ref[:, sl]) / jnp.sum(e, axis=-1, keepdims=True)
        o_ref[:, sl] = o.astype(BF16)


def _mem_attn(qm, km, vm, batch, seq, m_len, tq):
    t = batch * seq
    nq = seq // tq
    hd = MEM_HEADS * MEM_HEAD_DIM
    return pl.pallas_call(
        _mem_attn_kernel,
        grid=(batch, nq),
        in_specs=[pl.BlockSpec((tq, hd), lambda b, i: (b * nq + i, 0)),
                  pl.BlockSpec((m_len, hd), lambda b, i: (b, 0)),
                  pl.BlockSpec((m_len, hd), lambda b, i: (b, 0))],
        out_specs=pl.BlockSpec((tq, hd), lambda b, i: (b * nq + i, 0)),
        out_shape=jax.ShapeDtypeStruct((t, hd), BF16),
        compiler_params=pltpu.CompilerParams(dimension_semantics=("arbitrary", "arbitrary"),
                                             vmem_limit_bytes=VMEM_LIMIT_BYTES),
        name="mem_attn",
    )(qm, km, vm)


def _out_proj_kernel(x_ref, ya_ref, yb_ref, ym_ref, w_ref, h_ref):
    na = ya_ref.shape[1]
    nb = na + yb_ref.shape[1]
    h_ref[...] = (x_ref[...] + _dot(ya_ref[...], w_ref[0:na, :]) + _dot(yb_ref[...], w_ref[na:nb, :])
                  + _dot(ym_ref[...], w_ref[nb:, :]))


def _out_proj(x2d, ya, yb, ym, w_out, tm):
    t, d = x2d.shape
    rows = lambda w: pl.BlockSpec((tm, w), lambda i: (i, 0))
    return pl.pallas_call(
        _out_proj_kernel,
        grid=(t // tm,),
        in_specs=[rows(d), rows(ya.shape[1]), rows(yb.shape[1]), rows(ym.shape[1]),
                  _const_spec(w_out.shape)],
        out_specs=rows(d),
        out_shape=jax.ShapeDtypeStruct((t, d), F32),
        compiler_params=pltpu.CompilerParams(dimension_semantics=("arbitrary",),
                                             vmem_limit_bytes=VMEM_LIMIT_BYTES),
        name="out_proj",
    )(x2d, ya, yb, ym, w_out)


def _ffn_kernel(h_ref, g_ref, wg_ref, wu_ref, wd_ref, o_ref, fn_ref):
    j = pl.program_id(1)

    @pl.when(j == 0)
    def _():
        h = h_ref[...]
        fn_ref[...] = _rms(h, g_ref[...]).astype(BF16)
        o_ref[...] = h

    fn = fn_ref[...]
    gate = _dot(fn, wg_ref[...])
    up = _dot(fn, wu_ref[...])
    act = (gate * jax.nn.sigmoid(gate) * up).astype(BF16)
    o_ref[...] += _dot(act, wd_ref[...])


def _ffn(h, g, wg, wu, wd, tm, tf):
    t, d = h.shape
    dff = wg.shape[1]
    return pl.pallas_call(
        _ffn_kernel,
        grid=(t // tm, dff // tf),
        in_specs=[pl.BlockSpec((tm, d), lambda i, j: (i, 0)),
                  _const_spec((1, d)),
                  pl.BlockSpec((d, tf), lambda i, j: (0, j)),
                  pl.BlockSpec((d, tf), lambda i, j: (0, j)),
                  pl.BlockSpec((tf, d), lambda i, j: (j, 0))],
        out_specs=pl.BlockSpec((tm, d), lambda i, j: (i, 0)),
        out_shape=jax.ShapeDtypeStruct((t, d), F32),
        scratch_shapes=[pltpu.VMEM((tm, d), BF16)],
        compiler_params=pltpu.CompilerParams(dimension_semantics=("arbitrary", "arbitrary"),
                                             vmem_limit_bytes=VMEM_LIMIT_BYTES),
        name="ffn",
    )(h, g, wg, wu, wd)


def _tile_row(v, width):
    v = v.astype(F32).reshape(-1)
    return jnp.tile(v, width // v.shape[0])


def _layer(h, mem2d, pos_col, pos_row, batch, seq, m_len, p):
    (attn_norm_g, w_in, swa_q_g, swa_k_g, swa_sinks, cq_g, ckv_g, w_uq, w_ukv, qn_g, qr_g, kn_g, kr_g,
     mem_g, w_mem_kv, memq_g, memk_g, w_out, ffn_g, w_gate, w_up, w_down) = p
    width = 2 * LANES
    inv_freq = ROPE_THETA ** (-jnp.arange(0, MLA_ROPE_DIM, 2, dtype=F32) / MLA_ROPE_DIM)
    sign = jnp.concatenate([-jnp.ones((MLA_ROPE_DIM // 2,), F32), jnp.ones((MLA_ROPE_DIM // 2,), F32)])
    rows = [swa_q_g, swa_k_g, qn_g, qr_g, kn_g, kr_g, memq_g, inv_freq, sign]
    sp = jnp.stack([_tile_row(r, width) for r in rows]
                   + [jnp.zeros((width,), F32)] * (_SP_ROWS - len(rows)))

    kr0 = _C_KR[0]
    w1 = jnp.concatenate([w_in[:, :kr0 + MLA_ROPE_DIM], w_in[:, kr0:kr0 + MLA_ROPE_DIM],
                          w_in[:, kr0 + MLA_ROPE_DIM:]], axis=1).astype(BF16)
    qd = MLA_NOPE_DIM + MLA_ROPE_DIM
    wq3 = w_uq.reshape(w_uq.shape[0], MLA_HEADS, qd)
    wq = jnp.concatenate([wq3, wq3[:, :, MLA_NOPE_DIM:]], axis=2).reshape(w_uq.shape[0], -1).astype(BF16)
    wkv3 = w_ukv.reshape(w_ukv.shape[0], MLA_HEADS, MLA_NOPE_DIM + MLA_V_DIM)
    wkv = jnp.concatenate([wkv3[:, :, :MLA_NOPE_DIM].reshape(w_ukv.shape[0], -1),
                           wkv3[:, :, MLA_NOPE_DIM:].reshape(w_ukv.shape[0], -1)], axis=1).astype(BF16)
    idx = jnp.arange(MXU_DIM) // SWA_HEAD_DIM
    bd = (idx[:, None] == idx[None, :]).astype(BF16)
    slopes = 2.0 ** (-8.0 * jnp.arange(1, SWA_Q_HEADS + 1, dtype=F32) / SWA_Q_HEADS)
    slope_sink = jnp.stack([slopes, swa_sinks.astype(F32)]) * LOG2E

    km, vm = _mem_kv(mem2d, mem_g.reshape(1, -1), w_mem_kv, memk_g.reshape(1, -1), batch, m_len)
    qa, ka, va, krd, qm, qb, kn, vb, wg16, wu16, wd16, wo16 = _in_proj(
        h, pos_col, attn_norm_g.reshape(1, -1), sp, w1, wq, wkv, cq_g.reshape(1, -1),
        ckv_g.reshape(1, -1), bd, (w_gate, w_up, w_down, w_out), tm=512)
    ya = _swa_attn(slope_sink, qa, ka, va, pos_col, pos_row, batch, seq, tq=512)
    yb = _mla_attn(qb, kn, krd, vb, batch, seq, tq=512, hps=2)
    ym = _mem_attn(qm, km, vm, batch, seq, m_len, tq=1024)
    h = _out_proj(h, ya, yb, ym, wo16, tm=512)
    return _ffn(h, ffn_g.reshape(1, -1), wg16, wu16, wd16, tm=1024, tf=512)


def kernel(x, mem, positions, attn_norm_g, w_in, swa_q_norm_g, swa_k_norm_g, swa_sinks, mla_cq_norm_g,
           mla_ckv_norm_g, w_uq, w_ukv, mla_qn_norm_g, mla_qr_norm_g, mla_kn_norm_g, mla_kr_norm_g,
           mem_norm_g, w_mem_kv, mem_q_norm_g, mem_k_norm_g, w_out, ffn_norm_g, w_gate, w_up, w_down):
    batch, seq, d = x.shape
    m_len = mem.shape[1]
    stacked = (attn_norm_g, w_in, swa_q_norm_g, swa_k_norm_g, swa_sinks, mla_cq_norm_g, mla_ckv_norm_g,
               w_uq, w_ukv, mla_qn_norm_g, mla_qr_norm_g, mla_kn_norm_g, mla_kr_norm_g, mem_norm_g,
               w_mem_kv, mem_q_norm_g, mem_k_norm_g, w_out, ffn_norm_g, w_gate, w_up, w_down)
    h = x.reshape(batch * seq, d)
    mem2d = mem.reshape(batch * m_len, d)
    pos_col = positions.reshape(batch * seq, 1)
    pos_row = positions.reshape(batch, 1, seq)
    for l in range(attn_norm_g.shape[0]):
        h = _layer(h, mem2d, pos_col, pos_row, batch, seq, m_len, tuple(a[l] for a in stacked))
    return h.reshape(batch, seq, d)
```

```python
import functools

import jax
import jax.numpy as jnp
from jax import lax
from jax.experimental import pallas as pl
from jax.experimental.pallas import tpu as pltpu

EPS = 1e-6
NEG_INF = -1e30
LOG2E = 1.4426950408889634
BLOCK = 128
WINDOW = 128

SWA_Q_HEADS = 16
SWA_KV_HEADS = 2
SWA_HEAD_DIM = 64

MLA_HEADS = 4
MLA_NOPE_DIM = 128
MLA_ROPE_DIM = 64
MLA_V_DIM = 128
ROPE_THETA = 10000.0

MEM_HEADS = 4
MEM_HEAD_DIM = 128

LANES = 128
MXU_DIM = 256
VMEM_LIMIT_BYTES = 56 * 1024 * 1024

F32 = jnp.float32
BF16 = jnp.bfloat16

_SP_SWA_Q, _SP_SWA_K, _SP_QN, _SP_QR, _SP_KN, _SP_KR, _SP_MEMQ, _SP_FREQ, _SP_SIGN = range(9)
_SP_ROWS = 16


def _dot(a, b):
    return jnp.dot(a, b, preferred_element_type=F32)


def _dot_nt(a, b):
    return lax.dot_general(a, b, (((1,), (1,)), ((), ())), preferred_element_type=F32)


def _rms(x, g):
    return x * lax.rsqrt(jnp.mean(x * x, axis=-1, keepdims=True) + EPS) * g


def _const_spec(shape):
    nd = len(shape)
    return pl.BlockSpec(shape, lambda *_: (0,) * nd, pipeline_mode=pl.Buffered(1))


def _mem_kv_kernel(mem_ref, g_ref, w32_ref, gk_ref, k_ref, v_ref, w_ref):
    @pl.when(pl.program_id(0) == 0)
    def _():
        w_ref[...] = w32_ref[...].astype(BF16)

    xn = _rms(mem_ref[...], g_ref[...]).astype(BF16)
    kv = _dot(xn, w_ref[...])
    hd = MEM_HEADS * MEM_HEAD_DIM
    for h in range(MEM_HEADS):
        sl = slice(h * MEM_HEAD_DIM, (h + 1) * MEM_HEAD_DIM)
        k_ref[:, sl] = _rms(kv[:, sl], gk_ref[...]).astype(BF16)
    v_ref[...] = kv[:, hd:].astype(BF16)


def _mem_kv(mem2d, g, w, gk, batch, m_len):
    d = mem2d.shape[1]
    hd = MEM_HEADS * MEM_HEAD_DIM
    return pl.pallas_call(
        _mem_kv_kernel,
        grid=(batch,),
        in_specs=[pl.BlockSpec((m_len, d), lambda b: (b, 0)),
                  _const_spec((1, d)), _const_spec((d, 2 * hd)), _const_spec((1, MEM_HEAD_DIM))],
        out_specs=[pl.BlockSpec((m_len, hd), lambda b: (b, 0))] * 2,
        out_shape=[jax.ShapeDtypeStruct((batch * m_len, hd), BF16)] * 2,
        scratch_shapes=[pltpu.VMEM((d, 2 * hd), BF16)],
        compiler_params=pltpu.CompilerParams(dimension_semantics=("arbitrary",),
                                             vmem_limit_bytes=VMEM_LIMIT_BYTES),
        name="mem_kv",
    )(mem2d, g, w, gk)


_C_QA = (0, 1024)
_C_KVA = (1024, 1280)
_C_CQ = (1280, 1792)
_C_CKV = (1792, 2304)
_W1_COLS = 2304
_C_KR = (2304, 2432)
_C_QM = (2432, 2944)


def _in_proj_kernel(x_ref, pos_ref, gattn_ref, sp_ref, w1_ref, w2_ref, wq_ref, wkv_ref, gcq_ref, gckv_ref,
                    bd_ref, wg32_ref, wu32_ref, wd32_ref, wo32_ref,
                    qa_ref, ka_ref, va_ref, krd_ref, qm_ref, qb_ref, kn_ref, vb_ref,
                    wg16_ref, wu16_ref, wd16_ref, wo16_ref):
    wg16_ref[...] = wg32_ref[...].astype(BF16)
    wu16_ref[...] = wu32_ref[...].astype(BF16)
    wd16_ref[...] = wd32_ref[...].astype(BF16)
    wo16_ref[...] = wo32_ref[...].astype(BF16)
    sp = sp_ref[...]
    row = lambda r, n=LANES: sp[r:r + 1, :n]
    hn = _rms(x_ref[...], gattn_ref[...]).astype(BF16)

    def proj(cols):
        if cols[0] >= _W1_COLS:
            return _dot(hn, w2_ref[:, cols[0] - _W1_COLS:cols[1] - _W1_COLS])
        return _dot(hn, w1_ref[:, cols[0]:cols[1]])

    def head64_rms(blk, g):
        ss = _dot((blk * blk).astype(BF16), bd_ref[...])
        return blk * lax.rsqrt(ss * (1.0 / SWA_HEAD_DIM) + EPS) * g

    ang = pos_ref[...].astype(F32) * row(_SP_FREQ)
    cos = jnp.cos(ang)
    sin_signed = jnp.sin(ang) * row(_SP_SIGN)

    def rope_dup(y):
        return y * cos + pltpu.roll(y, MLA_ROPE_DIM // 2, 1) * sin_signed

    cq = proj(_C_CQ)
    ckv = proj(_C_CKV)
    qm = proj(_C_QM)
    kva = proj(_C_KVA)
    kr = proj(_C_KR)
    cqn = _rms(cq, gcq_ref[...]).astype(BF16)
    ckvn = _rms(ckv, gckv_ref[...]).astype(BF16)

    for h in range(MEM_HEADS):
        sl = slice(h * MEM_HEAD_DIM, (h + 1) * MEM_HEAD_DIM)
        qm_ref[:, sl] = (_rms(qm[:, sl], row(_SP_MEMQ)) * (MEM_HEAD_DIM ** -0.5)).astype(BF16)
    lo = lax.broadcasted_iota(jnp.int32, (1, LANES), 1) < SWA_HEAD_DIM
    for src, dst in ((head64_rms(kva, row(_SP_SWA_K, MXU_DIM))[:, :LANES], ka_ref), (kva[:, LANES:], va_ref)):
        swapped = pltpu.roll(src, SWA_HEAD_DIM, 1)
        dst[:, :LANES] = jnp.where(lo, src, swapped).astype(BF16)
        dst[:, LANES:] = jnp.where(lo, swapped, src).astype(BF16)
    krd_ref[...] = rope_dup(_rms(kr, row(_SP_KR))).astype(BF16)

    qb = _dot(cqn, wq_ref[...])
    kvb = _dot(ckvn, wkv_ref[...])
    qa = proj(_C_QA)

    scale = LOG2E * (MLA_NOPE_DIM + MLA_ROPE_DIM) ** -0.5
    for h in range(MLA_HEADS):
        o = h * 2 * LANES
        qb_ref[:, o:o + LANES] = (_rms(qb[:, o:o + LANES], row(_SP_QN)) * scale).astype(BF16)
        qr = rope_dup(_rms(qb[:, o + LANES:o + 2 * LANES], row(_SP_QR)))
        qb_ref[:, o + LANES:o + 2 * LANES] = (qr * (0.5 * scale)).astype(BF16)
    hn_cols = MLA_HEADS * MLA_NOPE_DIM
    for h in range(MLA_HEADS):
        sl = slice(h * MLA_NOPE_DIM, (h + 1) * MLA_NOPE_DIM)
        kn_ref[:, sl] = _rms(kvb[:, sl], row(_SP_KN)).astype(BF16)
        vb_ref[:, 2 * h * LANES:(2 * h + 1) * LANES] = kvb[:, hn_cols + h * MLA_V_DIM:
                                                          hn_cols + (h + 1) * MLA_V_DIM].astype(BF16)
        vb_ref[:, (2 * h + 1) * LANES:(2 * h + 2) * LANES] = jnp.ones((kvb.shape[0], LANES), BF16)

    for c in range(_C_QA[1] // MXU_DIM):
        sl = slice(c * MXU_DIM, (c + 1) * MXU_DIM)
        qa_ref[:, sl] = (head64_rms(qa[:, sl], row(_SP_SWA_Q, MXU_DIM))
                         * (LOG2E * SWA_HEAD_DIM ** -0.5)).astype(BF16)


def _in_proj(x2d, pos_col, gattn, sp, w1, w2, wq, wkv, gcq, gckv, bd, late_w, tm):
    t, d = x2d.shape
    steps = t // tm
    widths = (1024, 256, 256, 128, 512, 1024, 512, 1024)
    rows = lambda w: pl.BlockSpec((tm, w), lambda i: (i, 0))
    slabs = []
    for w in late_w:
        assert w.shape[0] % (16 * steps) == 0, (w.shape, steps)
        slabs.append(pl.BlockSpec((w.shape[0] // steps, w.shape[1]), lambda i: (i, 0)))
    return pl.pallas_call(
        _in_proj_kernel,
        grid=(steps,),
        in_specs=[rows(d), rows(1), _const_spec(gattn.shape), _const_spec(sp.shape),
                  _const_spec(w1.shape), _const_spec(w2.shape), _const_spec(wq.shape),
                  _const_spec(wkv.shape),
                  _const_spec(gcq.shape), _const_spec(gckv.shape), _const_spec(bd.shape)] + slabs,
        out_specs=[rows(w) for w in widths] + slabs,
        out_shape=[jax.ShapeDtypeStruct((t, w), BF16) for w in widths]
        + [jax.ShapeDtypeStruct(w.shape, BF16) for w in late_w],
        compiler_params=pltpu.CompilerParams(dimension_semantics=("arbitrary",),
                                             vmem_limit_bytes=VMEM_LIMIT_BYTES),
        name="in_proj",
    )(x2d, pos_col, gattn, sp, w1, w2, wq, wkv, gcq, gckv, bd, *late_w)


_SWA_MASK_DIST = -NEG_INF * 2.0 ** 8


def _swa_kernel(ss_ref, q_ref, kc_ref, kp_ref, vc_ref, vp_ref, pq_ref, pkc_ref, pkp_ref, o_ref, *, tq):
    i = pl.program_id(1)
    g = SWA_Q_HEADS // SWA_KV_HEADS
    kk = lax.broadcasted_iota(jnp.int32, (BLOCK, BLOCK), 0)
    qq = lax.broadcasted_iota(jnp.int32, (BLOCK, BLOCK), 1)
    from_prev = kk > qq
    prev_w = jnp.where(from_prev, 1.0, 0.0).astype(BF16)
    cur_w = jnp.where(from_prev, 0.0, 1.0).astype(BF16)
    lane_lo = lax.broadcasted_iota(jnp.int32, (BLOCK, LANES), 1) < SWA_HEAD_DIM
    lo_w = jnp.where(lane_lo, 1.0, 0.0).astype(BF16)
    hi_w = jnp.where(lane_lo, 0.0, 1.0).astype(BF16)

    for j in range(tq // BLOCK):
        rows = slice(j * BLOCK, (j + 1) * BLOCK)
        prows = slice((j - 1) * BLOCK, j * BLOCK)
        k_prev, v_prev, pk_prev = ((kp_ref[...], vp_ref[...], pkp_ref[...]) if j == 0 else
                                   (kc_ref[prows, :], vc_ref[prows, :], pkc_ref[prows, :]))
        kj = jnp.concatenate([k_prev, kc_ref[rows, :]], axis=0)
        vj = jnp.concatenate([v_prev, vc_ref[rows, :]], axis=0)
        pq = pq_ref[:, rows]
        dist = jnp.where(from_prev, jnp.abs(pk_prev - pq), jnp.abs(pkc_ref[rows, :] - pq)).astype(F32)
        if j == 0:
            dist = jnp.where(from_prev, jnp.maximum(dist, jnp.where(i == 0, _SWA_MASK_DIST, 0.0)), dist)
        for kv in range(SWA_KV_HEADS):
            kvl = slice(kv * LANES, (kv + 1) * LANES)
            qm = jnp.concatenate(
                [q_ref[rows, (kv * g + hh) // 2 * LANES:((kv * g + hh) // 2 + 1) * LANES]
                 * (lo_w if hh % 2 == 0 else hi_w) for hh in range(g)], axis=0)
            st = _dot_nt(kj[:, kvl], qm)
            pts = []
            for hh in range(g):
                head = kv * g + hh
                cols = slice(hh * BLOCK, (hh + 1) * BLOCK)
                t = jnp.where(from_prev, st[:BLOCK, cols], st[BLOCK:, cols]) - ss_ref[0, head] * dist
                sink = ss_ref[1, head]
                m = jnp.maximum(jnp.max(t, axis=0, keepdims=True), sink)
                e = jnp.exp2(t - m)
                den = jnp.sum(e, axis=0, keepdims=True) + jnp.exp2(sink - m)
                p = (e * (1.0 / den)).astype(BF16)
                pts.append(jnp.concatenate([p * prev_w, p * cur_w], axis=0))
            pt = jnp.concatenate(pts, axis=1)
            o = lax.dot_general(pt, vj[:, kvl], (((0,), (0,)), ((), ())), preferred_element_type=F32)
            for pr in range(g // 2):
                grp = kv * (g // 2) + pr
                even = o[(2 * pr) * BLOCK:(2 * pr + 1) * BLOCK]
                odd = o[(2 * pr + 1) * BLOCK:(2 * pr + 2) * BLOCK]
                o_ref[rows, grp * LANES:(grp + 1) * LANES] = jnp.where(lane_lo, even, odd).astype(BF16)


def _swa_attn(slope_sink, qa, ka, va, pos_col, pos_row, batch, seq, tq):
    t = batch * seq
    nq = seq // tq
    nb = tq // BLOCK
    cur = lambda w: pl.BlockSpec((tq, w), lambda b, i: (b * nq + i, 0))
    prev = lambda w: pl.BlockSpec(
        (BLOCK, w), lambda b, i: (b * nq * nb + jnp.maximum(i * nb - 1, 0), 0))
    return pl.pallas_call(
        functools.partial(_swa_kernel, tq=tq),
        grid=(batch, nq),
        in_specs=[pl.BlockSpec(memory_space=pltpu.SMEM),
                  cur(SWA_Q_HEADS * SWA_HEAD_DIM), cur(2 * LANES), prev(2 * LANES), cur(2 * LANES),
                  prev(2 * LANES),
                  pl.BlockSpec((None, 1, tq), lambda b, i: (b, 0, i)),
                  cur(1), prev(1)],
        out_specs=cur(SWA_Q_HEADS * SWA_HEAD_DIM),
        out_shape=jax.ShapeDtypeStruct((t, SWA_Q_HEADS * SWA_HEAD_DIM), BF16),
        compiler_params=pltpu.CompilerParams(dimension_semantics=("arbitrary", "arbitrary"),
                                             vmem_limit_bytes=VMEM_LIMIT_BYTES),
        name="swa_attn",
    )(slope_sink, qa, ka, ka, va, va, pos_row, pos_col, pos_col)


def _mla_kernel(q_ref, kn_ref, kr_ref, v_ref, o_ref, s_sc, m_sc, acc_sc, *, tq, hps):
    qi = pl.program_id(2)
    qw = 2 * LANES
    tk = tq // 2

    def scores_to(slot, j, row0=0):
        k0 = pl.multiple_of(j * tk, tk)
        kr = kr_ref[pl.ds(k0, tk), :]
        for hh in range(hps):
            k = jnp.concatenate([kn_ref[pl.ds(k0, tk), hh * LANES:(hh + 1) * LANES], kr], axis=1)
            s_sc[slot, hh, row0:, :] = _dot_nt(q_ref[row0:, hh * qw:(hh + 1) * qw], k)

    def update_from(slot, j, row0=0, masked=False):
        k0 = pl.multiple_of(j * tk, tk)
        for hh in range(hps):
            s = s_sc[slot, hh, row0:, :]
            if masked:
                r = lax.broadcasted_iota(jnp.int32, s.shape, 0)
                c = lax.broadcasted_iota(jnp.int32, s.shape, 1)
                s = jnp.where(c <= r, s, NEG_INF)
            m = m_sc[hh, row0:, :]
            m_new = jnp.maximum(m, jnp.max(s, axis=-1, keepdims=True))
            p = jnp.exp2(s - m_new)
            m_sc[hh, row0:, :] = m_new
            acc_sc[hh, row0:, :] = jnp.exp2(m - m_new) * acc_sc[hh, row0:, :] + _dot(
                p.astype(BF16), v_ref[pl.ds(k0, tk), hh * qw:(hh + 1) * qw])

    m_sc[...] = jnp.full(m_sc.shape, NEG_INF, F32)
    acc_sc[...] = jnp.zeros(acc_sc.shape, F32)
    scores_to(0, 0)

    def pair(p, carry):
        j = 2 * p
        scores_to(1, j + 1)
        update_from(0, j)
        scores_to(0, j + 2)
        update_from(1, j + 1)
        return carry

    lax.fori_loop(0, qi, pair, 0)
    scores_to(1, 2 * qi + 1, row0=tk)
    update_from(0, 2 * qi, masked=True)
    update_from(1, 2 * qi + 1, row0=tk, masked=True)

    for hh in range(hps):
        acc = acc_sc[hh]
        o_ref[:, hh * MLA_V_DIM:(hh + 1) * MLA_V_DIM] = (acc[:, :MLA_V_DIM] / acc[:, MLA_V_DIM:]).astype(BF16)


def _mla_attn(qb, kn, krd, vb, batch, seq, tq, hps):
    t = batch * seq
    nq = seq // tq
    assert tq % 2 == 0 and seq % tq == 0
    return pl.pallas_call(
        functools.partial(_mla_kernel, tq=tq, hps=hps),
        grid=(batch, MLA_HEADS // hps, nq),
        in_specs=[pl.BlockSpec((tq, 2 * LANES * hps), lambda b, h, i: (b * nq + i, h)),
                  pl.BlockSpec((seq, MLA_NOPE_DIM * hps), lambda b, h, i: (b, h)),
                  pl.BlockSpec((seq, LANES), lambda b, h, i: (b, 0)),
                  pl.BlockSpec((seq, 2 * LANES * hps), lambda b, h, i: (b, h))],
        out_specs=pl.BlockSpec((tq, MLA_V_DIM * hps), lambda b, h, i: (b * nq + i, h)),
        out_shape=jax.ShapeDtypeStruct((t, MLA_HEADS * MLA_V_DIM), BF16),
        scratch_shapes=[pltpu.VMEM((2, hps, tq, tq // 2), F32), pltpu.VMEM((hps, tq, 1), F32),
                        pltpu.VMEM((hps, tq, 2 * LANES), F32)],
        compiler_params=pltpu.CompilerParams(dimension_semantics=("arbitrary",) * 3,
                                             vmem_limit_bytes=VMEM_LIMIT_BYTES),
        name="mla_attn",
    )(qb, kn, krd, vb)


def _mem_attn_kernel(q_ref, k_ref, v_ref, o_ref):
    for h in range(MEM_HEADS):
        sl = slice(h * MEM_HEAD_DIM, (h + 1) * MEM_HEAD_DIM)
        s = _dot_nt(q_ref[:, sl], k_ref[:, sl])
        e = jnp.exp(s - jnp.max(s, axis=-1, keepdims=True))
        o = _dot(e.astype(BF16), v_ref[:, sl]) / jnp.sum(e, axis=-1, keepdims=True)
        o_ref[:, sl] = o.astype(BF16)


def _mem_attn(qm, km, vm, batch, seq, m_len, tq):
    t = batch * seq
    nq = seq // tq
    hd = MEM_HEADS * MEM_HEAD_DIM
    return pl.pallas_call(
        _mem_attn_kernel,
        grid=(batch, nq),
        in_specs=[pl.BlockSpec((tq, hd), lambda b, i: (b * nq + i, 0)),
                  pl.BlockSpec((m_len, hd), lambda b, i: (b, 0)),
                  pl.BlockSpec((m_len, hd), lambda b, i: (b, 0))],
        out_specs=pl.BlockSpec((tq, hd), lambda b, i: (b * nq + i, 0)),
        out_shape=jax.ShapeDtypeStruct((t, hd), BF16),
        compiler_params=pltpu.CompilerParams(dimension_semantics=("arbitrary", "arbitrary"),
                                             vmem_limit_bytes=VMEM_LIMIT_BYTES),
        name="mem_attn",
    )(qm, km, vm)


def _out_proj_kernel(x_ref, ya_ref, yb_ref, ym_ref, w_ref, h_ref):
    na = ya_ref.shape[1]
    nb = na + yb_ref.shape[1]
    h_ref[...] = (x_ref[...] + _dot(ya_ref[...], w_ref[0:na, :]) + _dot(yb_ref[...], w_ref[na:nb, :])
                  + _dot(ym_ref[...], w_ref[nb:, :]))


def _out_proj(x2d, ya, yb, ym, w_out, tm):
    t, d = x2d.shape
    rows = lambda w: pl.BlockSpec((tm, w), lambda i: (i, 0))
    return pl.pallas_call(
        _out_proj_kernel,
        grid=(t // tm,),
        in_specs=[rows(d), rows(ya.shape[1]), rows(yb.shape[1]), rows(ym.shape[1]),
                  _const_spec(w_out.shape)],
        out_specs=rows(d),
        out_shape=jax.ShapeDtypeStruct((t, d), F32),
        compiler_params=pltpu.CompilerParams(dimension_semantics=("arbitrary",),
                                             vmem_limit_bytes=VMEM_LIMIT_BYTES),
        name="out_proj",
    )(x2d, ya, yb, ym, w_out)


def _ffn_kernel(h_ref, g_ref, wg_ref, wu_ref, wd_ref, o_ref, fn_ref):
    j = pl.program_id(1)

    @pl.when(j == 0)
    def _():
        h = h_ref[...]
        fn_ref[...] = _rms(h, g_ref[...]).astype(BF16)
        o_ref[...] = h

    fn = fn_ref[...]
    gate = _dot(fn, wg_ref[...])
    up = _dot(fn, wu_ref[...])
    act = (gate * jax.nn.sigmoid(gate) * up).astype(BF16)
    o_ref[...] += _dot(act, wd_ref[...])


def _ffn(h, g, wg, wu, wd, tm, tf):
    t, d = h.shape
    dff = wg.shape[1]
    return pl.pallas_call(
        _ffn_kernel,
        grid=(t // tm, dff // tf),
        in_specs=[pl.BlockSpec((tm, d), lambda i, j: (i, 0)),
                  _const_spec((1, d)),
                  pl.BlockSpec((d, tf), lambda i, j: (0, j)),
                  pl.BlockSpec((d, tf), lambda i, j: (0, j)),
                  pl.BlockSpec((tf, d), lambda i, j: (j, 0))],
        out_specs=pl.BlockSpec((tm, d), lambda i, j: (i, 0)),
        out_shape=jax.ShapeDtypeStruct((t, d), F32),
        scratch_shapes=[pltpu.VMEM((tm, d), BF16)],
        compiler_params=pltpu.CompilerParams(dimension_semantics=("arbitrary", "arbitrary"),
                                             vmem_limit_bytes=VMEM_LIMIT_BYTES),
        name="ffn",
    )(h, g, wg, wu, wd)


def _tile_row(v, width):
    v = v.astype(F32).reshape(-1)
    return jnp.tile(v, width // v.shape[0])


def _layer(h, mem2d, pos_col, pos_row, batch, seq, m_len, p):
    (attn_norm_g, w_in, swa_q_g, swa_k_g, swa_sinks, cq_g, ckv_g, w_uq, w_ukv, qn_g, qr_g, kn_g, kr_g,
     mem_g, w_mem_kv, memq_g, memk_g, w_out, ffn_g, w_gate, w_up, w_down) = p
    width = 2 * LANES
    inv_freq = ROPE_THETA ** (-jnp.arange(0, MLA_ROPE_DIM, 2, dtype=F32) / MLA_ROPE_DIM)
    sign = jnp.concatenate([-jnp.ones((MLA_ROPE_DIM // 2,), F32), jnp.ones((MLA_ROPE_DIM // 2,), F32)])
    rows = [swa_q_g, swa_k_g, qn_g, qr_g, kn_g, kr_g, memq_g, inv_freq, sign]
    sp = jnp.stack([_tile_row(r, width) for r in rows]
                   + [jnp.zeros((width,), F32)] * (_SP_ROWS - len(rows)))

    w1 = w_in[:, :_W1_COLS].astype(BF16)
    w_kr = w_in[:, _W1_COLS:_W1_COLS + MLA_ROPE_DIM]
    w2 = jnp.concatenate([w_kr, w_kr, w_in[:, _W1_COLS + MLA_ROPE_DIM:]], axis=1).astype(BF16)
    qd = MLA_NOPE_DIM + MLA_ROPE_DIM
    wq3 = w_uq.reshape(w_uq.shape[0], MLA_HEADS, qd)
    wq = jnp.concatenate([wq3, wq3[:, :, MLA_NOPE_DIM:]], axis=2).reshape(w_uq.shape[0], -1).astype(BF16)
    wkv3 = w_ukv.reshape(w_ukv.shape[0], MLA_HEADS, MLA_NOPE_DIM + MLA_V_DIM)
    wkv = jnp.concatenate([wkv3[:, :, :MLA_NOPE_DIM].reshape(w_ukv.shape[0], -1),
                           wkv3[:, :, MLA_NOPE_DIM:].reshape(w_ukv.shape[0], -1)], axis=1).astype(BF16)
    idx = jnp.arange(MXU_DIM) // SWA_HEAD_DIM
    bd = (idx[:, None] == idx[None, :]).astype(BF16)
    slopes = 2.0 ** (-8.0 * jnp.arange(1, SWA_Q_HEADS + 1, dtype=F32) / SWA_Q_HEADS)
    slope_sink = jnp.stack([slopes, swa_sinks.astype(F32)]) * LOG2E

    km, vm = _mem_kv(mem2d, mem_g.reshape(1, -1), w_mem_kv, memk_g.reshape(1, -1), batch, m_len)
    qa, ka, va, krd, qm, qb, kn, vb, wg16, wu16, wd16, wo16 = _in_proj(
        h, pos_col, attn_norm_g.reshape(1, -1), sp, w1, w2, wq, wkv, cq_g.reshape(1, -1),
        ckv_g.reshape(1, -1), bd, (w_gate, w_up, w_down, w_out), tm=512)
    ya = _swa_attn(slope_sink, qa, ka, va, pos_col, pos_row, batch, seq, tq=512)
    yb = _mla_attn(qb, kn, krd, vb, batch, seq, tq=1024, hps=2)
    ym = _mem_attn(qm, km, vm, batch, seq, m_len, tq=1024)
    h = _out_proj(h, ya, yb, ym, wo16, tm=512)
    return _ffn(h, ffn_g.reshape(1, -1), wg16, wu16, wd16, tm=1024, tf=512)


def kernel(x, mem, positions, attn_norm_g, w_in, swa_q_norm_g, swa_k_norm_g, swa_sinks, mla_cq_norm_g,
           mla_ckv_norm_g, w_uq, w_ukv, mla_qn_norm_g, mla_qr_norm_g, mla_kn_norm_g, mla_kr_norm_g,
           mem_norm_g, w_mem_kv, mem_q_norm_g, mem_k_norm_g, w_out, ffn_norm_g, w_gate, w_up, w_down):
    batch, seq, d = x.shape
    m_len = mem.shape[1]
    stacked = (attn_norm_g, w_in, swa_q_norm_g, swa_k_norm_g, swa_sinks, mla_cq_norm_g, mla_ckv_norm_g,
               w_uq, w_ukv, mla_qn_norm_g, mla_qr_norm_g, mla_kn_norm_g, mla_kr_norm_g, mem_norm_g,
               w_mem_kv, mem_q_norm_g, mem_k_norm_g, w_out, ffn_norm_g, w_gate, w_up, w_down)
    h = x.reshape(batch * seq, d)
    mem2d = mem.reshape(batch * m_len, d)
    pos_col = positions.reshape(batch * seq, 1)
    pos_row = positions.reshape(batch, 1, seq)
    for l in range(attn_norm_g.shape[0]):
        h = _layer(h, mem2d, pos_col, pos_row, batch, seq, m_len, tuple(a[l] for a in stacked))
    return h.reshape(batch, seq, d)
```

```python
import functools

import jax
import jax.numpy as jnp
from jax import lax
from jax.experimental import pallas as pl
from jax.experimental.pallas import tpu as pltpu

EPS = 1e-6
NEG_INF = -1e30
LOG2E = 1.4426950408889634
BLOCK = 128
WINDOW = 128

SWA_Q_HEADS = 16
SWA_KV_HEADS = 2
SWA_HEAD_DIM = 64

MLA_HEADS = 4
MLA_NOPE_DIM = 128
MLA_ROPE_DIM = 64
MLA_V_DIM = 128
ROPE_THETA = 10000.0

MEM_HEADS = 4
MEM_HEAD_DIM = 128

LANES = 128
MXU_DIM = 256
VMEM_LIMIT_BYTES = 56 * 1024 * 1024

F32 = jnp.float32
BF16 = jnp.bfloat16

_SP_SWA_Q, _SP_SWA_K, _SP_QN, _SP_QR, _SP_KN, _SP_KR, _SP_MEMQ, _SP_FREQ, _SP_SIGN = range(9)
_SP_ROWS = 16


def _dot(a, b):
    return jnp.dot(a, b, preferred_element_type=F32)


def _dot_nt(a, b):
    return lax.dot_general(a, b, (((1,), (1,)), ((), ())), preferred_element_type=F32)


def _rms(x, g):
    return x * lax.rsqrt(jnp.mean(x * x, axis=-1, keepdims=True) + EPS) * g


def _const_spec(shape):
    nd = len(shape)
    return pl.BlockSpec(shape, lambda *_: (0,) * nd, pipeline_mode=pl.Buffered(1))


def _mem_kv_kernel(mem_ref, g_ref, w32_ref, gk_ref, win_ref, k_ref, v_ref, w1_ref, w2_ref, w_ref):
    @pl.when(pl.program_id(0) == 0)
    def _():
        w_ref[...] = w32_ref[...].astype(BF16)

    w1_ref[...] = win_ref[:, :_W1_COLS].astype(BF16)
    half = MLA_ROPE_DIM
    lo = lax.broadcasted_iota(jnp.int32, (1, LANES), 1) < half
    n_full = (win_ref.shape[1] - _W1_COLS) // LANES
    tail = [win_ref[:, _W1_COLS + k * LANES:_W1_COLS + (k + 1) * LANES] for k in range(n_full)]
    last = win_ref[:, _W1_COLS + n_full * LANES:]
    tail.append(jnp.concatenate([last, last], axis=1))
    swapped = [pltpu.roll(c, half, 1) for c in tail[:n_full]]
    w2_ref[:, :LANES] = jnp.where(lo, tail[0], swapped[0]).astype(BF16)
    for k in range(1, n_full + 1):
        nxt = swapped[k] if k < n_full else tail[k]
        w2_ref[:, k * LANES:(k + 1) * LANES] = jnp.where(lo, swapped[k - 1], nxt).astype(BF16)

    xn = _rms(mem_ref[...], g_ref[...]).astype(BF16)
    kv = _dot(xn, w_ref[...])
    hd = MEM_HEADS * MEM_HEAD_DIM
    for h in range(MEM_HEADS):
        sl = slice(h * MEM_HEAD_DIM, (h + 1) * MEM_HEAD_DIM)
        k_ref[:, sl] = _rms(kv[:, sl], gk_ref[...]).astype(BF16)
        v_ref[:, 2 * h * LANES:(2 * h + 1) * LANES] = kv[:, hd + h * MEM_HEAD_DIM:
                                                         hd + (h + 1) * MEM_HEAD_DIM].astype(BF16)
        v_ref[:, (2 * h + 1) * LANES:(2 * h + 2) * LANES] = jnp.ones((kv.shape[0], LANES), BF16)


def _mem_kv(mem2d, g, w, gk, w_in, batch, m_len):
    d = mem2d.shape[1]
    hd = MEM_HEADS * MEM_HEAD_DIM
    rows, cols = w_in.shape
    assert rows % (16 * batch) == 0 and (cols - _W1_COLS) % LANES == MLA_ROPE_DIM
    slab = rows // batch
    w2_cols = cols - _W1_COLS + MLA_ROPE_DIM
    return pl.pallas_call(
        _mem_kv_kernel,
        grid=(batch,),
        in_specs=[pl.BlockSpec((m_len, d), lambda b: (b, 0)),
                  _const_spec((1, d)), _const_spec((d, 2 * hd)), _const_spec((1, MEM_HEAD_DIM)),
                  pl.BlockSpec((slab, cols), lambda b: (b, 0))],
        out_specs=[pl.BlockSpec((m_len, hd), lambda b: (b, 0)), pl.BlockSpec((m_len, 2 * hd), lambda b: (b, 0)),
                   pl.BlockSpec((slab, _W1_COLS), lambda b: (b, 0)), pl.BlockSpec((slab, w2_cols), lambda b: (b, 0))],
        out_shape=[jax.ShapeDtypeStruct((batch * m_len, hd), BF16),
                   jax.ShapeDtypeStruct((batch * m_len, 2 * hd), BF16),
                   jax.ShapeDtypeStruct((rows, _W1_COLS), BF16), jax.ShapeDtypeStruct((rows, w2_cols), BF16)],
        scratch_shapes=[pltpu.VMEM((d, 2 * hd), BF16)],
        compiler_params=pltpu.CompilerParams(dimension_semantics=("arbitrary",),
                                             vmem_limit_bytes=VMEM_LIMIT_BYTES),
        name="mem_kv",
    )(mem2d, g, w, gk, w_in)


_C_QA = (0, 1024)
_C_KVA = (1024, 1280)
_C_CQ = (1280, 1792)
_C_CKV = (1792, 2304)
_W1_COLS = 2304
_C_KR = (2304, 2432)
_C_QM = (2432, 2944)


def _in_proj_kernel(x_ref, pos_ref, gattn_ref, sp_ref, w1_ref, w2_ref, wq_ref, wkv_ref, gcq_ref, gckv_ref,
                    bd_ref, wg32_ref, wu32_ref, wd32_ref, wo32_ref,
                    qa_ref, ka_ref, va_ref, krd_ref, qm_ref, qb_ref, kn_ref, vb_ref,
                    wg16_ref, wu16_ref, wd16_ref, wo16_ref):
    wg16_ref[...] = wg32_ref[...].astype(BF16)
    wu16_ref[...] = wu32_ref[...].astype(BF16)
    wd16_ref[...] = wd32_ref[...].astype(BF16)
    wo16_ref[...] = wo32_ref[...].astype(BF16)
    sp = sp_ref[...]
    row = lambda r, n=LANES: sp[r:r + 1, :n]
    hn = _rms(x_ref[...], gattn_ref[...]).astype(BF16)

    def proj(cols):
        if cols[0] >= _W1_COLS:
            return _dot(hn, w2_ref[:, cols[0] - _W1_COLS:cols[1] - _W1_COLS])
        return _dot(hn, w1_ref[:, cols[0]:cols[1]])

    def head64_rms(blk, g):
        ss = _dot((blk * blk).astype(BF16), bd_ref[...])
        return blk * lax.rsqrt(ss * (1.0 / SWA_HEAD_DIM) + EPS) * g

    ang = pos_ref[...].astype(F32) * row(_SP_FREQ)
    cos = jnp.cos(ang)
    sin_signed = jnp.sin(ang) * row(_SP_SIGN)

    def rope_dup(y):
        return y * cos + pltpu.roll(y, MLA_ROPE_DIM // 2, 1) * sin_signed

    cq = proj(_C_CQ)
    ckv = proj(_C_CKV)
    qm = proj(_C_QM)
    kva = proj(_C_KVA)
    kr = proj(_C_KR)
    cqn = _rms(cq, gcq_ref[...]).astype(BF16)
    ckvn = _rms(ckv, gckv_ref[...]).astype(BF16)

    for h in range(MEM_HEADS):
        sl = slice(h * MEM_HEAD_DIM, (h + 1) * MEM_HEAD_DIM)
        qm_ref[:, sl] = (_rms(qm[:, sl], row(_SP_MEMQ)) * (LOG2E * MEM_HEAD_DIM ** -0.5)).astype(BF16)
    lo = lax.broadcasted_iota(jnp.int32, (1, LANES), 1) < SWA_HEAD_DIM
    for src, dst in ((head64_rms(kva, row(_SP_SWA_K, MXU_DIM))[:, :LANES], ka_ref), (kva[:, LANES:], va_ref)):
        swapped = pltpu.roll(src, SWA_HEAD_DIM, 1)
        dst[:, :LANES] = jnp.where(lo, src, swapped).astype(BF16)
        dst[:, LANES:] = jnp.where(lo, swapped, src).astype(BF16)
    krd_ref[...] = rope_dup(_rms(kr, row(_SP_KR))).astype(BF16)

    qb = _dot(cqn, wq_ref[...])
    kvb = _dot(ckvn, wkv_ref[...])
    qa = proj(_C_QA)

    scale = LOG2E * (MLA_NOPE_DIM + MLA_ROPE_DIM) ** -0.5
    for h in range(MLA_HEADS):
        o = h * 2 * LANES
        qb_ref[:, o:o + LANES] = (_rms(qb[:, o:o + LANES], row(_SP_QN)) * scale).astype(BF16)
        qr = rope_dup(_rms(qb[:, o + LANES:o + 2 * LANES], row(_SP_QR)))
        qb_ref[:, o + LANES:o + 2 * LANES] = (qr * (0.5 * scale)).astype(BF16)
    hn_cols = MLA_HEADS * MLA_NOPE_DIM
    for h in range(MLA_HEADS):
        sl = slice(h * MLA_NOPE_DIM, (h + 1) * MLA_NOPE_DIM)
        kn_ref[:, sl] = _rms(kvb[:, sl], row(_SP_KN)).astype(BF16)
        vb_ref[:, 2 * h * LANES:(2 * h + 1) * LANES] = kvb[:, hn_cols + h * MLA_V_DIM:
                                                          hn_cols + (h + 1) * MLA_V_DIM].astype(BF16)
        vb_ref[:, (2 * h + 1) * LANES:(2 * h + 2) * LANES] = jnp.ones((kvb.shape[0], LANES), BF16)

    for c in range(_C_QA[1] // MXU_DIM):
        sl = slice(c * MXU_DIM, (c + 1) * MXU_DIM)
        qa_ref[:, sl] = (head64_rms(qa[:, sl], row(_SP_SWA_Q, MXU_DIM))
                         * (LOG2E * SWA_HEAD_DIM ** -0.5)).astype(BF16)


def _in_proj(x2d, pos_col, gattn, sp, w1, w2, wq, wkv, gcq, gckv, bd, late_w, tm):
    t, d = x2d.shape
    steps = t // tm
    widths = (1024, 256, 256, 128, 512, 1024, 512, 1024)
    rows = lambda w: pl.BlockSpec((tm, w), lambda i: (i, 0))
    slabs = []
    for w in late_w:
        assert w.shape[0] % (16 * steps) == 0, (w.shape, steps)
        slabs.append(pl.BlockSpec((w.shape[0] // steps, w.shape[1]), lambda i: (i, 0)))
    return pl.pallas_call(
        _in_proj_kernel,
        grid=(steps,),
        in_specs=[rows(d), rows(1), _const_spec(gattn.shape), _const_spec(sp.shape),
                  _const_spec(w1.shape), _const_spec(w2.shape), _const_spec(wq.shape),
                  _const_spec(wkv.shape),
                  _const_spec(gcq.shape), _const_spec(gckv.shape), _const_spec(bd.shape)] + slabs,
        out_specs=[rows(w) for w in widths] + slabs,
        out_shape=[jax.ShapeDtypeStruct((t, w), BF16) for w in widths]
        + [jax.ShapeDtypeStruct(w.shape, BF16) for w in late_w],
        compiler_params=pltpu.CompilerParams(dimension_semantics=("arbitrary",),
                                             vmem_limit_bytes=VMEM_LIMIT_BYTES),
        name="in_proj",
    )(x2d, pos_col, gattn, sp, w1, w2, wq, wkv, gcq, gckv, bd, *late_w)


_SWA_MASK_DIST = -NEG_INF * 2.0 ** 8


def _swa_kernel(ss_ref, q_ref, kc_ref, kp_ref, vc_ref, vp_ref, pq_ref, pkc_ref, pkp_ref, o_ref, *, tq):
    i = pl.program_id(1)
    g = SWA_Q_HEADS // SWA_KV_HEADS
    kk = lax.broadcasted_iota(jnp.int32, (BLOCK, BLOCK), 0)
    qq = lax.broadcasted_iota(jnp.int32, (BLOCK, BLOCK), 1)
    from_prev = kk > qq
    prev_w = jnp.where(from_prev, 1.0, 0.0).astype(BF16)
    cur_w = jnp.where(from_prev, 0.0, 1.0).astype(BF16)
    lane_lo = lax.broadcasted_iota(jnp.int32, (BLOCK, LANES), 1) < SWA_HEAD_DIM
    lo_w = jnp.where(lane_lo, 1.0, 0.0).astype(BF16)
    hi_w = jnp.where(lane_lo, 0.0, 1.0).astype(BF16)

    for j in range(tq // BLOCK):
        rows = slice(j * BLOCK, (j + 1) * BLOCK)
        prows = slice((j - 1) * BLOCK, j * BLOCK)
        k_prev, v_prev, pk_prev = ((kp_ref[...], vp_ref[...], pkp_ref[...]) if j == 0 else
                                   (kc_ref[prows, :], vc_ref[prows, :], pkc_ref[prows, :]))
        kj = jnp.concatenate([k_prev, kc_ref[rows, :]], axis=0)
        vj = jnp.concatenate([v_prev, vc_ref[rows, :]], axis=0)
        pq = pq_ref[:, rows]
        dist = jnp.where(from_prev, jnp.abs(pk_prev - pq), jnp.abs(pkc_ref[rows, :] - pq)).astype(F32)
        if j == 0:
            dist = jnp.where(from_prev, jnp.maximum(dist, jnp.where(i == 0, _SWA_MASK_DIST, 0.0)), dist)
        for kv in range(SWA_KV_HEADS):
            kvl = slice(kv * LANES, (kv + 1) * LANES)
            qm = jnp.concatenate(
                [q_ref[rows, (kv * g + hh) // 2 * LANES:((kv * g + hh) // 2 + 1) * LANES]
                 * (lo_w if hh % 2 == 0 else hi_w) for hh in range(g)], axis=0)
            st = _dot_nt(kj[:, kvl], qm)
            pts = []
            for hh in range(g):
                head = kv * g + hh
                cols = slice(hh * BLOCK, (hh + 1) * BLOCK)
                t = jnp.where(from_prev, st[:BLOCK, cols], st[BLOCK:, cols]) - ss_ref[0, head] * dist
                sink = ss_ref[1, head]
                m = jnp.maximum(jnp.max(t, axis=0, keepdims=True), sink)
                e = jnp.exp2(t - m)
                den = jnp.sum(e, axis=0, keepdims=True) + jnp.exp2(sink - m)
                p = (e * (1.0 / den)).astype(BF16)
                pts.append(jnp.concatenate([p * prev_w, p * cur_w], axis=0))
            pt = jnp.concatenate(pts, axis=1)
            o = lax.dot_general(pt, vj[:, kvl], (((0,), (0,)), ((), ())), preferred_element_type=F32)
            for pr in range(g // 2):
                grp = kv * (g // 2) + pr
                even = o[(2 * pr) * BLOCK:(2 * pr + 1) * BLOCK]
                odd = o[(2 * pr + 1) * BLOCK:(2 * pr + 2) * BLOCK]
                o_ref[rows, grp * LANES:(grp + 1) * LANES] = jnp.where(lane_lo, even, odd).astype(BF16)


def _swa_attn(slope_sink, qa, ka, va, pos_col, pos_row, batch, seq, tq):
    t = batch * seq
    nq = seq // tq
    nb = tq // BLOCK
    cur = lambda w: pl.BlockSpec((tq, w), lambda b, i: (b * nq + i, 0))
    prev = lambda w: pl.BlockSpec(
        (BLOCK, w), lambda b, i: (b * nq * nb + jnp.maximum(i * nb - 1, 0), 0))
    return pl.pallas_call(
        functools.partial(_swa_kernel, tq=tq),
        grid=(batch, nq),
        in_specs=[pl.BlockSpec(memory_space=pltpu.SMEM),
                  cur(SWA_Q_HEADS * SWA_HEAD_DIM), cur(2 * LANES), prev(2 * LANES), cur(2 * LANES),
                  prev(2 * LANES),
                  pl.BlockSpec((None, 1, tq), lambda b, i: (b, 0, i)),
                  cur(1), prev(1)],
        out_specs=cur(SWA_Q_HEADS * SWA_HEAD_DIM),
        out_shape=jax.ShapeDtypeStruct((t, SWA_Q_HEADS * SWA_HEAD_DIM), BF16),
        compiler_params=pltpu.CompilerParams(dimension_semantics=("arbitrary", "arbitrary"),
                                             vmem_limit_bytes=VMEM_LIMIT_BYTES),
        name="swa_attn",
    )(slope_sink, qa, ka, ka, va, va, pos_row, pos_col, pos_col)


def _mla_kernel(q_ref, kn_ref, kr_ref, v_ref, o_ref, s_sc, m_sc, acc_sc, *, tq, hps):
    qi = pl.program_id(2)
    qw = 2 * LANES
    tk = tq // 2

    def scores_to(slot, j, row0=0):
        k0 = pl.multiple_of(j * tk, tk)
        kr = kr_ref[pl.ds(k0, tk), :]
        for hh in range(hps):
            k = jnp.concatenate([kn_ref[pl.ds(k0, tk), hh * LANES:(hh + 1) * LANES], kr], axis=1)
            s_sc[slot, hh, row0:, :] = _dot_nt(q_ref[row0:, hh * qw:(hh + 1) * qw], k)

    def update_from(slot, j, row0=0, masked=False):
        k0 = pl.multiple_of(j * tk, tk)
        for hh in range(hps):
            s = s_sc[slot, hh, row0:, :]
            if masked:
                r = lax.broadcasted_iota(jnp.int32, s.shape, 0)
                c = lax.broadcasted_iota(jnp.int32, s.shape, 1)
                s = jnp.where(c <= r, s, NEG_INF)
            m = m_sc[hh, row0:, :]
            m_new = jnp.maximum(m, jnp.max(s, axis=-1, keepdims=True))
            p = jnp.exp2(s - m_new)
            m_sc[hh, row0:, :] = m_new
            acc_sc[hh, row0:, :] = jnp.exp2(m - m_new) * acc_sc[hh, row0:, :] + _dot(
                p.astype(BF16), v_ref[pl.ds(k0, tk), hh * qw:(hh + 1) * qw])

    m_sc[...] = jnp.full(m_sc.shape, NEG_INF, F32)
    acc_sc[...] = jnp.zeros(acc_sc.shape, F32)
    scores_to(0, 0)

    def pair(p, carry):
        j = 2 * p
        scores_to(1, j + 1)
        update_from(0, j)
        scores_to(0, j + 2)
        update_from(1, j + 1)
        return carry

    lax.fori_loop(0, qi, pair, 0)
    scores_to(1, 2 * qi + 1, row0=tk)
    update_from(0, 2 * qi, masked=True)
    update_from(1, 2 * qi + 1, row0=tk, masked=True)

    for hh in range(hps):
        acc = acc_sc[hh]
        o_ref[:, hh * MLA_V_DIM:(hh + 1) * MLA_V_DIM] = (acc[:, :MLA_V_DIM] / acc[:, MLA_V_DIM:]).astype(BF16)


def _mla_attn(qb, kn, krd, vb, batch, seq, tq, hps):
    t = batch * seq
    nq = seq // tq
    assert tq % 2 == 0 and seq % tq == 0
    return pl.pallas_call(
        functools.partial(_mla_kernel, tq=tq, hps=hps),
        grid=(batch, MLA_HEADS // hps, nq),
        in_specs=[pl.BlockSpec((tq, 2 * LANES * hps), lambda b, h, i: (b * nq + i, h)),
                  pl.BlockSpec((seq, MLA_NOPE_DIM * hps), lambda b, h, i: (b, h)),
                  pl.BlockSpec((seq, LANES), lambda b, h, i: (b, 0)),
                  pl.BlockSpec((seq, 2 * LANES * hps), lambda b, h, i: (b, h))],
        out_specs=pl.BlockSpec((tq, MLA_V_DIM * hps), lambda b, h, i: (b * nq + i, h)),
        out_shape=jax.ShapeDtypeStruct((t, MLA_HEADS * MLA_V_DIM), BF16),
        scratch_shapes=[pltpu.VMEM((2, hps, tq, tq // 2), F32), pltpu.VMEM((hps, tq, 1), F32),
                        pltpu.VMEM((hps, tq, 2 * LANES), F32)],
        compiler_params=pltpu.CompilerParams(dimension_semantics=("arbitrary",) * 3,
                                             vmem_limit_bytes=VMEM_LIMIT_BYTES),
        name="mla_attn",
    )(qb, kn, krd, vb)


def _mem_attn_kernel(q_ref, k_ref, v_ref, o_ref):
    for h in range(MEM_HEADS):
        sl = slice(h * MEM_HEAD_DIM, (h + 1) * MEM_HEAD_DIM)
        s = _dot_nt(q_ref[:, sl], k_ref[:, sl])
        e = jnp.exp2(s - jnp.max(s, axis=-1, keepdims=True))
        o = _dot(e.astype(BF16), v_ref[:, 2 * h * LANES:(2 * h + 2) * LANES])
        o_ref[:, sl] = (o[:, :MEM_HEAD_DIM] / o[:, MEM_HEAD_DIM:]).astype(BF16)


def _mem_attn(qm, km, vm, batch, seq, m_len, tq):
    t = batch * seq
    nq = seq // tq
    hd = MEM_HEADS * MEM_HEAD_DIM
    return pl.pallas_call(
        _mem_attn_kernel,
        grid=(batch, nq),
        in_specs=[pl.BlockSpec((tq, hd), lambda b, i: (b * nq + i, 0)),
                  pl.BlockSpec((m_len, hd), lambda b, i: (b, 0)),
                  pl.BlockSpec((m_len, 2 * hd), lambda b, i: (b, 0))],
        out_specs=pl.BlockSpec((tq, hd), lambda b, i: (b * nq + i, 0)),
        out_shape=jax.ShapeDtypeStruct((t, hd), BF16),
        compiler_params=pltpu.CompilerParams(dimension_semantics=("arbitrary", "arbitrary"),
                                             vmem_limit_bytes=VMEM_LIMIT_BYTES),
        name="mem_attn",
    )(qm, km, vm)


def _out_proj_kernel(x_ref, ya_ref, yb_ref, ym_ref, w_ref, h_ref):
    na = ya_ref.shape[1]
    nb = na + yb_ref.shape[1]
    h_ref[...] = (x_ref[...] + _dot(ya_ref[...], w_ref[0:na, :]) + _dot(yb_ref[...], w_ref[na:nb, :])
                  + _dot(ym_ref[...], w_ref[nb:, :]))


def _out_proj(x2d, ya, yb, ym, w_out, tm):
    t, d = x2d.shape
    rows = lambda w: pl.BlockSpec((tm, w), lambda i: (i, 0))
    return pl.pallas_call(
        _out_proj_kernel,
        grid=(t // tm,),
        in_specs=[rows(d), rows(ya.shape[1]), rows(yb.shape[1]), rows(ym.shape[1]),
                  _const_spec(w_out.shape)],
        out_specs=rows(d),
        out_shape=jax.ShapeDtypeStruct((t, d), F32),
        compiler_params=pltpu.CompilerParams(dimension_semantics=("arbitrary",),
                                             vmem_limit_bytes=VMEM_LIMIT_BYTES),
        name="out_proj",
    )(x2d, ya, yb, ym, w_out)


def _ffn_kernel(h_ref, g_ref, wg_ref, wu_ref, wd_ref, o_ref, fn_ref):
    j = pl.program_id(1)

    @pl.when(j == 0)
    def _():
        h = h_ref[...]
        fn_ref[...] = _rms(h, g_ref[...]).astype(BF16)
        o_ref[...] = h

    fn = fn_ref[...]
    gate = _dot(fn, wg_ref[...])
    up = _dot(fn, wu_ref[...])
    act = (gate * jax.nn.sigmoid(gate) * up).astype(BF16)
    o_ref[...] += _dot(act, wd_ref[...])


def _ffn(h, g, wg, wu, wd, tm, tf):
    t, d = h.shape
    dff = wg.shape[1]
    return pl.pallas_call(
        _ffn_kernel,
        grid=(t // tm, dff // tf),
        in_specs=[pl.BlockSpec((tm, d), lambda i, j: (i, 0)),
                  _const_spec((1, d)),
                  pl.BlockSpec((d, tf), lambda i, j: (0, j)),
                  pl.BlockSpec((d, tf), lambda i, j: (0, j)),
                  pl.BlockSpec((tf, d), lambda i, j: (j, 0))],
        out_specs=pl.BlockSpec((tm, d), lambda i, j: (i, 0)),
        out_shape=jax.ShapeDtypeStruct((t, d), F32),
        scratch_shapes=[pltpu.VMEM((tm, d), BF16)],
        compiler_params=pltpu.CompilerParams(dimension_semantics=("arbitrary", "arbitrary"),
                                             vmem_limit_bytes=VMEM_LIMIT_BYTES),
        name="ffn",
    )(h, g, wg, wu, wd)


def _tile_row(v, width):
    v = v.astype(F32).reshape(-1)
    return jnp.tile(v, width // v.shape[0])


def _layer(h, mem2d, pos_col, pos_row, batch, seq, m_len, p):
    (attn_norm_g, w_in, swa_q_g, swa_k_g, swa_sinks, cq_g, ckv_g, w_uq, w_ukv, qn_g, qr_g, kn_g, kr_g,
     mem_g, w_mem_kv, memq_g, memk_g, w_out, ffn_g, w_gate, w_up, w_down) = p
    width = 2 * LANES
    inv_freq = ROPE_THETA ** (-jnp.arange(0, MLA_ROPE_DIM, 2, dtype=F32) / MLA_ROPE_DIM)
    sign = jnp.concatenate([-jnp.ones((MLA_ROPE_DIM // 2,), F32), jnp.ones((MLA_ROPE_DIM // 2,), F32)])
    rows = [swa_q_g, swa_k_g, qn_g, qr_g, kn_g, kr_g, memq_g, inv_freq, sign]
    sp = jnp.stack([_tile_row(r, width) for r in rows]
                   + [jnp.zeros((width,), F32)] * (_SP_ROWS - len(rows)))

    qd = MLA_NOPE_DIM + MLA_ROPE_DIM
    wq3 = w_uq.reshape(w_uq.shape[0], MLA_HEADS, qd)
    wq = jnp.concatenate([wq3, wq3[:, :, MLA_NOPE_DIM:]], axis=2).reshape(w_uq.shape[0], -1).astype(BF16)
    wkv3 = w_ukv.reshape(w_ukv.shape[0], MLA_HEADS, MLA_NOPE_DIM + MLA_V_DIM)
    wkv = jnp.concatenate([wkv3[:, :, :MLA_NOPE_DIM].reshape(w_ukv.shape[0], -1),
                           wkv3[:, :, MLA_NOPE_DIM:].reshape(w_ukv.shape[0], -1)], axis=1).astype(BF16)
    idx = jnp.arange(MXU_DIM) // SWA_HEAD_DIM
    bd = (idx[:, None] == idx[None, :]).astype(BF16)
    slopes = 2.0 ** (-8.0 * jnp.arange(1, SWA_Q_HEADS + 1, dtype=F32) / SWA_Q_HEADS)
    slope_sink = jnp.stack([slopes, swa_sinks.astype(F32)]) * LOG2E

    km, vm, w1, w2 = _mem_kv(mem2d, mem_g.reshape(1, -1), w_mem_kv, memk_g.reshape(1, -1), w_in,
                             batch, m_len)
    qa, ka, va, krd, qm, qb, kn, vb, wg16, wu16, wd16, wo16 = _in_proj(
        h, pos_col, attn_norm_g.reshape(1, -1), sp, w1, w2, wq, wkv, cq_g.reshape(1, -1),
        ckv_g.reshape(1, -1), bd, (w_gate, w_up, w_down, w_out), tm=512)
    ya = _swa_attn(slope_sink, qa, ka, va, pos_col, pos_row, batch, seq, tq=512)
    yb = _mla_attn(qb, kn, krd, vb, batch, seq, tq=1024, hps=2)
    ym = _mem_attn(qm, km, vm, batch, seq, m_len, tq=1024)
    h = _out_proj(h, ya, yb, ym, wo16, tm=512)
    return _ffn(h, ffn_g.reshape(1, -1), wg16, wu16, wd16, tm=1024, tf=512)


def kernel(x, mem, positions, attn_norm_g, w_in, swa_q_norm_g, swa_k_norm_g, swa_sinks, mla_cq_norm_g,
           mla_ckv_norm_g, w_uq, w_ukv, mla_qn_norm_g, mla_qr_norm_g, mla_kn_norm_g, mla_kr_norm_g,
           mem_norm_g, w_mem_kv, mem_q_norm_g, mem_k_norm_g, w_out, ffn_norm_g, w_gate, w_up, w_down):
    batch, seq, d = x.shape
    m_len = mem.shape[1]
    stacked = (attn_norm_g, w_in, swa_q_norm_g, swa_k_norm_g, swa_sinks, mla_cq_norm_g, mla_ckv_norm_g,
               w_uq, w_ukv, mla_qn_norm_g, mla_qr_norm_g, mla_kn_norm_g, mla_kr_norm_g, mem_norm_g,
               w_mem_kv, mem_q_norm_g, mem_k_norm_g, w_out, ffn_norm_g, w_gate, w_up, w_down)
    h = x.reshape(batch * seq, d)
    mem2d = mem.reshape(batch * m_len, d)
    pos_col = positions.reshape(batch * seq, 1)
    pos_row = positions.reshape(batch, 1, seq)
    for l in range(attn_norm_g.shape[0]):
        h = _layer(h, mem2d, pos_col, pos_row, batch, seq, m_len, tuple(a[l] for a in stacked))
    return h.reshape(batch, seq, d)
```

```python
import functools

import jax
import jax.numpy as jnp
from jax import lax
from jax.experimental import pallas as pl
from jax.experimental.pallas import tpu as pltpu

EPS = 1e-6
NEG_INF = -1e30
LOG2E = 1.4426950408889634
BLOCK = 128
WINDOW = 128

SWA_Q_HEADS = 16
SWA_KV_HEADS = 2
SWA_HEAD_DIM = 64

MLA_HEADS = 4
MLA_NOPE_DIM = 128
MLA_ROPE_DIM = 64
MLA_V_DIM = 128
ROPE_THETA = 10000.0

MEM_HEADS = 4
MEM_HEAD_DIM = 128

LANES = 128
MXU_DIM = 256
VMEM_LIMIT_BYTES = 56 * 1024 * 1024

F32 = jnp.float32
BF16 = jnp.bfloat16

_SP_SWA_Q, _SP_SWA_K, _SP_QN, _SP_QR, _SP_KN, _SP_KR, _SP_MEMQ, _SP_FREQ, _SP_SIGN = range(9)
_SP_ROWS = 16


def _dot(a, b):
    return jnp.dot(a, b, preferred_element_type=F32)


def _dot_nt(a, b):
    return lax.dot_general(a, b, (((1,), (1,)), ((), ())), preferred_element_type=F32)


def _rms(x, g):
    return x * lax.rsqrt(jnp.mean(x * x, axis=-1, keepdims=True) + EPS) * g


def _const_spec(shape):
    nd = len(shape)
    return pl.BlockSpec(shape, lambda *_: (0,) * nd, pipeline_mode=pl.Buffered(1))


def _mem_kv_kernel(mem_ref, g_ref, w32_ref, gk_ref, win_ref, k_ref, v_ref, w1_ref, w2_ref, w_ref):
    @pl.when(pl.program_id(0) == 0)
    def _():
        w_ref[...] = w32_ref[...].astype(BF16)

    w1_ref[...] = win_ref[:, :_W1_COLS].astype(BF16)
    half = MLA_ROPE_DIM
    lo = lax.broadcasted_iota(jnp.int32, (1, LANES), 1) < half
    n_full = (win_ref.shape[1] - _W1_COLS) // LANES
    tail = [win_ref[:, _W1_COLS + k * LANES:_W1_COLS + (k + 1) * LANES] for k in range(n_full)]
    last = win_ref[:, _W1_COLS + n_full * LANES:]
    tail.append(jnp.concatenate([last, last], axis=1))
    swapped = [pltpu.roll(c, half, 1) for c in tail[:n_full]]
    w2_ref[:, :LANES] = jnp.where(lo, tail[0], swapped[0]).astype(BF16)
    for k in range(1, n_full + 1):
        nxt = swapped[k] if k < n_full else tail[k]
        w2_ref[:, k * LANES:(k + 1) * LANES] = jnp.where(lo, swapped[k - 1], nxt).astype(BF16)

    xn = _rms(mem_ref[...], g_ref[...]).astype(BF16)
    kv = _dot(xn, w_ref[...])
    hd = MEM_HEADS * MEM_HEAD_DIM
    for h in range(MEM_HEADS):
        sl = slice(h * MEM_HEAD_DIM, (h + 1) * MEM_HEAD_DIM)
        k_ref[:, sl] = _rms(kv[:, sl], gk_ref[...]).astype(BF16)
        v_ref[:, 2 * h * LANES:(2 * h + 1) * LANES] = kv[:, hd + h * MEM_HEAD_DIM:
                                                         hd + (h + 1) * MEM_HEAD_DIM].astype(BF16)
        v_ref[:, (2 * h + 1) * LANES:(2 * h + 2) * LANES] = jnp.ones((kv.shape[0], LANES), BF16)


def _mem_kv(mem2d, g, w, gk, w_in_stack, layer, batch, m_len):
    d = mem2d.shape[1]
    hd = MEM_HEADS * MEM_HEAD_DIM
    _, rows, cols = w_in_stack.shape
    assert rows % (16 * batch) == 0 and (cols - _W1_COLS) % LANES == MLA_ROPE_DIM
    slab = rows // batch
    w2_cols = cols - _W1_COLS + MLA_ROPE_DIM
    return pl.pallas_call(
        _mem_kv_kernel,
        grid=(batch,),
        in_specs=[pl.BlockSpec((m_len, d), lambda b: (b, 0)),
                  _const_spec((1, d)), _const_spec((d, 2 * hd)), _const_spec((1, MEM_HEAD_DIM)),
                  pl.BlockSpec((None, slab, cols), lambda b: (layer, b, 0))],
        out_specs=[pl.BlockSpec((m_len, hd), lambda b: (b, 0)), pl.BlockSpec((m_len, 2 * hd), lambda b: (b, 0)),
                   pl.BlockSpec((slab, _W1_COLS), lambda b: (b, 0)), pl.BlockSpec((slab, w2_cols), lambda b: (b, 0))],
        out_shape=[jax.ShapeDtypeStruct((batch * m_len, hd), BF16),
                   jax.ShapeDtypeStruct((batch * m_len, 2 * hd), BF16),
                   jax.ShapeDtypeStruct((rows, _W1_COLS), BF16), jax.ShapeDtypeStruct((rows, w2_cols), BF16)],
        scratch_shapes=[pltpu.VMEM((d, 2 * hd), BF16)],
        compiler_params=pltpu.CompilerParams(dimension_semantics=("arbitrary",),
                                             vmem_limit_bytes=VMEM_LIMIT_BYTES),
        name="mem_kv",
    )(mem2d, g, w, gk, w_in_stack)


_C_QA = (0, 1024)
_C_KVA = (1024, 1280)
_C_CQ = (1280, 1792)
_C_CKV = (1792, 2304)
_W1_COLS = 2304
_C_KR = (2304, 2432)
_C_QM = (2432, 2944)


def _in_proj_kernel(x_ref, pos_ref, gattn_ref, sp_ref, w1_ref, w2_ref, wq_ref, wkv_ref, gcq_ref, gckv_ref,
                    bd_ref, wg32_ref, wu32_ref, wd32_ref, wo32_ref,
                    qa_ref, ka_ref, va_ref, krd_ref, qm_ref, qb_ref, kn_ref, vb_ref,
                    wg16_ref, wu16_ref, wd16_ref, wo16_ref):
    wg16_ref[...] = wg32_ref[...].astype(BF16)
    wu16_ref[...] = wu32_ref[...].astype(BF16)
    wd16_ref[...] = wd32_ref[...].astype(BF16)
    wo16_ref[...] = wo32_ref[...].astype(BF16)
    sp = sp_ref[...]
    row = lambda r, n=LANES: sp[r:r + 1, :n]
    hn = _rms(x_ref[...], gattn_ref[...]).astype(BF16)

    def proj(cols):
        if cols[0] >= _W1_COLS:
            return _dot(hn, w2_ref[:, cols[0] - _W1_COLS:cols[1] - _W1_COLS])
        return _dot(hn, w1_ref[:, cols[0]:cols[1]])

    def head64_rms(blk, g):
        ss = _dot((blk * blk).astype(BF16), bd_ref[...])
        return blk * lax.rsqrt(ss * (1.0 / SWA_HEAD_DIM) + EPS) * g

    ang = pos_ref[...].astype(F32) * row(_SP_FREQ)
    cos = jnp.cos(ang)
    sin_signed = jnp.sin(ang) * row(_SP_SIGN)

    def rope_dup(y):
        return y * cos + pltpu.roll(y, MLA_ROPE_DIM // 2, 1) * sin_signed

    cq = proj(_C_CQ)
    ckv = proj(_C_CKV)
    qm = proj(_C_QM)
    kva = proj(_C_KVA)
    kr = proj(_C_KR)
    cqn = _rms(cq, gcq_ref[...]).astype(BF16)
    ckvn = _rms(ckv, gckv_ref[...]).astype(BF16)

    for h in range(MEM_HEADS):
        sl = slice(h * MEM_HEAD_DIM, (h + 1) * MEM_HEAD_DIM)
        qm_ref[:, sl] = (_rms(qm[:, sl], row(_SP_MEMQ)) * (LOG2E * MEM_HEAD_DIM ** -0.5)).astype(BF16)
    lo = lax.broadcasted_iota(jnp.int32, (1, LANES), 1) < SWA_HEAD_DIM
    for src, dst in ((head64_rms(kva, row(_SP_SWA_K, MXU_DIM))[:, :LANES], ka_ref), (kva[:, LANES:], va_ref)):
        swapped = pltpu.roll(src, SWA_HEAD_DIM, 1)
        dst[:, :LANES] = jnp.where(lo, src, swapped).astype(BF16)
        dst[:, LANES:] = jnp.where(lo, swapped, src).astype(BF16)
    krd_ref[...] = rope_dup(_rms(kr, row(_SP_KR))).astype(BF16)

    qb = _dot(cqn, wq_ref[...])
    kvb = _dot(ckvn, wkv_ref[...])
    qa = proj(_C_QA)

    scale = LOG2E * (MLA_NOPE_DIM + MLA_ROPE_DIM) ** -0.5
    for h in range(MLA_HEADS):
        o = h * 2 * LANES
        qb_ref[:, o:o + LANES] = (_rms(qb[:, o:o + LANES], row(_SP_QN)) * scale).astype(BF16)
        qr = rope_dup(_rms(qb[:, o + LANES:o + 2 * LANES], row(_SP_QR)))
        qb_ref[:, o + LANES:o + 2 * LANES] = (qr * (0.5 * scale)).astype(BF16)
    hn_cols = MLA_HEADS * MLA_NOPE_DIM
    for h in range(MLA_HEADS):
        sl = slice(h * MLA_NOPE_DIM, (h + 1) * MLA_NOPE_DIM)
        kn_ref[:, sl] = _rms(kvb[:, sl], row(_SP_KN)).astype(BF16)
        vb_ref[:, 2 * h * LANES:(2 * h + 1) * LANES] = kvb[:, hn_cols + h * MLA_V_DIM:
                                                          hn_cols + (h + 1) * MLA_V_DIM].astype(BF16)
        vb_ref[:, (2 * h + 1) * LANES:(2 * h + 2) * LANES] = jnp.ones((kvb.shape[0], LANES), BF16)

    for c in range(_C_QA[1] // MXU_DIM):
        sl = slice(c * MXU_DIM, (c + 1) * MXU_DIM)
        qa_ref[:, sl] = (head64_rms(qa[:, sl], row(_SP_SWA_Q, MXU_DIM))
                         * (LOG2E * SWA_HEAD_DIM ** -0.5)).astype(BF16)


def _in_proj(x2d, pos_col, gattn, sp, w1, w2, wq, wkv, gcq, gckv, bd, late_w, tm):
    t, d = x2d.shape
    steps = t // tm
    widths = (1024, 256, 256, 128, 512, 1024, 512, 1024)
    rows = lambda w: pl.BlockSpec((tm, w), lambda i: (i, 0))
    slabs = []
    for w in late_w:
        assert w.shape[0] % (16 * steps) == 0, (w.shape, steps)
        slabs.append(pl.BlockSpec((w.shape[0] // steps, w.shape[1]), lambda i: (i, 0)))
    return pl.pallas_call(
        _in_proj_kernel,
        grid=(steps,),
        in_specs=[rows(d), rows(1), _const_spec(gattn.shape), _const_spec(sp.shape),
                  _const_spec(w1.shape), _const_spec(w2.shape), _const_spec(wq.shape),
                  _const_spec(wkv.shape),
                  _const_spec(gcq.shape), _const_spec(gckv.shape), _const_spec(bd.shape)] + slabs,
        out_specs=[rows(w) for w in widths] + slabs,
        out_shape=[jax.ShapeDtypeStruct((t, w), BF16) for w in widths]
        + [jax.ShapeDtypeStruct(w.shape, BF16) for w in late_w],
        compiler_params=pltpu.CompilerParams(dimension_semantics=("arbitrary",),
                                             vmem_limit_bytes=VMEM_LIMIT_BYTES),
        name="in_proj",
    )(x2d, pos_col, gattn, sp, w1, w2, wq, wkv, gcq, gckv, bd, *late_w)


_SWA_MASK_DIST = -NEG_INF * 2.0 ** 8


def _swa_kernel(ss_ref, q_ref, kc_ref, kp_ref, vc_ref, vp_ref, pq_ref, pkc_ref, pkp_ref, o_ref, *, tq):
    i = pl.program_id(1)
    g = SWA_Q_HEADS // SWA_KV_HEADS
    kk = lax.broadcasted_iota(jnp.int32, (BLOCK, BLOCK), 0)
    qq = lax.broadcasted_iota(jnp.int32, (BLOCK, BLOCK), 1)
    from_prev = kk > qq
    prev_w = jnp.where(from_prev, 1.0, 0.0).astype(BF16)
    cur_w = jnp.where(from_prev, 0.0, 1.0).astype(BF16)
    lane_lo = lax.broadcasted_iota(jnp.int32, (BLOCK, LANES), 1) < SWA_HEAD_DIM
    lo_w = jnp.where(lane_lo, 1.0, 0.0).astype(BF16)
    hi_w = jnp.where(lane_lo, 0.0, 1.0).astype(BF16)

    for j in range(tq // BLOCK):
        rows = slice(j * BLOCK, (j + 1) * BLOCK)
        prows = slice((j - 1) * BLOCK, j * BLOCK)
        k_prev, v_prev, pk_prev = ((kp_ref[...], vp_ref[...], pkp_ref[...]) if j == 0 else
                                   (kc_ref[prows, :], vc_ref[prows, :], pkc_ref[prows, :]))
        kj = jnp.concatenate([k_prev, kc_ref[rows, :]], axis=0)
        vj = jnp.concatenate([v_prev, vc_ref[rows, :]], axis=0)
        pq = pq_ref[:, rows]
        dist = jnp.where(from_prev, jnp.abs(pk_prev - pq), jnp.abs(pkc_ref[rows, :] - pq)).astype(F32)
        if j == 0:
            dist = jnp.where(from_prev, jnp.maximum(dist, jnp.where(i == 0, _SWA_MASK_DIST, 0.0)), dist)
        for kv in range(SWA_KV_HEADS):
            kvl = slice(kv * LANES, (kv + 1) * LANES)
            qm = jnp.concatenate(
                [q_ref[rows, (kv * g + hh) // 2 * LANES:((kv * g + hh) // 2 + 1) * LANES]
                 * (lo_w if hh % 2 == 0 else hi_w) for hh in range(g)], axis=0)
            st = _dot_nt(kj[:, kvl], qm)
            pts = []
            for hh in range(g):
                head = kv * g + hh
                cols = slice(hh * BLOCK, (hh + 1) * BLOCK)
                t = jnp.where(from_prev, st[:BLOCK, cols], st[BLOCK:, cols]) - ss_ref[0, head] * dist
                sink = ss_ref[1, head]
                m = jnp.maximum(jnp.max(t, axis=0, keepdims=True), sink)
                e = jnp.exp2(t - m)
                den = jnp.sum(e, axis=0, keepdims=True) + jnp.exp2(sink - m)
                p = (e * (1.0 / den)).astype(BF16)
                pts.append(jnp.concatenate([p * prev_w, p * cur_w], axis=0))
            pt = jnp.concatenate(pts, axis=1)
            o = lax.dot_general(pt, vj[:, kvl], (((0,), (0,)), ((), ())), preferred_element_type=F32)
            for pr in range(g // 2):
                grp = kv * (g // 2) + pr
                even = o[(2 * pr) * BLOCK:(2 * pr + 1) * BLOCK]
                odd = o[(2 * pr + 1) * BLOCK:(2 * pr + 2) * BLOCK]
                o_ref[rows, grp * LANES:(grp + 1) * LANES] = jnp.where(lane_lo, even, odd).astype(BF16)


def _swa_attn(slope_sink, qa, ka, va, pos_col, pos_row, batch, seq, tq):
    t = batch * seq
    nq = seq // tq
    nb = tq // BLOCK
    cur = lambda w: pl.BlockSpec((tq, w), lambda b, i: (b * nq + i, 0))
    prev = lambda w: pl.BlockSpec(
        (BLOCK, w), lambda b, i: (b * nq * nb + jnp.maximum(i * nb - 1, 0), 0))
    return pl.pallas_call(
        functools.partial(_swa_kernel, tq=tq),
        grid=(batch, nq),
        in_specs=[pl.BlockSpec(memory_space=pltpu.SMEM),
                  cur(SWA_Q_HEADS * SWA_HEAD_DIM), cur(2 * LANES), prev(2 * LANES), cur(2 * LANES),
                  prev(2 * LANES),
                  pl.BlockSpec((None, 1, tq), lambda b, i: (b, 0, i)),
                  cur(1), prev(1)],
        out_specs=cur(SWA_Q_HEADS * SWA_HEAD_DIM),
        out_shape=jax.ShapeDtypeStruct((t, SWA_Q_HEADS * SWA_HEAD_DIM), BF16),
        compiler_params=pltpu.CompilerParams(dimension_semantics=("arbitrary", "arbitrary"),
                                             vmem_limit_bytes=VMEM_LIMIT_BYTES),
        name="swa_attn",
    )(slope_sink, qa, ka, ka, va, va, pos_row, pos_col, pos_col)


def _mla_kernel(q_ref, kn_ref, kr_ref, v_ref, o_ref, s_sc, m_sc, acc_sc, *, tq, hps):
    qi = pl.program_id(2)
    qw = 2 * LANES
    tk = tq // 2

    def scores_to(slot, j, row0=0):
        k0 = pl.multiple_of(j * tk, tk)
        kr = kr_ref[pl.ds(k0, tk), :]
        for hh in range(hps):
            k = jnp.concatenate([kn_ref[pl.ds(k0, tk), hh * LANES:(hh + 1) * LANES], kr], axis=1)
            s_sc[slot, hh, row0:, :] = _dot_nt(q_ref[row0:, hh * qw:(hh + 1) * qw], k)

    def update_from(slot, j, row0=0, masked=False):
        k0 = pl.multiple_of(j * tk, tk)
        for hh in range(hps):
            s = s_sc[slot, hh, row0:, :]
            if masked:
                r = lax.broadcasted_iota(jnp.int32, s.shape, 0)
                c = lax.broadcasted_iota(jnp.int32, s.shape, 1)
                s = jnp.where(c <= r, s, NEG_INF)
            m = m_sc[hh, row0:, :]
            m_new = jnp.maximum(m, jnp.max(s, axis=-1, keepdims=True))
            p = jnp.exp2(s - m_new)
            m_sc[hh, row0:, :] = m_new
            acc_sc[hh, row0:, :] = jnp.exp2(m - m_new) * acc_sc[hh, row0:, :] + _dot(
                p.astype(BF16), v_ref[pl.ds(k0, tk), hh * qw:(hh + 1) * qw])

    m_sc[...] = jnp.full(m_sc.shape, NEG_INF, F32)
    acc_sc[...] = jnp.zeros(acc_sc.shape, F32)
    scores_to(0, 0)

    def pair(p, carry):
        j = 2 * p
        scores_to(1, j + 1)
        update_from(0, j)
        scores_to(0, j + 2)
        update_from(1, j + 1)
        return carry

    lax.fori_loop(0, qi, pair, 0)
    scores_to(1, 2 * qi + 1, row0=tk)
    update_from(0, 2 * qi, masked=True)
    update_from(1, 2 * qi + 1, row0=tk, masked=True)

    for hh in range(hps):
        acc = acc_sc[hh]
        o_ref[:, hh * MLA_V_DIM:(hh + 1) * MLA_V_DIM] = (acc[:, :MLA_V_DIM] / acc[:, MLA_V_DIM:]).astype(BF16)


def _mla_attn(qb, kn, krd, vb, batch, seq, tq, hps):
    t = batch * seq
    nq = seq // tq
    assert tq % 2 == 0 and seq % tq == 0
    return pl.pallas_call(
        functools.partial(_mla_kernel, tq=tq, hps=hps),
        grid=(batch, MLA_HEADS // hps, nq),
        in_specs=[pl.BlockSpec((tq, 2 * LANES * hps), lambda b, h, i: (b * nq + i, h)),
                  pl.BlockSpec((seq, MLA_NOPE_DIM * hps), lambda b, h, i: (b, h)),
                  pl.BlockSpec((seq, LANES), lambda b, h, i: (b, 0)),
                  pl.BlockSpec((seq, 2 * LANES * hps), lambda b, h, i: (b, h))],
        out_specs=pl.BlockSpec((tq, MLA_V_DIM * hps), lambda b, h, i: (b * nq + i, h)),
        out_shape=jax.ShapeDtypeStruct((t, MLA_HEADS * MLA_V_DIM), BF16),
        scratch_shapes=[pltpu.VMEM((2, hps, tq, tq // 2), F32), pltpu.VMEM((hps, tq, 1), F32),
                        pltpu.VMEM((hps, tq, 2 * LANES), F32)],
        compiler_params=pltpu.CompilerParams(dimension_semantics=("arbitrary",) * 3,
                                             vmem_limit_bytes=VMEM_LIMIT_BYTES),
        name="mla_attn",
    )(qb, kn, krd, vb)


def _mem_attn_kernel(q_ref, k_ref, v_ref, o_ref):
    for h in range(MEM_HEADS):
        sl = slice(h * MEM_HEAD_DIM, (h + 1) * MEM_HEAD_DIM)
        s = _dot_nt(q_ref[:, sl], k_ref[:, sl])
        e = jnp.exp2(s - jnp.max(s, axis=-1, keepdims=True))
        o = _dot(e.astype(BF16), v_ref[:, 2 * h * LANES:(2 * h + 2) * LANES])
        o_ref[:, sl] = (o[:, :MEM_HEAD_DIM] / o[:, MEM_HEAD_DIM:]).astype(BF16)


def _mem_attn(qm, km, vm, batch, seq, m_len, tq):
    t = batch * seq
    nq = seq // tq
    hd = MEM_HEADS * MEM_HEAD_DIM
    return pl.pallas_call(
        _mem_attn_kernel,
        grid=(batch, nq),
        in_specs=[pl.BlockSpec((tq, hd), lambda b, i: (b * nq + i, 0)),
                  pl.BlockSpec((m_len, hd), lambda b, i: (b, 0)),
                  pl.BlockSpec((m_len, 2 * hd), lambda b, i: (b, 0))],
        out_specs=pl.BlockSpec((tq, hd), lambda b, i: (b * nq + i, 0)),
        out_shape=jax.ShapeDtypeStruct((t, hd), BF16),
        compiler_params=pltpu.CompilerParams(dimension_semantics=("arbitrary", "arbitrary"),
                                             vmem_limit_bytes=VMEM_LIMIT_BYTES),
        name="mem_attn",
    )(qm, km, vm)


def _out_proj_kernel(x_ref, ya_ref, yb_ref, ym_ref, w_ref, h_ref):
    na = ya_ref.shape[1]
    nb = na + yb_ref.shape[1]
    h_ref[...] = (x_ref[...] + _dot(ya_ref[...], w_ref[0:na, :]) + _dot(yb_ref[...], w_ref[na:nb, :])
                  + _dot(ym_ref[...], w_ref[nb:, :]))


def _out_proj(x2d, ya, yb, ym, w_out, tm):
    t, d = x2d.shape
    rows = lambda w: pl.BlockSpec((tm, w), lambda i: (i, 0))
    return pl.pallas_call(
        _out_proj_kernel,
        grid=(t // tm,),
        in_specs=[rows(d), rows(ya.shape[1]), rows(yb.shape[1]), rows(ym.shape[1]),
                  _const_spec(w_out.shape)],
        out_specs=rows(d),
        out_shape=jax.ShapeDtypeStruct((t, d), F32),
        compiler_params=pltpu.CompilerParams(dimension_semantics=("arbitrary",),
                                             vmem_limit_bytes=VMEM_LIMIT_BYTES),
        name="out_proj",
    )(x2d, ya, yb, ym, w_out)


def _ffn_kernel(h_ref, g_ref, wg_ref, wu_ref, wd_ref, o_ref, fn_ref):
    j = pl.program_id(1)

    @pl.when(j == 0)
    def _():
        h = h_ref[...]
        fn_ref[...] = _rms(h, g_ref[...]).astype(BF16)
        o_ref[...] = h

    fn = fn_ref[...]
    gate = _dot(fn, wg_ref[...])
    up = _dot(fn, wu_ref[...])
    act = (gate * jax.nn.sigmoid(gate) * up).astype(BF16)
    o_ref[...] += _dot(act, wd_ref[...])


def _ffn(h, g, wg, wu, wd, tm, tf):
    t, d = h.shape
    dff = wg.shape[1]
    return pl.pallas_call(
        _ffn_kernel,
        grid=(t // tm, dff // tf),
        in_specs=[pl.BlockSpec((tm, d), lambda i, j: (i, 0)),
                  _const_spec((1, d)),
                  pl.BlockSpec((d, tf), lambda i, j: (0, j)),
                  pl.BlockSpec((d, tf), lambda i, j: (0, j)),
                  pl.BlockSpec((tf, d), lambda i, j: (j, 0))],
        out_specs=pl.BlockSpec((tm, d), lambda i, j: (i, 0)),
        out_shape=jax.ShapeDtypeStruct((t, d), F32),
        scratch_shapes=[pltpu.VMEM((tm, d), BF16)],
        compiler_params=pltpu.CompilerParams(dimension_semantics=("arbitrary", "arbitrary"),
                                             vmem_limit_bytes=VMEM_LIMIT_BYTES),
        name="ffn",
    )(h, g, wg, wu, wd)


def _tile_row(v, width):
    v = v.astype(F32).reshape(-1)
    return jnp.tile(v, width // v.shape[0])


def _layer(h, mem2d, pos_col, pos_row, batch, seq, m_len, w_in_stack, layer, p):
    (attn_norm_g, swa_q_g, swa_k_g, swa_sinks, cq_g, ckv_g, w_uq, w_ukv, qn_g, qr_g, kn_g, kr_g,
     mem_g, w_mem_kv, memq_g, memk_g, w_out, ffn_g, w_gate, w_up, w_down) = p
    width = 2 * LANES
    inv_freq = ROPE_THETA ** (-jnp.arange(0, MLA_ROPE_DIM, 2, dtype=F32) / MLA_ROPE_DIM)
    sign = jnp.concatenate([-jnp.ones((MLA_ROPE_DIM // 2,), F32), jnp.ones((MLA_ROPE_DIM // 2,), F32)])
    rows = [swa_q_g, swa_k_g, qn_g, qr_g, kn_g, kr_g, memq_g, inv_freq, sign]
    sp = jnp.stack([_tile_row(r, width) for r in rows]
                   + [jnp.zeros((width,), F32)] * (_SP_ROWS - len(rows)))

    qd = MLA_NOPE_DIM + MLA_ROPE_DIM
    wq3 = w_uq.reshape(w_uq.shape[0], MLA_HEADS, qd)
    wq = jnp.concatenate([wq3, wq3[:, :, MLA_NOPE_DIM:]], axis=2).reshape(w_uq.shape[0], -1).astype(BF16)
    wkv3 = w_ukv.reshape(w_ukv.shape[0], MLA_HEADS, MLA_NOPE_DIM + MLA_V_DIM)
    wkv = jnp.concatenate([wkv3[:, :, :MLA_NOPE_DIM].reshape(w_ukv.shape[0], -1),
                           wkv3[:, :, MLA_NOPE_DIM:].reshape(w_ukv.shape[0], -1)], axis=1).astype(BF16)
    idx = jnp.arange(MXU_DIM) // SWA_HEAD_DIM
    bd = (idx[:, None] == idx[None, :]).astype(BF16)
    slopes = 2.0 ** (-8.0 * jnp.arange(1, SWA_Q_HEADS + 1, dtype=F32) / SWA_Q_HEADS)
    slope_sink = jnp.stack([slopes, swa_sinks.astype(F32)]) * LOG2E

    km, vm, w1, w2 = _mem_kv(mem2d, mem_g.reshape(1, -1), w_mem_kv, memk_g.reshape(1, -1), w_in_stack,
                             layer, batch, m_len)
    qa, ka, va, krd, qm, qb, kn, vb, wg16, wu16, wd16, wo16 = _in_proj(
        h, pos_col, attn_norm_g.reshape(1, -1), sp, w1, w2, wq, wkv, cq_g.reshape(1, -1),
        ckv_g.reshape(1, -1), bd, (w_gate, w_up, w_down, w_out), tm=512)
    ya = _swa_attn(slope_sink, qa, ka, va, pos_col, pos_row, batch, seq, tq=512)
    yb = _mla_attn(qb, kn, krd, vb, batch, seq, tq=1024, hps=2)
    ym = _mem_attn(qm, km, vm, batch, seq, m_len, tq=1024)
    h = _out_proj(h, ya, yb, ym, wo16, tm=512)
    return _ffn(h, ffn_g.reshape(1, -1), wg16, wu16, wd16, tm=1024, tf=512)


def kernel(x, mem, positions, attn_norm_g, w_in, swa_q_norm_g, swa_k_norm_g, swa_sinks, mla_cq_norm_g,
           mla_ckv_norm_g, w_uq, w_ukv, mla_qn_norm_g, mla_qr_norm_g, mla_kn_norm_g, mla_kr_norm_g,
           mem_norm_g, w_mem_kv, mem_q_norm_g, mem_k_norm_g, w_out, ffn_norm_g, w_gate, w_up, w_down):
    batch, seq, d = x.shape
    m_len = mem.shape[1]
    stacked = (attn_norm_g, swa_q_norm_g, swa_k_norm_g, swa_sinks, mla_cq_norm_g, mla_ckv_norm_g,
               w_uq, w_ukv, mla_qn_norm_g, mla_qr_norm_g, mla_kn_norm_g, mla_kr_norm_g, mem_norm_g,
               w_mem_kv, mem_q_norm_g, mem_k_norm_g, w_out, ffn_norm_g, w_gate, w_up, w_down)
    h = x.reshape(batch * seq, d)
    mem2d = mem.reshape(batch * m_len, d)
    pos_col = positions.reshape(batch * seq, 1)
    pos_row = positions.reshape(batch, 1, seq)
    for l in range(attn_norm_g.shape[0]):
        h = _layer(h, mem2d, pos_col, pos_row, batch, seq, m_len, w_in, l, tuple(a[l] for a in stacked))
    return h.reshape(batch, seq, d)
```

```python
import functools

import jax
import jax.numpy as jnp
from jax import lax
from jax.experimental import pallas as pl
from jax.experimental.pallas import tpu as pltpu

EPS = 1e-6
NEG_INF = -1e30
LOG2E = 1.4426950408889634
BLOCK = 128
WINDOW = 128

SWA_Q_HEADS = 16
SWA_KV_HEADS = 2
SWA_HEAD_DIM = 64

MLA_HEADS = 4
MLA_NOPE_DIM = 128
MLA_ROPE_DIM = 64
MLA_V_DIM = 128
ROPE_THETA = 10000.0

MEM_HEADS = 4
MEM_HEAD_DIM = 128

LANES = 128
MXU_DIM = 256
VMEM_LIMIT_BYTES = 56 * 1024 * 1024

F32 = jnp.float32
BF16 = jnp.bfloat16

_SP_SWA_Q, _SP_SWA_K, _SP_QN, _SP_QR, _SP_KN, _SP_KR, _SP_MEMQ, _SP_FREQ, _SP_SIGN = range(9)
_SP_ROWS = 16


def _dot(a, b):
    return jnp.dot(a, b, preferred_element_type=F32)


def _dot_nt(a, b):
    return lax.dot_general(a, b, (((1,), (1,)), ((), ())), preferred_element_type=F32)


def _rms(x, g):
    return x * lax.rsqrt(jnp.mean(x * x, axis=-1, keepdims=True) + EPS) * g


def _const_spec(shape):
    nd = len(shape)
    return pl.BlockSpec(shape, lambda *_: (0,) * nd, pipeline_mode=pl.Buffered(1))


def _mem_kv_kernel(mem_ref, g_ref, w32_ref, gk_ref, win_ref, k_ref, v_ref, w1_ref, w2_ref, w_ref):
    @pl.when(pl.program_id(0) == 0)
    def _():
        w_ref[...] = w32_ref[...].astype(BF16)

    w1_ref[...] = win_ref[:, :_W1_COLS].astype(BF16)
    half = MLA_ROPE_DIM
    lo = lax.broadcasted_iota(jnp.int32, (1, LANES), 1) < half
    n_full = (win_ref.shape[1] - _W1_COLS) // LANES
    tail = [win_ref[:, _W1_COLS + k * LANES:_W1_COLS + (k + 1) * LANES] for k in range(n_full)]
    last = win_ref[:, _W1_COLS + n_full * LANES:]
    tail.append(jnp.concatenate([last, last], axis=1))
    swapped = [pltpu.roll(c, half, 1) for c in tail[:n_full]]
    w2_ref[:, :LANES] = jnp.where(lo, tail[0], swapped[0]).astype(BF16)
    for k in range(1, n_full + 1):
        nxt = swapped[k] if k < n_full else tail[k]
        w2_ref[:, k * LANES:(k + 1) * LANES] = jnp.where(lo, swapped[k - 1], nxt).astype(BF16)

    xn = _rms(mem_ref[...], g_ref[...]).astype(BF16)
    kv = _dot(xn, w_ref[...])
    hd = MEM_HEADS * MEM_HEAD_DIM
    for h in range(MEM_HEADS):
        sl = slice(h * MEM_HEAD_DIM, (h + 1) * MEM_HEAD_DIM)
        k_ref[:, sl] = _rms(kv[:, sl], gk_ref[...]).astype(BF16)
        v_ref[:, 2 * h * LANES:(2 * h + 1) * LANES] = kv[:, hd + h * MEM_HEAD_DIM:
                                                         hd + (h + 1) * MEM_HEAD_DIM].astype(BF16)
        v_ref[:, (2 * h + 1) * LANES:(2 * h + 2) * LANES] = jnp.ones((kv.shape[0], LANES), BF16)


def _mem_kv(mem2d, g, w, gk, w_in_stack, layer, batch, m_len):
    d = mem2d.shape[1]
    hd = MEM_HEADS * MEM_HEAD_DIM
    _, rows, cols = w_in_stack.shape
    assert rows % (16 * batch) == 0 and (cols - _W1_COLS) % LANES == MLA_ROPE_DIM
    slab = rows // batch
    w2_cols = cols - _W1_COLS + MLA_ROPE_DIM
    return pl.pallas_call(
        _mem_kv_kernel,
        grid=(batch,),
        in_specs=[pl.BlockSpec((m_len, d), lambda b: (b, 0)),
                  _const_spec((1, d)), _const_spec((d, 2 * hd)), _const_spec((1, MEM_HEAD_DIM)),
                  pl.BlockSpec((None, slab, cols), lambda b: (layer, b, 0))],
        out_specs=[pl.BlockSpec((m_len, hd), lambda b: (b, 0)), pl.BlockSpec((m_len, 2 * hd), lambda b: (b, 0)),
                   pl.BlockSpec((slab, _W1_COLS), lambda b: (b, 0)), pl.BlockSpec((slab, w2_cols), lambda b: (b, 0))],
        out_shape=[jax.ShapeDtypeStruct((batch * m_len, hd), BF16),
                   jax.ShapeDtypeStruct((batch * m_len, 2 * hd), BF16),
                   jax.ShapeDtypeStruct((rows, _W1_COLS), BF16), jax.ShapeDtypeStruct((rows, w2_cols), BF16)],
        scratch_shapes=[pltpu.VMEM((d, 2 * hd), BF16)],
        compiler_params=pltpu.CompilerParams(dimension_semantics=("arbitrary",),
                                             vmem_limit_bytes=VMEM_LIMIT_BYTES),
        name="mem_kv",
    )(mem2d, g, w, gk, w_in_stack)


_C_QA = (0, 1024)
_C_KVA = (1024, 1280)
_C_CQ = (1280, 1792)
_C_CKV = (1792, 2304)
_W1_COLS = 2304
_C_KR = (2304, 2432)
_C_QM = (2432, 2944)


def _in_proj_kernel(x_ref, pos_ref, gattn_ref, sp_ref, w1_ref, w2_ref, wq_ref, wkv_ref, gcq_ref, gckv_ref,
                    bd_ref, wg32_ref, wu32_ref, wd32_ref, wo32_ref,
                    qa_ref, ka_ref, va_ref, krd_ref, qm_ref, qb_ref, kn_ref, vb_ref,
                    wg16_ref, wu16_ref, wd16_ref, wo16_ref):
    wg16_ref[...] = wg32_ref[...].astype(BF16)
    wu16_ref[...] = wu32_ref[...].astype(BF16)
    wd16_ref[...] = wd32_ref[...].astype(BF16)
    wo16_ref[...] = wo32_ref[...].astype(BF16)
    sp = sp_ref[...]
    row = lambda r, n=LANES: sp[r:r + 1, :n]
    hn = _rms(x_ref[...], gattn_ref[...]).astype(BF16)

    def proj(cols):
        if cols[0] >= _W1_COLS:
            return _dot(hn, w2_ref[:, cols[0] - _W1_COLS:cols[1] - _W1_COLS])
        return _dot(hn, w1_ref[:, cols[0]:cols[1]])

    def head64_rms(blk, g):
        ss = _dot((blk * blk).astype(BF16), bd_ref[...])
        return blk * lax.rsqrt(ss * (1.0 / SWA_HEAD_DIM) + EPS) * g

    ang = pos_ref[...].astype(F32) * row(_SP_FREQ)
    cos = jnp.cos(ang)
    sin_signed = jnp.sin(ang) * row(_SP_SIGN)

    def rope_dup(y):
        return y * cos + pltpu.roll(y, MLA_ROPE_DIM // 2, 1) * sin_signed

    cq = proj(_C_CQ)
    ckv = proj(_C_CKV)
    qm = proj(_C_QM)
    kva = proj(_C_KVA)
    kr = proj(_C_KR)
    cqn = _rms(cq, gcq_ref[...]).astype(BF16)
    ckvn = _rms(ckv, gckv_ref[...]).astype(BF16)

    for h in range(MEM_HEADS):
        sl = slice(h * MEM_HEAD_DIM, (h + 1) * MEM_HEAD_DIM)
        qm_ref[:, sl] = (_rms(qm[:, sl], row(_SP_MEMQ)) * (LOG2E * MEM_HEAD_DIM ** -0.5)).astype(BF16)
    lo = lax.broadcasted_iota(jnp.int32, (1, LANES), 1) < SWA_HEAD_DIM
    for src, dst in ((head64_rms(kva, row(_SP_SWA_K, MXU_DIM))[:, :LANES], ka_ref), (kva[:, LANES:], va_ref)):
        swapped = pltpu.roll(src, SWA_HEAD_DIM, 1)
        dst[:, :LANES] = jnp.where(lo, src, swapped).astype(BF16)
        dst[:, LANES:] = jnp.where(lo, swapped, src).astype(BF16)
    krd_ref[...] = rope_dup(_rms(kr, row(_SP_KR))).astype(BF16)

    qb = _dot(cqn, wq_ref[...])
    kvb = _dot(ckvn, wkv_ref[...])
    qa = proj(_C_QA)

    scale = LOG2E * (MLA_NOPE_DIM + MLA_ROPE_DIM) ** -0.5
    for h in range(MLA_HEADS):
        o = h * 2 * LANES
        qb_ref[:, o:o + LANES] = (_rms(qb[:, o:o + LANES], row(_SP_QN)) * scale).astype(BF16)
        qr = rope_dup(_rms(qb[:, o + LANES:o + 2 * LANES], row(_SP_QR)))
        qb_ref[:, o + LANES:o + 2 * LANES] = (qr * (0.5 * scale)).astype(BF16)
    hn_cols = MLA_HEADS * MLA_NOPE_DIM
    for h in range(MLA_HEADS):
        sl = slice(h * MLA_NOPE_DIM, (h + 1) * MLA_NOPE_DIM)
        kn_ref[:, sl] = _rms(kvb[:, sl], row(_SP_KN)).astype(BF16)
        vb_ref[:, 2 * h * LANES:(2 * h + 1) * LANES] = kvb[:, hn_cols + h * MLA_V_DIM:
                                                          hn_cols + (h + 1) * MLA_V_DIM].astype(BF16)
        vb_ref[:, (2 * h + 1) * LANES:(2 * h + 2) * LANES] = jnp.ones((kvb.shape[0], LANES), BF16)

    for c in range(_C_QA[1] // MXU_DIM):
        sl = slice(c * MXU_DIM, (c + 1) * MXU_DIM)
        qa_ref[:, sl] = (head64_rms(qa[:, sl], row(_SP_SWA_Q, MXU_DIM))
                         * (LOG2E * SWA_HEAD_DIM ** -0.5)).astype(BF16)


def _in_proj(x2d, pos_col, gattn, sp, w1, w2, wq, wkv, gcq, gckv, bd, late_w, tm):
    t, d = x2d.shape
    steps = t // tm
    widths = (1024, 256, 256, 128, 512, 1024, 512, 1024)
    rows = lambda w: pl.BlockSpec((tm, w), lambda i: (i, 0))
    slabs = []
    for w in late_w:
        assert w.shape[0] % (16 * steps) == 0, (w.shape, steps)
        slabs.append(pl.BlockSpec((w.shape[0] // steps, w.shape[1]), lambda i: (i, 0)))
    return pl.pallas_call(
        _in_proj_kernel,
        grid=(steps,),
        in_specs=[rows(d), rows(1), _const_spec(gattn.shape), _const_spec(sp.shape),
                  _const_spec(w1.shape), _const_spec(w2.shape), _const_spec(wq.shape),
                  _const_spec(wkv.shape),
                  _const_spec(gcq.shape), _const_spec(gckv.shape), _const_spec(bd.shape)] + slabs,
        out_specs=[rows(w) for w in widths] + slabs,
        out_shape=[jax.ShapeDtypeStruct((t, w), BF16) for w in widths]
        + [jax.ShapeDtypeStruct(w.shape, BF16) for w in late_w],
        compiler_params=pltpu.CompilerParams(dimension_semantics=("arbitrary",),
                                             vmem_limit_bytes=VMEM_LIMIT_BYTES),
        name="in_proj",
    )(x2d, pos_col, gattn, sp, w1, w2, wq, wkv, gcq, gckv, bd, *late_w)


_SWA_MASK_DIST = -NEG_INF * 2.0 ** 8


def _swa_kernel(ss_ref, q_ref, kc_ref, kp_ref, vc_ref, vp_ref, pq_ref, pkc_ref, pkp_ref, o_ref, *, tq):
    i = pl.program_id(1)
    g = SWA_Q_HEADS // SWA_KV_HEADS
    kk = lax.broadcasted_iota(jnp.int32, (BLOCK, BLOCK), 0)
    qq = lax.broadcasted_iota(jnp.int32, (BLOCK, BLOCK), 1)
    from_prev = kk > qq
    prev_w = jnp.where(from_prev, 1.0, 0.0).astype(BF16)
    cur_w = jnp.where(from_prev, 0.0, 1.0).astype(BF16)
    lane_lo = lax.broadcasted_iota(jnp.int32, (BLOCK, LANES), 1) < SWA_HEAD_DIM
    lo_w = jnp.where(lane_lo, 1.0, 0.0).astype(BF16)
    hi_w = jnp.where(lane_lo, 0.0, 1.0).astype(BF16)

    for j in range(tq // BLOCK):
        rows = slice(j * BLOCK, (j + 1) * BLOCK)
        prows = slice((j - 1) * BLOCK, j * BLOCK)
        k_prev, v_prev, pk_prev = ((kp_ref[...], vp_ref[...], pkp_ref[...]) if j == 0 else
                                   (kc_ref[prows, :], vc_ref[prows, :], pkc_ref[prows, :]))
        kj = jnp.concatenate([k_prev, kc_ref[rows, :]], axis=0)
        vj = jnp.concatenate([v_prev, vc_ref[rows, :]], axis=0)
        pq = pq_ref[:, rows]
        dist = jnp.where(from_prev, jnp.abs(pk_prev - pq), jnp.abs(pkc_ref[rows, :] - pq)).astype(F32)
        if j == 0:
            dist = jnp.where(from_prev, jnp.maximum(dist, jnp.where(i == 0, _SWA_MASK_DIST, 0.0)), dist)
        for kv in range(SWA_KV_HEADS):
            kvl = slice(kv * LANES, (kv + 1) * LANES)
            qm = jnp.concatenate(
                [q_ref[rows, (kv * g + hh) // 2 * LANES:((kv * g + hh) // 2 + 1) * LANES]
                 * (lo_w if hh % 2 == 0 else hi_w) for hh in range(g)], axis=0)
            st = _dot_nt(kj[:, kvl], qm)
            pts = []
            for hh in range(g):
                head = kv * g + hh
                cols = slice(hh * BLOCK, (hh + 1) * BLOCK)
                t = jnp.where(from_prev, st[:BLOCK, cols], st[BLOCK:, cols]) - ss_ref[0, head] * dist
                sink = ss_ref[1, head]
                m = jnp.maximum(jnp.max(t, axis=0, keepdims=True), sink)
                e = jnp.exp2(t - m)
                den = jnp.sum(e, axis=0, keepdims=True) + jnp.exp2(sink - m)
                p = (e * (1.0 / den)).astype(BF16)
                pts.append(jnp.concatenate([p * prev_w, p * cur_w], axis=0))
            pt = jnp.concatenate(pts, axis=1)
            o = lax.dot_general(pt, vj[:, kvl], (((0,), (0,)), ((), ())), preferred_element_type=F32)
            for pr in range(g // 2):
                grp = kv * (g // 2) + pr
                even = o[(2 * pr) * BLOCK:(2 * pr + 1) * BLOCK]
                odd = o[(2 * pr + 1) * BLOCK:(2 * pr + 2) * BLOCK]
                o_ref[rows, grp * LANES:(grp + 1) * LANES] = jnp.where(lane_lo, even, odd).astype(BF16)


def _swa_attn(slope_sink, qa, ka, va, pos_col, pos_row, batch, seq, tq):
    t = batch * seq
    nq = seq // tq
    nb = tq // BLOCK
    cur = lambda w: pl.BlockSpec((tq, w), lambda b, i: (b * nq + i, 0))
    prev = lambda w: pl.BlockSpec(
        (BLOCK, w), lambda b, i: (b * nq * nb + jnp.maximum(i * nb - 1, 0), 0))
    return pl.pallas_call(
        functools.partial(_swa_kernel, tq=tq),
        grid=(batch, nq),
        in_specs=[pl.BlockSpec(memory_space=pltpu.SMEM),
                  cur(SWA_Q_HEADS * SWA_HEAD_DIM), cur(2 * LANES), prev(2 * LANES), cur(2 * LANES),
                  prev(2 * LANES),
                  pl.BlockSpec((None, 1, tq), lambda b, i: (b, 0, i)),
                  cur(1), prev(1)],
        out_specs=cur(SWA_Q_HEADS * SWA_HEAD_DIM),
        out_shape=jax.ShapeDtypeStruct((t, SWA_Q_HEADS * SWA_HEAD_DIM), BF16),
        compiler_params=pltpu.CompilerParams(dimension_semantics=("arbitrary", "arbitrary"),
                                             vmem_limit_bytes=VMEM_LIMIT_BYTES),
        name="swa_attn",
    )(slope_sink, qa, ka, ka, va, va, pos_row, pos_col, pos_col)


def _swa_out_kernel(ss_ref, q_ref, kc_ref, kp_ref, vc_ref, vp_ref, pq_ref, pkc_ref, pkp_ref,
                    x_ref, yb_ref, qm_ref, km_ref, vm_ref, w_ref, h_ref, ya_sc, ym_sc, *, tq):
    na = ya_sc.shape[1]
    nb = na + yb_ref.shape[1]
    _mem_attn_kernel(qm_ref, km_ref, vm_ref, ym_sc)
    h_ref[...] = x_ref[...] + _dot(yb_ref[...], w_ref[na:nb, :]) + _dot(ym_sc[...], w_ref[nb:, :])
    _swa_kernel(ss_ref, q_ref, kc_ref, kp_ref, vc_ref, vp_ref, pq_ref, pkc_ref, pkp_ref, ya_sc, tq=tq)
    h_ref[...] += _dot(ya_sc[...], w_ref[0:na, :])


def _swa_out(slope_sink, qa, ka, va, pos_col, pos_row, x2d, yb, qm, km, vm, w_out, batch, seq, m_len, tq):
    t, d = x2d.shape
    nq = seq // tq
    nb = tq // BLOCK
    na = SWA_Q_HEADS * SWA_HEAD_DIM
    hd = MEM_HEADS * MEM_HEAD_DIM
    cur = lambda w: pl.BlockSpec((tq, w), lambda b, i: (b * nq + i, 0))
    prev = lambda w: pl.BlockSpec(
        (BLOCK, w), lambda b, i: (b * nq * nb + jnp.maximum(i * nb - 1, 0), 0))
    return pl.pallas_call(
        functools.partial(_swa_out_kernel, tq=tq),
        grid=(batch, nq),
        in_specs=[pl.BlockSpec(memory_space=pltpu.SMEM),
                  cur(na), cur(2 * LANES), prev(2 * LANES), cur(2 * LANES), prev(2 * LANES),
                  pl.BlockSpec((None, 1, tq), lambda b, i: (b, 0, i)),
                  cur(1), prev(1),
                  cur(d), cur(yb.shape[1]), cur(hd),
                  pl.BlockSpec((m_len, hd), lambda b, i: (b, 0)),
                  pl.BlockSpec((m_len, 2 * hd), lambda b, i: (b, 0)),
                  _const_spec(w_out.shape)],
        out_specs=cur(d),
        out_shape=jax.ShapeDtypeStruct((t, d), F32),
        scratch_shapes=[pltpu.VMEM((tq, na), BF16), pltpu.VMEM((tq, hd), BF16)],
        compiler_params=pltpu.CompilerParams(dimension_semantics=("arbitrary", "arbitrary"),
                                             vmem_limit_bytes=VMEM_LIMIT_BYTES),
        name="swa_out",
    )(slope_sink, qa, ka, ka, va, va, pos_row, pos_col, pos_col, x2d, yb, qm, km, vm, w_out)


def _mla_kernel(q_ref, kn_ref, kr_ref, v_ref, o_ref, s_sc, m_sc, acc_sc, *, tq, hps):
    qi = pl.program_id(2)
    qw = 2 * LANES
    tk = tq // 2

    def scores_to(slot, j, row0=0):
        k0 = pl.multiple_of(j * tk, tk)
        kr = kr_ref[pl.ds(k0, tk), :]
        for hh in range(hps):
            k = jnp.concatenate([kn_ref[pl.ds(k0, tk), hh * LANES:(hh + 1) * LANES], kr], axis=1)
            s_sc[slot, hh, row0:, :] = _dot_nt(q_ref[row0:, hh * qw:(hh + 1) * qw], k)

    def update_from(slot, j, row0=0, masked=False):
        k0 = pl.multiple_of(j * tk, tk)
        for hh in range(hps):
            s = s_sc[slot, hh, row0:, :]
            if masked:
                r = lax.broadcasted_iota(jnp.int32, s.shape, 0)
                c = lax.broadcasted_iota(jnp.int32, s.shape, 1)
                s = jnp.where(c <= r, s, NEG_INF)
            m = m_sc[hh, row0:, :]
            m_new = jnp.maximum(m, jnp.max(s, axis=-1, keepdims=True))
            p = jnp.exp2(s - m_new)
            m_sc[hh, row0:, :] = m_new
            acc_sc[hh, row0:, :] = jnp.exp2(m - m_new) * acc_sc[hh, row0:, :] + _dot(
                p.astype(BF16), v_ref[pl.ds(k0, tk), hh * qw:(hh + 1) * qw])

    m_sc[...] = jnp.full(m_sc.shape, NEG_INF, F32)
    acc_sc[...] = jnp.zeros(acc_sc.shape, F32)
    scores_to(0, 0)

    def pair(p, carry):
        j = 2 * p
        scores_to(1, j + 1)
        update_from(0, j)
        scores_to(0, j + 2)
        update_from(1, j + 1)
        return carry

    lax.fori_loop(0, qi, pair, 0)
    scores_to(1, 2 * qi + 1, row0=tk)
    update_from(0, 2 * qi, masked=True)
    update_from(1, 2 * qi + 1, row0=tk, masked=True)

    for hh in range(hps):
        acc = acc_sc[hh]
        o_ref[:, hh * MLA_V_DIM:(hh + 1) * MLA_V_DIM] = (acc[:, :MLA_V_DIM] / acc[:, MLA_V_DIM:]).astype(BF16)


def _mla_attn(qb, kn, krd, vb, batch, seq, tq, hps):
    t = batch * seq
    nq = seq // tq
    assert tq % 2 == 0 and seq % tq == 0
    return pl.pallas_call(
        functools.partial(_mla_kernel, tq=tq, hps=hps),
        grid=(batch, MLA_HEADS // hps, nq),
        in_specs=[pl.BlockSpec((tq, 2 * LANES * hps), lambda b, h, i: (b * nq + i, h)),
                  pl.BlockSpec((seq, MLA_NOPE_DIM * hps), lambda b, h, i: (b, h)),
                  pl.BlockSpec((seq, LANES), lambda b, h, i: (b, 0)),
                  pl.BlockSpec((seq, 2 * LANES * hps), lambda b, h, i: (b, h))],
        out_specs=pl.BlockSpec((tq, MLA_V_DIM * hps), lambda b, h, i: (b * nq + i, h)),
        out_shape=jax.ShapeDtypeStruct((t, MLA_HEADS * MLA_V_DIM), BF16),
        scratch_shapes=[pltpu.VMEM((2, hps, tq, tq // 2), F32), pltpu.VMEM((hps, tq, 1), F32),
                        pltpu.VMEM((hps, tq, 2 * LANES), F32)],
        compiler_params=pltpu.CompilerParams(dimension_semantics=("arbitrary",) * 3,
                                             vmem_limit_bytes=VMEM_LIMIT_BYTES),
        name="mla_attn",
    )(qb, kn, krd, vb)


def _mem_attn_kernel(q_ref, k_ref, v_ref, o_ref):
    for h in range(MEM_HEADS):
        sl = slice(h * MEM_HEAD_DIM, (h + 1) * MEM_HEAD_DIM)
        s = _dot_nt(q_ref[:, sl], k_ref[:, sl])
        e = jnp.exp2(s - jnp.max(s, axis=-1, keepdims=True))
        o = _dot(e.astype(BF16), v_ref[:, 2 * h * LANES:(2 * h + 2) * LANES])
        o_ref[:, sl] = (o[:, :MEM_HEAD_DIM] / o[:, MEM_HEAD_DIM:]).astype(BF16)


def _mem_attn(qm, km, vm, batch, seq, m_len, tq):
    t = batch * seq
    nq = seq // tq
    hd = MEM_HEADS * MEM_HEAD_DIM
    return pl.pallas_call(
        _mem_attn_kernel,
        grid=(batch, nq),
        in_specs=[pl.BlockSpec((tq, hd), lambda b, i: (b * nq + i, 0)),
                  pl.BlockSpec((m_len, hd), lambda b, i: (b, 0)),
                  pl.BlockSpec((m_len, 2 * hd), lambda b, i: (b, 0))],
        out_specs=pl.BlockSpec((tq, hd), lambda b, i: (b * nq + i, 0)),
        out_shape=jax.ShapeDtypeStruct((t, hd), BF16),
        compiler_params=pltpu.CompilerParams(dimension_semantics=("arbitrary", "arbitrary"),
                                             vmem_limit_bytes=VMEM_LIMIT_BYTES),
        name="mem_attn",
    )(qm, km, vm)


def _out_proj_kernel(x_ref, ya_ref, yb_ref, ym_ref, w_ref, h_ref):
    na = ya_ref.shape[1]
    nb = na + yb_ref.shape[1]
    h_ref[...] = (x_ref[...] + _dot(ya_ref[...], w_ref[0:na, :]) + _dot(yb_ref[...], w_ref[na:nb, :])
                  + _dot(ym_ref[...], w_ref[nb:, :]))


def _out_proj(x2d, ya, yb, ym, w_out, tm):
    t, d = x2d.shape
    rows = lambda w: pl.BlockSpec((tm, w), lambda i: (i, 0))
    return pl.pallas_call(
        _out_proj_kernel,
        grid=(t // tm,),
        in_specs=[rows(d), rows(ya.shape[1]), rows(yb.shape[1]), rows(ym.shape[1]),
                  _const_spec(w_out.shape)],
        out_specs=rows(d),
        out_shape=jax.ShapeDtypeStruct((t, d), F32),
        compiler_params=pltpu.CompilerParams(dimension_semantics=("arbitrary",),
                                             vmem_limit_bytes=VMEM_LIMIT_BYTES),
        name="out_proj",
    )(x2d, ya, yb, ym, w_out)


def _ffn_kernel(h_ref, g_ref, wg_ref, wu_ref, wd_ref, o_ref, fn_ref):
    j = pl.program_id(1)

    @pl.when(j == 0)
    def _():
        h = h_ref[...]
        fn_ref[...] = _rms(h, g_ref[...]).astype(BF16)
        o_ref[...] = h

    fn = fn_ref[...]
    gate = _dot(fn, wg_ref[...])
    up = _dot(fn, wu_ref[...])
    act = (gate * jax.nn.sigmoid(gate) * up).astype(BF16)
    o_ref[...] += _dot(act, wd_ref[...])


def _ffn(h, g, wg, wu, wd, tm, tf):
    t, d = h.shape
    dff = wg.shape[1]
    return pl.pallas_call(
        _ffn_kernel,
        grid=(t // tm, dff // tf),
        in_specs=[pl.BlockSpec((tm, d), lambda i, j: (i, 0)),
                  _const_spec((1, d)),
                  pl.BlockSpec((d, tf), lambda i, j: (0, j)),
                  pl.BlockSpec((d, tf), lambda i, j: (0, j)),
                  pl.BlockSpec((tf, d), lambda i, j: (j, 0))],
        out_specs=pl.BlockSpec((tm, d), lambda i, j: (i, 0)),
        out_shape=jax.ShapeDtypeStruct((t, d), F32),
        scratch_shapes=[pltpu.VMEM((tm, d), BF16)],
        compiler_params=pltpu.CompilerParams(dimension_semantics=("arbitrary", "arbitrary"),
                                             vmem_limit_bytes=VMEM_LIMIT_BYTES),
        name="ffn",
    )(h, g, wg, wu, wd)


def _tile_row(v, width):
    v = v.astype(F32).reshape(-1)
    return jnp.tile(v, width // v.shape[0])


def _layer(h, mem2d, pos_col, pos_row, batch, seq, m_len, w_in_stack, layer, p):
    (attn_norm_g, swa_q_g, swa_k_g, swa_sinks, cq_g, ckv_g, w_uq, w_ukv, qn_g, qr_g, kn_g, kr_g,
     mem_g, w_mem_kv, memq_g, memk_g, w_out, ffn_g, w_gate, w_up, w_down) = p
    width = 2 * LANES
    inv_freq = ROPE_THETA ** (-jnp.arange(0, MLA_ROPE_DIM, 2, dtype=F32) / MLA_ROPE_DIM)
    sign = jnp.concatenate([-jnp.ones((MLA_ROPE_DIM // 2,), F32), jnp.ones((MLA_ROPE_DIM // 2,), F32)])
    rows = [swa_q_g, swa_k_g, qn_g, qr_g, kn_g, kr_g, memq_g, inv_freq, sign]
    sp = jnp.stack([_tile_row(r, width) for r in rows]
                   + [jnp.zeros((width,), F32)] * (_SP_ROWS - len(rows)))

    qd = MLA_NOPE_DIM + MLA_ROPE_DIM
    wq3 = w_uq.reshape(w_uq.shape[0], MLA_HEADS, qd)
    wq = jnp.concatenate([wq3, wq3[:, :, MLA_NOPE_DIM:]], axis=2).reshape(w_uq.shape[0], -1).astype(BF16)
    wkv3 = w_ukv.reshape(w_ukv.shape[0], MLA_HEADS, MLA_NOPE_DIM + MLA_V_DIM)
    wkv = jnp.concatenate([wkv3[:, :, :MLA_NOPE_DIM].reshape(w_ukv.shape[0], -1),
                           wkv3[:, :, MLA_NOPE_DIM:].reshape(w_ukv.shape[0], -1)], axis=1).astype(BF16)
    idx = jnp.arange(MXU_DIM) // SWA_HEAD_DIM
    bd = (idx[:, None] == idx[None, :]).astype(BF16)
    slopes = 2.0 ** (-8.0 * jnp.arange(1, SWA_Q_HEADS + 1, dtype=F32) / SWA_Q_HEADS)
    slope_sink = jnp.stack([slopes, swa_sinks.astype(F32)]) * LOG2E

    km, vm, w1, w2 = _mem_kv(mem2d, mem_g.reshape(1, -1), w_mem_kv, memk_g.reshape(1, -1), w_in_stack,
                             layer, batch, m_len)
    qa, ka, va, krd, qm, qb, kn, vb, wg16, wu16, wd16, wo16 = _in_proj(
        h, pos_col, attn_norm_g.reshape(1, -1), sp, w1, w2, wq, wkv, cq_g.reshape(1, -1),
        ckv_g.reshape(1, -1), bd, (w_gate, w_up, w_down, w_out), tm=512)
    yb = _mla_attn(qb, kn, krd, vb, batch, seq, tq=1024, hps=2)
    h = _swa_out(slope_sink, qa, ka, va, pos_col, pos_row, h, yb, qm, km, vm, wo16, batch, seq, m_len, tq=512)
    return _ffn(h, ffn_g.reshape(1, -1), wg16, wu16, wd16, tm=1024, tf=512)


def kernel(x, mem, positions, attn_norm_g, w_in, swa_q_norm_g, swa_k_norm_g, swa_sinks, mla_cq_norm_g,
           mla_ckv_norm_g, w_uq, w_ukv, mla_qn_norm_g, mla_qr_norm_g, mla_kn_norm_g, mla_kr_norm_g,
           mem_norm_g, w_mem_kv, mem_q_norm_g, mem_k_norm_g, w_out, ffn_norm_g, w_gate, w_up, w_down):
    batch, seq, d = x.shape
    m_len = mem.shape[1]
    stacked = (attn_norm_g, swa_q_norm_g, swa_k_norm_g, swa_sinks, mla_cq_norm_g, mla_ckv_norm_g,
               w_uq, w_ukv, mla_qn_norm_g, mla_qr_norm_g, mla_kn_norm_g, mla_kr_norm_g, mem_norm_g,
               w_mem_kv, mem_q_norm_g, mem_k_norm_g, w_out, ffn_norm_g, w_gate, w_up, w_down)
    h = x.reshape(batch * seq, d)
    mem2d = mem.reshape(batch * m_len, d)
    pos_col = positions.reshape(batch * seq, 1)
    pos_row = positions.reshape(batch, 1, seq)
    for l in range(attn_norm_g.shape[0]):
        h = _layer(h, mem2d, pos_col, pos_row, batch, seq, m_len, w_in, l, tuple(a[l] for a in stacked))
    return h.reshape(batch, seq, d)
```

```python
import functools

import jax
import jax.numpy as jnp
from jax import lax
from jax.experimental import pallas as pl
from jax.experimental.pallas import tpu as pltpu

EPS = 1e-6
NEG_INF = -1e30
LOG2E = 1.4426950408889634
BLOCK = 128

SWA_Q_HEADS = 16
SWA_KV_HEADS = 2
SWA_HEAD_DIM = 64

MLA_HEADS = 4
MLA_NOPE_DIM = 128
MLA_ROPE_DIM = 64
MLA_V_DIM = 128
ROPE_THETA = 10000.0

MEM_HEADS = 4
MEM_HEAD_DIM = 128

LANES = 128
MXU_DIM = 256
VMEM_LIMIT_BYTES = 56 * 1024 * 1024

IN_PROJ_ROWS = 512
ATTN_OUT_ROWS = 512
MLA_Q_ROWS = 1024
MLA_HEADS_PER_STEP = 2
FFN_ROWS = 1024
FFN_COLS = 512

F32 = jnp.float32
BF16 = jnp.bfloat16

_SP_SWA_Q, _SP_SWA_K, _SP_QN, _SP_QR, _SP_KN, _SP_KR, _SP_MEMQ, _SP_FREQ, _SP_SIGN = range(9)
_SP_ROWS = 16


def _dot(a, b):
    return jnp.dot(a, b, preferred_element_type=F32)


def _dot_nt(a, b):
    return lax.dot_general(a, b, (((1,), (1,)), ((), ())), preferred_element_type=F32)


def _rms(x, g):
    return x * lax.rsqrt(jnp.mean(x * x, axis=-1, keepdims=True) + EPS) * g


def _const_spec(shape):
    nd = len(shape)
    return pl.BlockSpec(shape, lambda *_: (0,) * nd, pipeline_mode=pl.Buffered(1))


def _mem_kv_kernel(mem_ref, g_ref, w32_ref, gk_ref, win_ref, k_ref, v_ref, w1_ref, w2_ref, w_ref):
    @pl.when(pl.program_id(0) == 0)
    def _():
        w_ref[...] = w32_ref[...].astype(BF16)

    w1_ref[...] = win_ref[:, :_W1_COLS].astype(BF16)
    half = MLA_ROPE_DIM
    lo = lax.broadcasted_iota(jnp.int32, (1, LANES), 1) < half
    n_full = (win_ref.shape[1] - _W1_COLS) // LANES
    tail = [win_ref[:, _W1_COLS + k * LANES:_W1_COLS + (k + 1) * LANES] for k in range(n_full)]
    last = win_ref[:, _W1_COLS + n_full * LANES:]
    tail.append(jnp.concatenate([last, last], axis=1))
    swapped = [pltpu.roll(c, half, 1) for c in tail[:n_full]]
    w2_ref[:, :LANES] = jnp.where(lo, tail[0], swapped[0]).astype(BF16)
    for k in range(1, n_full + 1):
        nxt = swapped[k] if k < n_full else tail[k]
        w2_ref[:, k * LANES:(k + 1) * LANES] = jnp.where(lo, swapped[k - 1], nxt).astype(BF16)

    xn = _rms(mem_ref[...], g_ref[...]).astype(BF16)
    kv = _dot(xn, w_ref[...])
    hd = MEM_HEADS * MEM_HEAD_DIM
    for h in range(MEM_HEADS):
        sl = slice(h * MEM_HEAD_DIM, (h + 1) * MEM_HEAD_DIM)
        k_ref[:, sl] = _rms(kv[:, sl], gk_ref[...]).astype(BF16)
        v_ref[:, 2 * h * LANES:(2 * h + 1) * LANES] = kv[:, hd + h * MEM_HEAD_DIM:
                                                         hd + (h + 1) * MEM_HEAD_DIM].astype(BF16)
        v_ref[:, (2 * h + 1) * LANES:(2 * h + 2) * LANES] = jnp.ones((kv.shape[0], LANES), BF16)


def _mem_kv(mem2d, g, w, gk, w_in_stack, layer, batch, m_len):
    d = mem2d.shape[1]
    hd = MEM_HEADS * MEM_HEAD_DIM
    _, rows, cols = w_in_stack.shape
    assert rows % (16 * batch) == 0 and (cols - _W1_COLS) % LANES == MLA_ROPE_DIM
    slab = rows // batch
    w2_cols = cols - _W1_COLS + MLA_ROPE_DIM
    return pl.pallas_call(
        _mem_kv_kernel,
        grid=(batch,),
        in_specs=[pl.BlockSpec((m_len, d), lambda b: (b, 0)),
                  _const_spec((1, d)), _const_spec((d, 2 * hd)), _const_spec((1, MEM_HEAD_DIM)),
                  pl.BlockSpec((None, slab, cols), lambda b: (layer, b, 0))],
        out_specs=[pl.BlockSpec((m_len, hd), lambda b: (b, 0)), pl.BlockSpec((m_len, 2 * hd), lambda b: (b, 0)),
                   pl.BlockSpec((slab, _W1_COLS), lambda b: (b, 0)), pl.BlockSpec((slab, w2_cols), lambda b: (b, 0))],
        out_shape=[jax.ShapeDtypeStruct((batch * m_len, hd), BF16),
                   jax.ShapeDtypeStruct((batch * m_len, 2 * hd), BF16),
                   jax.ShapeDtypeStruct((rows, _W1_COLS), BF16), jax.ShapeDtypeStruct((rows, w2_cols), BF16)],
        scratch_shapes=[pltpu.VMEM((d, 2 * hd), BF16)],
        compiler_params=pltpu.CompilerParams(dimension_semantics=("arbitrary",),
                                             vmem_limit_bytes=VMEM_LIMIT_BYTES),
        name="mem_kv",
    )(mem2d, g, w, gk, w_in_stack)


_C_QA = (0, 1024)
_C_KVA = (1024, 1280)
_C_CQ = (1280, 1792)
_C_CKV = (1792, 2304)
_W1_COLS = 2304
_C_KR = (2304, 2432)
_C_QM = (2432, 2944)


def _in_proj_kernel(x_ref, pos_ref, gattn_ref, sp_ref, w1_ref, w2_ref, wq_ref, wkv_ref, gcq_ref, gckv_ref,
                    bd_ref, wg32_ref, wu32_ref, wd32_ref, wo32_ref,
                    qa_ref, ka_ref, va_ref, krd_ref, qm_ref, qb_ref, kn_ref, vb_ref,
                    wg16_ref, wu16_ref, wd16_ref, wo16_ref):
    wg16_ref[...] = wg32_ref[...].astype(BF16)
    wu16_ref[...] = wu32_ref[...].astype(BF16)
    wd16_ref[...] = wd32_ref[...].astype(BF16)
    wo16_ref[...] = wo32_ref[...].astype(BF16)
    sp = sp_ref[...]
    row = lambda r, n=LANES: sp[r:r + 1, :n]
    hn = _rms(x_ref[...], gattn_ref[...]).astype(BF16)

    def proj(cols):
        if cols[0] >= _W1_COLS:
            return _dot(hn, w2_ref[:, cols[0] - _W1_COLS:cols[1] - _W1_COLS])
        return _dot(hn, w1_ref[:, cols[0]:cols[1]])

    def head64_rms(blk, g):
        ss = _dot((blk * blk).astype(BF16), bd_ref[...])
        return blk * lax.rsqrt(ss * (1.0 / SWA_HEAD_DIM) + EPS) * g

    ang = pos_ref[...].astype(F32) * row(_SP_FREQ)
    cos = jnp.cos(ang)
    sin_signed = jnp.sin(ang) * row(_SP_SIGN)

    def rope_dup(y):
        return y * cos + pltpu.roll(y, MLA_ROPE_DIM // 2, 1) * sin_signed

    cq = proj(_C_CQ)
    ckv = proj(_C_CKV)
    qm = proj(_C_QM)
    kva = proj(_C_KVA)
    kr = proj(_C_KR)
    cqn = _rms(cq, gcq_ref[...]).astype(BF16)
    ckvn = _rms(ckv, gckv_ref[...]).astype(BF16)

    for h in range(MEM_HEADS):
        sl = slice(h * MEM_HEAD_DIM, (h + 1) * MEM_HEAD_DIM)
        qm_ref[:, sl] = (_rms(qm[:, sl], row(_SP_MEMQ)) * (LOG2E * MEM_HEAD_DIM ** -0.5)).astype(BF16)
    lo = lax.broadcasted_iota(jnp.int32, (1, LANES), 1) < SWA_HEAD_DIM
    for src, dst in ((head64_rms(kva, row(_SP_SWA_K, MXU_DIM))[:, :LANES], ka_ref), (kva[:, LANES:], va_ref)):
        swapped = pltpu.roll(src, SWA_HEAD_DIM, 1)
        dst[:, :LANES] = jnp.where(lo, src, swapped).astype(BF16)
        dst[:, LANES:] = jnp.where(lo, swapped, src).astype(BF16)
    krd_ref[...] = rope_dup(_rms(kr, row(_SP_KR))).astype(BF16)

    qb = _dot(cqn, wq_ref[...])
    kvb = _dot(ckvn, wkv_ref[...])
    qa = proj(_C_QA)

    scale = LOG2E * (MLA_NOPE_DIM + MLA_ROPE_DIM) ** -0.5
    for h in range(MLA_HEADS):
        o = h * 2 * LANES
        qb_ref[:, o:o + LANES] = (_rms(qb[:, o:o + LANES], row(_SP_QN)) * scale).astype(BF16)
        qr = rope_dup(_rms(qb[:, o + LANES:o + 2 * LANES], row(_SP_QR)))
        qb_ref[:, o + LANES:o + 2 * LANES] = (qr * (0.5 * scale)).astype(BF16)
    hn_cols = MLA_HEADS * MLA_NOPE_DIM
    for h in range(MLA_HEADS):
        sl = slice(h * MLA_NOPE_DIM, (h + 1) * MLA_NOPE_DIM)
        kn_ref[:, sl] = _rms(kvb[:, sl], row(_SP_KN)).astype(BF16)
        vb_ref[:, 2 * h * LANES:(2 * h + 1) * LANES] = kvb[:, hn_cols + h * MLA_V_DIM:
                                                          hn_cols + (h + 1) * MLA_V_DIM].astype(BF16)
        vb_ref[:, (2 * h + 1) * LANES:(2 * h + 2) * LANES] = jnp.ones((kvb.shape[0], LANES), BF16)

    for c in range(_C_QA[1] // MXU_DIM):
        sl = slice(c * MXU_DIM, (c + 1) * MXU_DIM)
        qa_ref[:, sl] = (head64_rms(qa[:, sl], row(_SP_SWA_Q, MXU_DIM))
                         * (LOG2E * SWA_HEAD_DIM ** -0.5)).astype(BF16)


def _in_proj(x2d, pos_col, gattn, sp, w1, w2, wq, wkv, gcq, gckv, bd, late_w, tm):
    t, d = x2d.shape
    steps = t // tm
    widths = (1024, 256, 256, 128, 512, 1024, 512, 1024)
    rows = lambda w: pl.BlockSpec((tm, w), lambda i: (i, 0))
    slabs = []
    for w in late_w:
        assert w.shape[0] % (16 * steps) == 0, (w.shape, steps)
        slabs.append(pl.BlockSpec((w.shape[0] // steps, w.shape[1]), lambda i: (i, 0)))
    return pl.pallas_call(
        _in_proj_kernel,
        grid=(steps,),
        in_specs=[rows(d), rows(1), _const_spec(gattn.shape), _const_spec(sp.shape),
                  _const_spec(w1.shape), _const_spec(w2.shape), _const_spec(wq.shape),
                  _const_spec(wkv.shape),
                  _const_spec(gcq.shape), _const_spec(gckv.shape), _const_spec(bd.shape)] + slabs,
        out_specs=[rows(w) for w in widths] + slabs,
        out_shape=[jax.ShapeDtypeStruct((t, w), BF16) for w in widths]
        + [jax.ShapeDtypeStruct(w.shape, BF16) for w in late_w],
        compiler_params=pltpu.CompilerParams(dimension_semantics=("arbitrary",),
                                             vmem_limit_bytes=VMEM_LIMIT_BYTES),
        name="in_proj",
    )(x2d, pos_col, gattn, sp, w1, w2, wq, wkv, gcq, gckv, bd, *late_w)


_SWA_MASK_DIST = -NEG_INF * 2.0 ** 8


def _swa_tile(i, ss_ref, q_ref, kc_ref, kp_ref, vc_ref, vp_ref, pq_ref, pkc_ref, pkp_ref, o_ref, *, tq,
              filler=None):
    g = SWA_Q_HEADS // SWA_KV_HEADS
    kk = lax.broadcasted_iota(jnp.int32, (BLOCK, BLOCK), 0)
    qq = lax.broadcasted_iota(jnp.int32, (BLOCK, BLOCK), 1)
    from_prev = kk > qq
    prev_w = jnp.where(from_prev, 1.0, 0.0).astype(BF16)
    cur_w = jnp.where(from_prev, 0.0, 1.0).astype(BF16)
    lane_lo = lax.broadcasted_iota(jnp.int32, (BLOCK, LANES), 1) < SWA_HEAD_DIM
    lo_w = jnp.where(lane_lo, 1.0, 0.0).astype(BF16)
    hi_w = jnp.where(lane_lo, 0.0, 1.0).astype(BF16)

    for j in range(tq // BLOCK):
        if filler is not None:
            filler(j, tq // BLOCK)
        rows = slice(j * BLOCK, (j + 1) * BLOCK)
        prows = slice((j - 1) * BLOCK, j * BLOCK)
        k_prev, v_prev, pk_prev = ((kp_ref[...], vp_ref[...], pkp_ref[...]) if j == 0 else
                                   (kc_ref[prows, :], vc_ref[prows, :], pkc_ref[prows, :]))
        kj = jnp.concatenate([k_prev, kc_ref[rows, :]], axis=0)
        vj = jnp.concatenate([v_prev, vc_ref[rows, :]], axis=0)
        pq = pq_ref[:, rows]
        dist = jnp.where(from_prev, jnp.abs(pk_prev - pq), jnp.abs(pkc_ref[rows, :] - pq)).astype(F32)
        if j == 0:
            dist = jnp.where(from_prev, jnp.maximum(dist, jnp.where(i == 0, _SWA_MASK_DIST, 0.0)), dist)
        for kv in range(SWA_KV_HEADS):
            kvl = slice(kv * LANES, (kv + 1) * LANES)
            qm = jnp.concatenate(
                [q_ref[rows, (kv * g + hh) // 2 * LANES:((kv * g + hh) // 2 + 1) * LANES]
                 * (lo_w if hh % 2 == 0 else hi_w) for hh in range(g)], axis=0)
            st = _dot_nt(kj[:, kvl], qm)
            pts = []
            for hh in range(g):
                head = kv * g + hh
                cols = slice(hh * BLOCK, (hh + 1) * BLOCK)
                t = jnp.where(from_prev, st[:BLOCK, cols], st[BLOCK:, cols]) - ss_ref[0, head] * dist
                sink = ss_ref[1, head]
                m = jnp.maximum(jnp.max(t, axis=0, keepdims=True), sink)
                e = jnp.exp2(t - m)
                den = jnp.sum(e, axis=0, keepdims=True) + jnp.exp2(sink - m)
                p = (e * (1.0 / den)).astype(BF16)
                pts.append(jnp.concatenate([p * prev_w, p * cur_w], axis=0))
            pt = jnp.concatenate(pts, axis=1)
            o = lax.dot_general(pt, vj[:, kvl], (((0,), (0,)), ((), ())), preferred_element_type=F32)
            for pr in range(g // 2):
                grp = kv * (g // 2) + pr
                even = o[(2 * pr) * BLOCK:(2 * pr + 1) * BLOCK]
                odd = o[(2 * pr + 1) * BLOCK:(2 * pr + 2) * BLOCK]
                o_ref[rows, grp * LANES:(grp + 1) * LANES] = jnp.where(lane_lo, even, odd).astype(BF16)


def _mem_attn_tile(q_ref, k_ref, v_ref, o_ref):
    for h in range(MEM_HEADS):
        sl = slice(h * MEM_HEAD_DIM, (h + 1) * MEM_HEAD_DIM)
        s = _dot_nt(q_ref[:, sl], k_ref[:, sl])
        e = jnp.exp2(s - jnp.max(s, axis=-1, keepdims=True))
        o = _dot(e.astype(BF16), v_ref[:, 2 * h * LANES:(2 * h + 2) * LANES])
        o_ref[:, sl] = (o[:, :MEM_HEAD_DIM] / o[:, MEM_HEAD_DIM:]).astype(BF16)


def _swa_out_kernel(ss_ref, q_ref, kc_ref, kp_ref, vc_ref, vp_ref, pq_ref, pkc_ref, pkp_ref,
                    x_ref, yb_ref, qm_ref, km_ref, vm_ref, w_ref, h_ref, ya_sc, ym_sc, *, tq):
    na = ya_sc.shape[1]
    nb = na + yb_ref.shape[1]
    _mem_attn_tile(qm_ref, km_ref, vm_ref, ym_sc)
    d = h_ref.shape[1]

    def partial_projection(j, n):
        cols = slice(j * d // n, (j + 1) * d // n)
        h_ref[:, cols] = (x_ref[:, cols] + _dot(yb_ref[...], w_ref[na:nb, cols])
                          + _dot(ym_sc[...], w_ref[nb:, cols]))

    _swa_tile(pl.program_id(1), ss_ref, q_ref, kc_ref, kp_ref, vc_ref, vp_ref, pq_ref, pkc_ref, pkp_ref,
              ya_sc, tq=tq, filler=partial_projection)
    h_ref[...] += _dot(ya_sc[...], w_ref[0:na, :])


def _swa_out(slope_sink, qa, ka, va, pos_col, pos_row, x2d, yb, qm, km, vm, w_out, batch, seq, m_len, tq):
    t, d = x2d.shape
    nq = seq // tq
    nb = tq // BLOCK
    na = SWA_Q_HEADS * SWA_HEAD_DIM
    hd = MEM_HEADS * MEM_HEAD_DIM
    cur = lambda w: pl.BlockSpec((tq, w), lambda b, i: (b * nq + i, 0))
    prev = lambda w: pl.BlockSpec(
        (BLOCK, w), lambda b, i: (b * nq * nb + jnp.maximum(i * nb - 1, 0), 0))
    return pl.pallas_call(
        functools.partial(_swa_out_kernel, tq=tq),
        grid=(batch, nq),
        in_specs=[pl.BlockSpec(memory_space=pltpu.SMEM),
                  cur(na), cur(2 * LANES), prev(2 * LANES), cur(2 * LANES), prev(2 * LANES),
                  pl.BlockSpec((None, 1, tq), lambda b, i: (b, 0, i)),
                  cur(1), prev(1),
                  cur(d), cur(yb.shape[1]), cur(hd),
                  pl.BlockSpec((m_len, hd), lambda b, i: (b, 0)),
                  pl.BlockSpec((m_len, 2 * hd), lambda b, i: (b, 0)),
                  _const_spec(w_out.shape)],
        out_specs=cur(d),
        out_shape=jax.ShapeDtypeStruct((t, d), F32),
        scratch_shapes=[pltpu.VMEM((tq, na), BF16), pltpu.VMEM((tq, hd), BF16)],
        compiler_params=pltpu.CompilerParams(dimension_semantics=("arbitrary", "arbitrary"),
                                             vmem_limit_bytes=VMEM_LIMIT_BYTES),
        name="swa_out",
    )(slope_sink, qa, ka, ka, va, va, pos_row, pos_col, pos_col, x2d, yb, qm, km, vm, w_out)


def _mla_kernel(q_ref, kn_ref, kr_ref, v_ref, o_ref, s_sc, m_sc, acc_sc, *, tq, hps):
    qi = pl.program_id(2)
    qw = 2 * LANES
    tk = tq // 2

    def scores_to(slot, j, row0=0):
        k0 = pl.multiple_of(j * tk, tk)
        kr = kr_ref[pl.ds(k0, tk), :]
        for hh in range(hps):
            k = jnp.concatenate([kn_ref[pl.ds(k0, tk), hh * LANES:(hh + 1) * LANES], kr], axis=1)
            s_sc[slot, hh, row0:, :] = _dot_nt(q_ref[row0:, hh * qw:(hh + 1) * qw], k)

    def update_from(slot, j, row0=0, masked=False):
        k0 = pl.multiple_of(j * tk, tk)
        for hh in range(hps):
            s = s_sc[slot, hh, row0:, :]
            if masked:
                r = lax.broadcasted_iota(jnp.int32, s.shape, 0)
                c = lax.broadcasted_iota(jnp.int32, s.shape, 1)
                s = jnp.where(c <= r, s, NEG_INF)
            m = m_sc[hh, row0:, :]
            m_new = jnp.maximum(m, jnp.max(s, axis=-1, keepdims=True))
            p = jnp.exp2(s - m_new)
            m_sc[hh, row0:, :] = m_new
            acc_sc[hh, row0:, :] = jnp.exp2(m - m_new) * acc_sc[hh, row0:, :] + _dot(
                p.astype(BF16), v_ref[pl.ds(k0, tk), hh * qw:(hh + 1) * qw])

    m_sc[...] = jnp.full(m_sc.shape, NEG_INF, F32)
    acc_sc[...] = jnp.zeros(acc_sc.shape, F32)
    scores_to(0, 0)

    def pair(p, carry):
        j = 2 * p
        scores_to(1, j + 1)
        update_from(0, j)
        scores_to(0, j + 2)
        update_from(1, j + 1)
        return carry

    lax.fori_loop(0, qi, pair, 0)
    scores_to(1, 2 * qi + 1, row0=tk)
    update_from(0, 2 * qi, masked=True)
    update_from(1, 2 * qi + 1, row0=tk, masked=True)

    for hh in range(hps):
        acc = acc_sc[hh]
        o_ref[:, hh * MLA_V_DIM:(hh + 1) * MLA_V_DIM] = (acc[:, :MLA_V_DIM] / acc[:, MLA_V_DIM:]).astype(BF16)


def _mla_attn(qb, kn, krd, vb, batch, seq, tq, hps):
    t = batch * seq
    nq = seq // tq
    assert tq % 2 == 0 and seq % tq == 0
    return pl.pallas_call(
        functools.partial(_mla_kernel, tq=tq, hps=hps),
        grid=(batch, MLA_HEADS // hps, nq),
        in_specs=[pl.BlockSpec((tq, 2 * LANES * hps), lambda b, h, i: (b * nq + i, h)),
                  pl.BlockSpec((seq, MLA_NOPE_DIM * hps), lambda b, h, i: (b, h)),
                  pl.BlockSpec((seq, LANES), lambda b, h, i: (b, 0)),
                  pl.BlockSpec((seq, 2 * LANES * hps), lambda b, h, i: (b, h))],
        out_specs=pl.BlockSpec((tq, MLA_V_DIM * hps), lambda b, h, i: (b * nq + i, h)),
        out_shape=jax.ShapeDtypeStruct((t, MLA_HEADS * MLA_V_DIM), BF16),
        scratch_shapes=[pltpu.VMEM((2, hps, tq, tq // 2), F32), pltpu.VMEM((hps, tq, 1), F32),
                        pltpu.VMEM((hps, tq, 2 * LANES), F32)],
        compiler_params=pltpu.CompilerParams(dimension_semantics=("arbitrary",) * 3,
                                             vmem_limit_bytes=VMEM_LIMIT_BYTES),
        name="mla_attn",
    )(qb, kn, krd, vb)


def _ffn_kernel(h_ref, g_ref, wg_ref, wu_ref, wd_ref, o_ref, fn_ref):
    j = pl.program_id(1)

    @pl.when(j == 0)
    def _():
        h = h_ref[...]
        fn_ref[...] = _rms(h, g_ref[...]).astype(BF16)
        o_ref[...] = h

    fn = fn_ref[...]
    gate = _dot(fn, wg_ref[...])
    up = _dot(fn, wu_ref[...])
    act = (gate * jax.nn.sigmoid(gate) * up).astype(BF16)
    o_ref[...] += _dot(act, wd_ref[...])


def _ffn(h, g, wg, wu, wd, tm, tf):
    t, d = h.shape
    dff = wg.shape[1]
    return pl.pallas_call(
        _ffn_kernel,
        grid=(t // tm, dff // tf),
        in_specs=[pl.BlockSpec((tm, d), lambda i, j: (i, 0)),
                  _const_spec((1, d)),
                  pl.BlockSpec((d, tf), lambda i, j: (0, j)),
                  pl.BlockSpec((d, tf), lambda i, j: (0, j)),
                  pl.BlockSpec((tf, d), lambda i, j: (j, 0))],
        out_specs=pl.BlockSpec((tm, d), lambda i, j: (i, 0)),
        out_shape=jax.ShapeDtypeStruct((t, d), F32),
        scratch_shapes=[pltpu.VMEM((tm, d), BF16)],
        compiler_params=pltpu.CompilerParams(dimension_semantics=("arbitrary", "arbitrary"),
                                             vmem_limit_bytes=VMEM_LIMIT_BYTES),
        name="ffn",
    )(h, g, wg, wu, wd)


def _tile_row(v, width):
    v = v.astype(F32).reshape(-1)
    return jnp.tile(v, width // v.shape[0])


def _layer(h, mem2d, pos_col, pos_row, batch, seq, m_len, w_in_stack, layer, p):
    (attn_norm_g, swa_q_g, swa_k_g, swa_sinks, cq_g, ckv_g, w_uq, w_ukv, qn_g, qr_g, kn_g, kr_g,
     mem_g, w_mem_kv, memq_g, memk_g, w_out, ffn_g, w_gate, w_up, w_down) = p
    width = 2 * LANES
    inv_freq = ROPE_THETA ** (-jnp.arange(0, MLA_ROPE_DIM, 2, dtype=F32) / MLA_ROPE_DIM)
    sign = jnp.concatenate([-jnp.ones((MLA_ROPE_DIM // 2,), F32), jnp.ones((MLA_ROPE_DIM // 2,), F32)])
    rows = [swa_q_g, swa_k_g, qn_g, qr_g, kn_g, kr_g, memq_g, inv_freq, sign]
    sp = jnp.stack([_tile_row(r, width) for r in rows]
                   + [jnp.zeros((width,), F32)] * (_SP_ROWS - len(rows)))

    qd = MLA_NOPE_DIM + MLA_ROPE_DIM
    wq3 = w_uq.reshape(w_uq.shape[0], MLA_HEADS, qd)
    wq = jnp.concatenate([wq3, wq3[:, :, MLA_NOPE_DIM:]], axis=2).reshape(w_uq.shape[0], -1).astype(BF16)
    wkv3 = w_ukv.reshape(w_ukv.shape[0], MLA_HEADS, MLA_NOPE_DIM + MLA_V_DIM)
    wkv = jnp.concatenate([wkv3[:, :, :MLA_NOPE_DIM].reshape(w_ukv.shape[0], -1),
                           wkv3[:, :, MLA_NOPE_DIM:].reshape(w_ukv.shape[0], -1)], axis=1).astype(BF16)
    idx = jnp.arange(MXU_DIM) // SWA_HEAD_DIM
    bd = (idx[:, None] == idx[None, :]).astype(BF16)
    slopes = 2.0 ** (-8.0 * jnp.arange(1, SWA_Q_HEADS + 1, dtype=F32) / SWA_Q_HEADS)
    slope_sink = jnp.stack([slopes, swa_sinks.astype(F32)]) * LOG2E

    km, vm, w1, w2 = _mem_kv(mem2d, mem_g.reshape(1, -1), w_mem_kv, memk_g.reshape(1, -1), w_in_stack,
                             layer, batch, m_len)
    qa, ka, va, krd, qm, qb, kn, vb, wg16, wu16, wd16, wo16 = _in_proj(
        h, pos_col, attn_norm_g.reshape(1, -1), sp, w1, w2, wq, wkv, cq_g.reshape(1, -1),
        ckv_g.reshape(1, -1), bd, (w_gate, w_up, w_down, w_out), tm=IN_PROJ_ROWS)
    yb = _mla_attn(qb, kn, krd, vb, batch, seq, tq=MLA_Q_ROWS, hps=MLA_HEADS_PER_STEP)
    h = _swa_out(slope_sink, qa, ka, va, pos_col, pos_row, h, yb, qm, km, vm, wo16, batch, seq, m_len,
                 tq=ATTN_OUT_ROWS)
    return _ffn(h, ffn_g.reshape(1, -1), wg16, wu16, wd16, tm=FFN_ROWS, tf=FFN_COLS)


def kernel(x, mem, positions, attn_norm_g, w_in, swa_q_norm_g, swa_k_norm_g, swa_sinks, mla_cq_norm_g,
           mla_ckv_norm_g, w_uq, w_ukv, mla_qn_norm_g, mla_qr_norm_g, mla_kn_norm_g, mla_kr_norm_g,
           mem_norm_g, w_mem_kv, mem_q_norm_g, mem_k_norm_g, w_out, ffn_norm_g, w_gate, w_up, w_down):
    batch, seq, d = x.shape
    m_len = mem.shape[1]
    stacked = (attn_norm_g, swa_q_norm_g, swa_k_norm_g, swa_sinks, mla_cq_norm_g, mla_ckv_norm_g,
               w_uq, w_ukv, mla_qn_norm_g, mla_qr_norm_g, mla_kn_norm_g, mla_kr_norm_g, mem_norm_g,
               w_mem_kv, mem_q_norm_g, mem_k_norm_g, w_out, ffn_norm_g, w_gate, w_up, w_down)
    h = x.reshape(batch * seq, d)
    mem2d = mem.reshape(batch * m_len, d)
    pos_col = positions.reshape(batch * seq, 1)
    pos_row = positions.reshape(batch, 1, seq)
    for l in range(attn_norm_g.shape[0]):
        h = _layer(h, mem2d, pos_col, pos_row, batch, seq, m_len, w_in, l, tuple(a[l] for a in stacked))
    return h.reshape(batch, seq, d)
```

```python
import functools

import jax
import jax.numpy as jnp
from jax import lax
from jax.experimental import pallas as pl
from jax.experimental.pallas import tpu as pltpu

EPS = 1e-6
NEG_INF = -1e30
LOG2E = 1.4426950408889634
BLOCK = 128

SWA_Q_HEADS = 16
SWA_KV_HEADS = 2
SWA_HEAD_DIM = 64

MLA_HEADS = 4
MLA_NOPE_DIM = 128
MLA_ROPE_DIM = 64
MLA_V_DIM = 128
ROPE_THETA = 10000.0

MEM_HEADS = 4
MEM_HEAD_DIM = 128

LANES = 128
MXU_DIM = 256
VMEM_LIMIT_BYTES = 56 * 1024 * 1024

IN_PROJ_ROWS = 512
ATTN_OUT_ROWS = 512
MLA_Q_ROWS = 1024
MLA_HEADS_PER_STEP = 2
FFN_ROWS = 1024
FFN_COLS = 512

F32 = jnp.float32
BF16 = jnp.bfloat16

_SP_SWA_Q, _SP_SWA_K, _SP_QN, _SP_QR, _SP_KN, _SP_KR, _SP_MEMQ, _SP_FREQ, _SP_SIGN = range(9)
_SP_ROWS = 16


def _dot(a, b):
    return jnp.dot(a, b, preferred_element_type=F32)


def _dot_nt(a, b):
    return lax.dot_general(a, b, (((1,), (1,)), ((), ())), preferred_element_type=F32)


def _rms(x, g):
    return x * lax.rsqrt(jnp.mean(x * x, axis=-1, keepdims=True) + EPS) * g


def _const_spec(shape):
    nd = len(shape)
    return pl.BlockSpec(shape, lambda *_: (0,) * nd, pipeline_mode=pl.Buffered(1))


def _mem_kv_kernel(mem_ref, g_ref, w32_ref, gk_ref, win_ref, k_ref, v_ref, w1_ref, w2_ref, w_ref):
    @pl.when(pl.program_id(0) == 0)
    def _():
        w_ref[...] = w32_ref[...].astype(BF16)

    w1_ref[...] = win_ref[:, :_W1_COLS].astype(BF16)
    half = MLA_ROPE_DIM
    lo = lax.broadcasted_iota(jnp.int32, (1, LANES), 1) < half
    n_full = (win_ref.shape[1] - _W1_COLS) // LANES
    tail = [win_ref[:, _W1_COLS + k * LANES:_W1_COLS + (k + 1) * LANES] for k in range(n_full)]
    last = win_ref[:, _W1_COLS + n_full * LANES:]
    tail.append(jnp.concatenate([last, last], axis=1))
    swapped = [pltpu.roll(c, half, 1) for c in tail[:n_full]]
    w2_ref[:, :LANES] = jnp.where(lo, tail[0], swapped[0]).astype(BF16)
    for k in range(1, n_full + 1):
        nxt = swapped[k] if k < n_full else tail[k]
        w2_ref[:, k * LANES:(k + 1) * LANES] = jnp.where(lo, swapped[k - 1], nxt).astype(BF16)

    xn = _rms(mem_ref[...], g_ref[...]).astype(BF16)
    kv = _dot(xn, w_ref[...])
    hd = MEM_HEADS * MEM_HEAD_DIM
    for h in range(MEM_HEADS):
        sl = slice(h * MEM_HEAD_DIM, (h + 1) * MEM_HEAD_DIM)
        k_ref[:, sl] = _rms(kv[:, sl], gk_ref[...]).astype(BF16)
        v_ref[:, 2 * h * LANES:(2 * h + 1) * LANES] = kv[:, hd + h * MEM_HEAD_DIM:
                                                         hd + (h + 1) * MEM_HEAD_DIM].astype(BF16)
        v_ref[:, (2 * h + 1) * LANES:(2 * h + 2) * LANES] = jnp.ones((kv.shape[0], LANES), BF16)


def _mem_kv(mem2d, g, w, gk, w_in_stack, layer, batch, m_len):
    d = mem2d.shape[1]
    hd = MEM_HEADS * MEM_HEAD_DIM
    _, rows, cols = w_in_stack.shape
    assert rows % (16 * batch) == 0 and (cols - _W1_COLS) % LANES == MLA_ROPE_DIM
    slab = rows // batch
    w2_cols = cols - _W1_COLS + MLA_ROPE_DIM
    return pl.pallas_call(
        _mem_kv_kernel,
        grid=(batch,),
        in_specs=[pl.BlockSpec((m_len, d), lambda b: (b, 0)),
                  _const_spec((1, d)), _const_spec((d, 2 * hd)), _const_spec((1, MEM_HEAD_DIM)),
                  pl.BlockSpec((None, slab, cols), lambda b: (layer, b, 0))],
        out_specs=[pl.BlockSpec((m_len, hd), lambda b: (b, 0)), pl.BlockSpec((m_len, 2 * hd), lambda b: (b, 0)),
                   pl.BlockSpec((slab, _W1_COLS), lambda b: (b, 0)), pl.BlockSpec((slab, w2_cols), lambda b: (b, 0))],
        out_shape=[jax.ShapeDtypeStruct((batch * m_len, hd), BF16),
                   jax.ShapeDtypeStruct((batch * m_len, 2 * hd), BF16),
                   jax.ShapeDtypeStruct((rows, _W1_COLS), BF16), jax.ShapeDtypeStruct((rows, w2_cols), BF16)],
        scratch_shapes=[pltpu.VMEM((d, 2 * hd), BF16)],
        compiler_params=pltpu.CompilerParams(dimension_semantics=("arbitrary",),
                                             vmem_limit_bytes=VMEM_LIMIT_BYTES),
        name="mem_kv",
    )(mem2d, g, w, gk, w_in_stack)


_C_QA = (0, 1024)
_C_KVA = (1024, 1280)
_C_CQ = (1280, 1792)
_C_CKV = (1792, 2304)
_W1_COLS = 2304
_C_KR = (2304, 2432)
_C_QM = (2432, 2944)


def _in_proj_kernel(x_ref, pos_ref, gattn_ref, sp_ref, w1_ref, w2_ref, wq_ref, wkv_ref, gcq_ref, gckv_ref,
                    bd_ref, wg32_ref, wu32_ref, wd32_ref, wo32_ref,
                    qa_ref, ka_ref, va_ref, krd_ref, qm_ref, qb_ref, kn_ref, vb_ref,
                    wg16_ref, wu16_ref, wd16_ref, wo16_ref):
    wg16_ref[...] = wg32_ref[...].astype(BF16)
    wu16_ref[...] = wu32_ref[...].astype(BF16)
    wd16_ref[...] = wd32_ref[...].astype(BF16)
    wo16_ref[...] = wo32_ref[...].astype(BF16)
    sp = sp_ref[...]
    row = lambda r, n=LANES: sp[r:r + 1, :n]
    hn = _rms(x_ref[...], gattn_ref[...]).astype(BF16)

    def proj(cols):
        if cols[0] >= _W1_COLS:
            return _dot(hn, w2_ref[:, cols[0] - _W1_COLS:cols[1] - _W1_COLS])
        return _dot(hn, w1_ref[:, cols[0]:cols[1]])

    def head64_rms(blk, g):
        ss = _dot((blk * blk).astype(BF16), bd_ref[...])
        return blk * lax.rsqrt(ss * (1.0 / SWA_HEAD_DIM) + EPS) * g

    ang = pos_ref[...].astype(F32) * row(_SP_FREQ)
    cos = jnp.cos(ang)
    sin_signed = jnp.sin(ang) * row(_SP_SIGN)

    def rope_dup(y):
        return y * cos + pltpu.roll(y, MLA_ROPE_DIM // 2, 1) * sin_signed

    cq = proj(_C_CQ)
    ckv = proj(_C_CKV)
    qm = proj(_C_QM)
    kva = proj(_C_KVA)
    kr = proj(_C_KR)
    cqn = _rms(cq, gcq_ref[...]).astype(BF16)
    ckvn = _rms(ckv, gckv_ref[...]).astype(BF16)

    for h in range(MEM_HEADS):
        sl = slice(h * MEM_HEAD_DIM, (h + 1) * MEM_HEAD_DIM)
        qm_ref[:, sl] = (_rms(qm[:, sl], row(_SP_MEMQ)) * (LOG2E * MEM_HEAD_DIM ** -0.5)).astype(BF16)
    lo = lax.broadcasted_iota(jnp.int32, (1, LANES), 1) < SWA_HEAD_DIM
    for src, dst in ((head64_rms(kva, row(_SP_SWA_K, MXU_DIM))[:, :LANES], ka_ref), (kva[:, LANES:], va_ref)):
        swapped = pltpu.roll(src, SWA_HEAD_DIM, 1)
        dst[:, :LANES] = jnp.where(lo, src, swapped).astype(BF16)
        dst[:, LANES:] = jnp.where(lo, swapped, src).astype(BF16)
    krd_ref[...] = rope_dup(_rms(kr, row(_SP_KR))).astype(BF16)

    qb = _dot(cqn, wq_ref[...])
    kvb = _dot(ckvn, wkv_ref[...])
    qa = proj(_C_QA)

    scale = LOG2E * (MLA_NOPE_DIM + MLA_ROPE_DIM) ** -0.5
    for h in range(MLA_HEADS):
        o = h * 2 * LANES
        qb_ref[:, o:o + LANES] = (_rms(qb[:, o:o + LANES], row(_SP_QN)) * scale).astype(BF16)
        qr = rope_dup(_rms(qb[:, o + LANES:o + 2 * LANES], row(_SP_QR)))
        qb_ref[:, o + LANES:o + 2 * LANES] = (qr * (0.5 * scale)).astype(BF16)
    hn_cols = MLA_HEADS * MLA_NOPE_DIM
    for h in range(MLA_HEADS):
        sl = slice(h * MLA_NOPE_DIM, (h + 1) * MLA_NOPE_DIM)
        kn_ref[:, sl] = _rms(kvb[:, sl], row(_SP_KN)).astype(BF16)
        vb_ref[:, 2 * h * LANES:(2 * h + 1) * LANES] = kvb[:, hn_cols + h * MLA_V_DIM:
                                                          hn_cols + (h + 1) * MLA_V_DIM].astype(BF16)
        vb_ref[:, (2 * h + 1) * LANES:(2 * h + 2) * LANES] = jnp.ones((kvb.shape[0], LANES), BF16)

    for c in range(_C_QA[1] // MXU_DIM):
        sl = slice(c * MXU_DIM, (c + 1) * MXU_DIM)
        qa_ref[:, sl] = (head64_rms(qa[:, sl], row(_SP_SWA_Q, MXU_DIM))
                         * (LOG2E * SWA_HEAD_DIM ** -0.5)).astype(BF16)


def _in_proj(x2d, pos_col, gattn, sp, w1, w2, wq, wkv, gcq, gckv, bd, late_w, tm):
    t, d = x2d.shape
    steps = t // tm
    widths = (1024, 256, 256, 128, 512, 1024, 512, 1024)
    rows = lambda w: pl.BlockSpec((tm, w), lambda i: (i, 0))
    slabs = []
    for w in late_w:
        assert w.shape[0] % (16 * steps) == 0, (w.shape, steps)
        slabs.append(pl.BlockSpec((w.shape[0] // steps, w.shape[1]), lambda i: (i, 0)))
    return pl.pallas_call(
        _in_proj_kernel,
        grid=(steps,),
        in_specs=[rows(d), rows(1), _const_spec(gattn.shape), _const_spec(sp.shape),
                  _const_spec(w1.shape), _const_spec(w2.shape), _const_spec(wq.shape),
                  _const_spec(wkv.shape),
                  _const_spec(gcq.shape), _const_spec(gckv.shape), _const_spec(bd.shape)] + slabs,
        out_specs=[rows(w) for w in widths] + slabs,
        out_shape=[jax.ShapeDtypeStruct((t, w), BF16) for w in widths]
        + [jax.ShapeDtypeStruct(w.shape, BF16) for w in late_w],
        compiler_params=pltpu.CompilerParams(dimension_semantics=("arbitrary",),
                                             vmem_limit_bytes=VMEM_LIMIT_BYTES),
        name="in_proj",
    )(x2d, pos_col, gattn, sp, w1, w2, wq, wkv, gcq, gckv, bd, *late_w)


_SWA_MASK_DIST = -NEG_INF * 2.0 ** 8


def _swa_tile(i, ss_ref, q_ref, kc_ref, kp_ref, vc_ref, vp_ref, pq_ref, pkc_ref, pkp_ref, o_ref, *, tq,
              filler=None):
    g = SWA_Q_HEADS // SWA_KV_HEADS
    kk = lax.broadcasted_iota(jnp.int32, (BLOCK, BLOCK), 0)
    qq = lax.broadcasted_iota(jnp.int32, (BLOCK, BLOCK), 1)
    from_prev = kk > qq
    prev_w = jnp.where(from_prev, 1.0, 0.0).astype(BF16)
    cur_w = jnp.where(from_prev, 0.0, 1.0).astype(BF16)
    lane_lo = lax.broadcasted_iota(jnp.int32, (BLOCK, LANES), 1) < SWA_HEAD_DIM
    lo_w = jnp.where(lane_lo, 1.0, 0.0).astype(BF16)
    hi_w = jnp.where(lane_lo, 0.0, 1.0).astype(BF16)

    for j in range(tq // BLOCK):
        if filler is not None:
            filler(j, tq // BLOCK)
        rows = slice(j * BLOCK, (j + 1) * BLOCK)
        prows = slice((j - 1) * BLOCK, j * BLOCK)
        k_prev, v_prev, pk_prev = ((kp_ref[...], vp_ref[...], pkp_ref[...]) if j == 0 else
                                   (kc_ref[prows, :], vc_ref[prows, :], pkc_ref[prows, :]))
        kj = jnp.concatenate([k_prev, kc_ref[rows, :]], axis=0)
        vj = jnp.concatenate([v_prev, vc_ref[rows, :]], axis=0)
        pq = pq_ref[:, rows]
        dist = jnp.where(from_prev, jnp.abs(pk_prev - pq), jnp.abs(pkc_ref[rows, :] - pq)).astype(F32)
        if j == 0:
            dist = jnp.where(from_prev, jnp.maximum(dist, jnp.where(i == 0, _SWA_MASK_DIST, 0.0)), dist)
        for kv in range(SWA_KV_HEADS):
            kvl = slice(kv * LANES, (kv + 1) * LANES)
            qm = jnp.concatenate(
                [q_ref[rows, (kv * g + hh) // 2 * LANES:((kv * g + hh) // 2 + 1) * LANES]
                 * (lo_w if hh % 2 == 0 else hi_w) for hh in range(g)], axis=0)
            st = _dot_nt(kj[:, kvl], qm)
            pts = []
            for hh in range(g):
                head = kv * g + hh
                cols = slice(hh * BLOCK, (hh + 1) * BLOCK)
                t = jnp.where(from_prev, st[:BLOCK, cols], st[BLOCK:, cols]) - ss_ref[0, head] * dist
                sink = ss_ref[1, head]
                m = jnp.maximum(jnp.max(t, axis=0, keepdims=True), sink)
                e = jnp.exp2(t - m)
                den = jnp.sum(e, axis=0, keepdims=True) + jnp.exp2(sink - m)
                p = (e * (1.0 / den)).astype(BF16)
                pts.append(jnp.concatenate([p * prev_w, p * cur_w], axis=0))
            pt = jnp.concatenate(pts, axis=1)
            o = lax.dot_general(pt, vj[:, kvl], (((0,), (0,)), ((), ())), preferred_element_type=F32)
            for pr in range(g // 2):
                grp = kv * (g // 2) + pr
                even = o[(2 * pr) * BLOCK:(2 * pr + 1) * BLOCK]
                odd = o[(2 * pr + 1) * BLOCK:(2 * pr + 2) * BLOCK]
                o_ref[rows, grp * LANES:(grp + 1) * LANES] = jnp.where(lane_lo, even, odd).astype(BF16)


def _mem_attn_tile(q_ref, k_ref, v_ref, o_ref):
    for h in range(MEM_HEADS):
        sl = slice(h * MEM_HEAD_DIM, (h + 1) * MEM_HEAD_DIM)
        s = _dot_nt(q_ref[:, sl], k_ref[:, sl])
        e = jnp.exp2(s - jnp.max(s, axis=-1, keepdims=True))
        o = _dot(e.astype(BF16), v_ref[:, 2 * h * LANES:(2 * h + 2) * LANES])
        o_ref[:, sl] = (o[:, :MEM_HEAD_DIM] / o[:, MEM_HEAD_DIM:]).astype(BF16)


def _swa_out_kernel(ss_ref, q_ref, kc_ref, kp_ref, vc_ref, vp_ref, pq_ref, pkc_ref, pkp_ref,
                    x_ref, yb_ref, qm_ref, km_ref, vm_ref, w_ref, h_ref, ya_sc, ym_sc, *, tq):
    na = ya_sc.shape[1]
    nb = na + yb_ref.shape[1]
    _mem_attn_tile(qm_ref, km_ref, vm_ref, ym_sc)
    d = h_ref.shape[1]

    def partial_projection(j, n):
        cols = slice(j * d // n, (j + 1) * d // n)
        h_ref[:, cols] = (x_ref[:, cols] + _dot(yb_ref[...], w_ref[na:nb, cols])
                          + _dot(ym_sc[...], w_ref[nb:, cols]))

    _swa_tile(pl.program_id(1), ss_ref, q_ref, kc_ref, kp_ref, vc_ref, vp_ref, pq_ref, pkc_ref, pkp_ref,
              ya_sc, tq=tq, filler=partial_projection)
    h_ref[...] += _dot(ya_sc[...], w_ref[0:na, :])


def _swa_out(slope_sink, qa, ka, va, pos_col, pos_row, x2d, yb, qm, km, vm, w_out, batch, seq, m_len, tq):
    t, d = x2d.shape
    nq = seq // tq
    nb = tq // BLOCK
    na = SWA_Q_HEADS * SWA_HEAD_DIM
    hd = MEM_HEADS * MEM_HEAD_DIM
    cur = lambda w: pl.BlockSpec((tq, w), lambda b, i: (b * nq + i, 0))
    prev = lambda w: pl.BlockSpec(
        (BLOCK, w), lambda b, i: (b * nq * nb + jnp.maximum(i * nb - 1, 0), 0))
    return pl.pallas_call(
        functools.partial(_swa_out_kernel, tq=tq),
        grid=(batch, nq),
        in_specs=[pl.BlockSpec(memory_space=pltpu.SMEM),
                  cur(na), cur(2 * LANES), prev(2 * LANES), cur(2 * LANES), prev(2 * LANES),
                  pl.BlockSpec((None, 1, tq), lambda b, i: (b, 0, i)),
                  cur(1), prev(1),
                  cur(d), cur(yb.shape[1]), cur(hd),
                  pl.BlockSpec((m_len, hd), lambda b, i: (b, 0)),
                  pl.BlockSpec((m_len, 2 * hd), lambda b, i: (b, 0)),
                  _const_spec(w_out.shape)],
        out_specs=cur(d),
        out_shape=jax.ShapeDtypeStruct((t, d), F32),
        scratch_shapes=[pltpu.VMEM((tq, na), BF16), pltpu.VMEM((tq, hd), BF16)],
        compiler_params=pltpu.CompilerParams(dimension_semantics=("arbitrary", "arbitrary"),
                                             vmem_limit_bytes=VMEM_LIMIT_BYTES),
        name="swa_out",
    )(slope_sink, qa, ka, ka, va, va, pos_row, pos_col, pos_col, x2d, yb, qm, km, vm, w_out)


def _mla_kernel(q_ref, kn_ref, kr_ref, v_ref, o_ref, s_sc, m_sc, acc_sc, *, tq, hps):
    qi = pl.program_id(2)
    qw = 2 * LANES
    tk = tq // 2

    all_heads = tuple(range(hps))

    def scores_to(slot, j, row0=0, heads=all_heads):
        k0 = pl.multiple_of(j * tk, tk)
        kr = kr_ref[pl.ds(k0, tk), :]
        for hh in heads:
            k = jnp.concatenate([kn_ref[pl.ds(k0, tk), hh * LANES:(hh + 1) * LANES], kr], axis=1)
            s_sc[slot, hh, row0:, :] = _dot_nt(q_ref[row0:, hh * qw:(hh + 1) * qw], k)

    def update_from(slot, j, row0=0, masked=False, heads=all_heads):
        k0 = pl.multiple_of(j * tk, tk)
        for hh in heads:
            for r0 in range(row0, tq, tk):
                rs = slice(r0, r0 + tk)
                s = s_sc[slot, hh, rs, :]
                if masked and r0 - row0 < tk:
                    r = lax.broadcasted_iota(jnp.int32, s.shape, 0) + (r0 - row0)
                    c = lax.broadcasted_iota(jnp.int32, s.shape, 1)
                    s = jnp.where(c <= r, s, NEG_INF)
                m = m_sc[hh, rs, :]
                m_new = jnp.maximum(m, jnp.max(s, axis=-1, keepdims=True))
                p = jnp.exp2(s - m_new)
                m_sc[hh, rs, :] = m_new
                acc_sc[hh, rs, :] = jnp.exp2(m - m_new) * acc_sc[hh, rs, :] + _dot(
                    p.astype(BF16), v_ref[pl.ds(k0, tk), hh * qw:(hh + 1) * qw])

    m_sc[...] = jnp.full(m_sc.shape, NEG_INF, F32)
    acc_sc[...] = jnp.zeros(acc_sc.shape, F32)

    scores_to(0, 0)

    def pair(p, carry):
        j = 2 * p
        for hh in all_heads:
            scores_to(1, j + 1, heads=(hh,))
            update_from(0, j, heads=(hh,))
        for hh in all_heads:
            scores_to(0, j + 2, heads=(hh,))
            update_from(1, j + 1, heads=(hh,))
        return carry

    lax.fori_loop(0, qi, pair, 0)
    for hh in all_heads:
        scores_to(1, 2 * qi + 1, row0=tk, heads=(hh,))
        update_from(0, 2 * qi, masked=True, heads=(hh,))
    update_from(1, 2 * qi + 1, row0=tk, masked=True)

    for hh in range(hps):
        acc = acc_sc[hh]
        o_ref[:, hh * MLA_V_DIM:(hh + 1) * MLA_V_DIM] = (acc[:, :MLA_V_DIM] / acc[:, MLA_V_DIM:]).astype(BF16)


def _mla_attn(qb, kn, krd, vb, batch, seq, tq, hps):
    t = batch * seq
    nq = seq // tq
    assert tq % 2 == 0 and seq % tq == 0
    return pl.pallas_call(
        functools.partial(_mla_kernel, tq=tq, hps=hps),
        grid=(batch, MLA_HEADS // hps, nq),
        in_specs=[pl.BlockSpec((tq, 2 * LANES * hps), lambda b, h, i: (b * nq + i, h)),
                  pl.BlockSpec((seq, MLA_NOPE_DIM * hps), lambda b, h, i: (b, h)),
                  pl.BlockSpec((seq, LANES), lambda b, h, i: (b, 0)),
                  pl.BlockSpec((seq, 2 * LANES * hps), lambda b, h, i: (b, h))],
        out_specs=pl.BlockSpec((tq, MLA_V_DIM * hps), lambda b, h, i: (b * nq + i, h)),
        out_shape=jax.ShapeDtypeStruct((t, MLA_HEADS * MLA_V_DIM), BF16),
        scratch_shapes=[pltpu.VMEM((2, hps, tq, tq // 2), F32), pltpu.VMEM((hps, tq, 1), F32),
                        pltpu.VMEM((hps, tq, 2 * LANES), F32)],
        compiler_params=pltpu.CompilerParams(dimension_semantics=("arbitrary",) * 3,
                                             vmem_limit_bytes=VMEM_LIMIT_BYTES),
        name="mla_attn",
    )(qb, kn, krd, vb)


def _ffn_kernel(h_ref, g_ref, wg_ref, wu_ref, wd_ref, o_ref, fn_ref):
    j = pl.program_id(1)

    @pl.when(j == 0)
    def _():
        h = h_ref[...]
        fn_ref[...] = _rms(h, g_ref[...]).astype(BF16)
        o_ref[...] = h

    fn = fn_ref[...]
    half = wg_ref.shape[1] // 2
    acts = []
    for c in range(2):
        cols = slice(c * half, (c + 1) * half)
        gate = _dot(fn, wg_ref[:, cols])
        up = _dot(fn, wu_ref[:, cols])
        acts.append((gate * jax.nn.sigmoid(gate) * up).astype(BF16))
    o_ref[...] += _dot(acts[0], wd_ref[:half, :]) + _dot(acts[1], wd_ref[half:, :])


def _ffn(h, g, wg, wu, wd, tm, tf):
    t, d = h.shape
    dff = wg.shape[1]
    return pl.pallas_call(
        _ffn_kernel,
        grid=(t // tm, dff // tf),
        in_specs=[pl.BlockSpec((tm, d), lambda i, j: (i, 0)),
                  _const_spec((1, d)),
                  pl.BlockSpec((d, tf), lambda i, j: (0, j)),
                  pl.BlockSpec((d, tf), lambda i, j: (0, j)),
                  pl.BlockSpec((tf, d), lambda i, j: (j, 0))],
        out_specs=pl.BlockSpec((tm, d), lambda i, j: (i, 0)),
        out_shape=jax.ShapeDtypeStruct((t, d), F32),
        scratch_shapes=[pltpu.VMEM((tm, d), BF16)],
        compiler_params=pltpu.CompilerParams(dimension_semantics=("arbitrary", "arbitrary"),
                                             vmem_limit_bytes=VMEM_LIMIT_BYTES),
        name="ffn",
    )(h, g, wg, wu, wd)


def _tile_row(v, width):
    v = v.astype(F32).reshape(-1)
    return jnp.tile(v, width // v.shape[0])


def _layer(h, mem2d, pos_col, pos_row, batch, seq, m_len, w_in_stack, layer, p):
    (attn_norm_g, swa_q_g, swa_k_g, swa_sinks, cq_g, ckv_g, w_uq, w_ukv, qn_g, qr_g, kn_g, kr_g,
     mem_g, w_mem_kv, memq_g, memk_g, w_out, ffn_g, w_gate, w_up, w_down) = p
    width = 2 * LANES
    inv_freq = ROPE_THETA ** (-jnp.arange(0, MLA_ROPE_DIM, 2, dtype=F32) / MLA_ROPE_DIM)
    sign = jnp.concatenate([-jnp.ones((MLA_ROPE_DIM // 2,), F32), jnp.ones((MLA_ROPE_DIM // 2,), F32)])
    rows = [swa_q_g, swa_k_g, qn_g, qr_g, kn_g, kr_g, memq_g, inv_freq, sign]
    sp = jnp.stack([_tile_row(r, width) for r in rows]
                   + [jnp.zeros((width,), F32)] * (_SP_ROWS - len(rows)))

    qd = MLA_NOPE_DIM + MLA_ROPE_DIM
    wq3 = w_uq.reshape(w_uq.shape[0], MLA_HEADS, qd)
    wq = jnp.concatenate([wq3, wq3[:, :, MLA_NOPE_DIM:]], axis=2).reshape(w_uq.shape[0], -1).astype(BF16)
    wkv3 = w_ukv.reshape(w_ukv.shape[0], MLA_HEADS, MLA_NOPE_DIM + MLA_V_DIM)
    wkv = jnp.concatenate([wkv3[:, :, :MLA_NOPE_DIM].reshape(w_ukv.shape[0], -1),
                           wkv3[:, :, MLA_NOPE_DIM:].reshape(w_ukv.shape[0], -1)], axis=1).astype(BF16)
    idx = jnp.arange(MXU_DIM) // SWA_HEAD_DIM
    bd = (idx[:, None] == idx[None, :]).astype(BF16)
    slopes = 2.0 ** (-8.0 * jnp.arange(1, SWA_Q_HEADS + 1, dtype=F32) / SWA_Q_HEADS)
    slope_sink = jnp.stack([slopes, swa_sinks.astype(F32)]) * LOG2E

    km, vm, w1, w2 = _mem_kv(mem2d, mem_g.reshape(1, -1), w_mem_kv, memk_g.reshape(1, -1), w_in_stack,
                             layer, batch, m_len)
    qa, ka, va, krd, qm, qb, kn, vb, wg16, wu16, wd16, wo16 = _in_proj(
        h, pos_col, attn_norm_g.reshape(1, -1), sp, w1, w2, wq, wkv, cq_g.reshape(1, -1),
        ckv_g.reshape(1, -1), bd, (w_gate, w_up, w_down, w_out), tm=IN_PROJ_ROWS)
    yb = _mla_attn(qb, kn, krd, vb, batch, seq, tq=MLA_Q_ROWS, hps=MLA_HEADS_PER_STEP)
    h = _swa_out(slope_sink, qa, ka, va, pos_col, pos_row, h, yb, qm, km, vm, wo16, batch, seq, m_len,
                 tq=ATTN_OUT_ROWS)
    return _ffn(h, ffn_g.reshape(1, -1), wg16, wu16, wd16, tm=FFN_ROWS, tf=FFN_COLS)


def kernel(x, mem, positions, attn_norm_g, w_in, swa_q_norm_g, swa_k_norm_g, swa_sinks, mla_cq_norm_g,
           mla_ckv_norm_g, w_uq, w_ukv, mla_qn_norm_g, mla_qr_norm_g, mla_kn_norm_g, mla_kr_norm_g,
           mem_norm_g, w_mem_kv, mem_q_norm_g, mem_k_norm_g, w_out, ffn_norm_g, w_gate, w_up, w_down):
    batch, seq, d = x.shape
    m_len = mem.shape[1]
    stacked = (attn_norm_g, swa_q_norm_g, swa_k_norm_g, swa_sinks, mla_cq_norm_g, mla_ckv_norm_g,
               w_uq, w_ukv, mla_qn_norm_g, mla_qr_norm_g, mla_kn_norm_g, mla_kr_norm_g, mem_norm_g,
               w_mem_kv, mem_q_norm_g, mem_k_norm_g, w_out, ffn_norm_g, w_gate, w_up, w_down)
    h = x.reshape(batch * seq, d)
    mem2d = mem.reshape(batch * m_len, d)
    pos_col = positions.reshape(batch * seq, 1)
    pos_row = positions.reshape(batch, 1, seq)
    for l in range(attn_norm_g.shape[0]):
        h = _layer(h, mem2d, pos_col, pos_row, batch, seq, m_len, w_in, l, tuple(a[l] for a in stacked))
    return h.reshape(batch, seq, d)
```

```python
import functools

import jax
import jax.numpy as jnp
from jax import lax
from jax.experimental import pallas as pl
from jax.experimental.pallas import tpu as pltpu

EPS = 1e-6
NEG_INF = -1e30
LOG2E = 1.4426950408889634
BLOCK = 128

SWA_Q_HEADS = 16
SWA_KV_HEADS = 2
SWA_HEAD_DIM = 64

MLA_HEADS = 4
MLA_Q_RANK = 512
MLA_KV_RANK = 512
MLA_NOPE_DIM = 128
MLA_ROPE_DIM = 64
MLA_V_DIM = 128
ROPE_THETA = 10000.0

MEM_HEADS = 4
MEM_HEAD_DIM = 128

LANES = 128
MXU_DIM = 256
VMEM_LIMIT_BYTES = 56 * 1024 * 1024

IN_PROJ_ROWS = 512
ATTN_OUT_ROWS = 512
MLA_Q_ROWS = 1024
MLA_HEADS_PER_STEP = 2
FFN_ROWS = 1024
FFN_COLS = 512

F32 = jnp.float32
BF16 = jnp.bfloat16

_SP_SWA_Q, _SP_SWA_K, _SP_QN, _SP_QR, _SP_KN, _SP_KR, _SP_MEMQ, _SP_FREQ, _SP_SIGN = range(9)
_SP_ROWS = 16


def _dot(a, b):
    return jnp.dot(a, b, preferred_element_type=F32)


def _dot_nt(a, b):
    return lax.dot_general(a, b, (((1,), (1,)), ((), ())), preferred_element_type=F32)


def _rms(x, g):
    return x * lax.rsqrt(jnp.mean(x * x, axis=-1, keepdims=True) + EPS) * g


def _const_spec(shape):
    nd = len(shape)
    return pl.BlockSpec(shape, lambda *_: (0,) * nd, pipeline_mode=pl.Buffered(1))


def _mem_kv_kernel(mem_ref, g_ref, w32_ref, gk_ref, win_ref, k_ref, v_ref, w2_ref, w_ref):
    @pl.when(pl.program_id(0) == 0)
    def _():
        w_ref[...] = w32_ref[...].astype(BF16)

    half = MLA_ROPE_DIM
    lo = lax.broadcasted_iota(jnp.int32, (1, LANES), 1) < half
    n_full = (win_ref.shape[1] - _W1_COLS) // LANES
    tail = [win_ref[:, _W1_COLS + k * LANES:_W1_COLS + (k + 1) * LANES].astype(F32) for k in range(n_full)]
    last = win_ref[:, _W1_COLS + n_full * LANES:].astype(F32)
    tail.append(jnp.concatenate([last, last], axis=1))
    swapped = [pltpu.roll(c, half, 1) for c in tail[:n_full]]
    w2_ref[:, :LANES] = jnp.where(lo, tail[0], swapped[0]).astype(BF16)
    for k in range(1, n_full + 1):
        nxt = swapped[k] if k < n_full else tail[k]
        w2_ref[:, k * LANES:(k + 1) * LANES] = jnp.where(lo, swapped[k - 1], nxt).astype(BF16)

    xn = _rms(mem_ref[...], g_ref[...]).astype(BF16)
    kv = _dot(xn, w_ref[...])
    hd = MEM_HEADS * MEM_HEAD_DIM
    for h in range(MEM_HEADS):
        sl = slice(h * MEM_HEAD_DIM, (h + 1) * MEM_HEAD_DIM)
        k_ref[:, sl] = _rms(kv[:, sl], gk_ref[...]).astype(BF16)
        v_ref[:, 2 * h * LANES:(2 * h + 1) * LANES] = kv[:, hd + h * MEM_HEAD_DIM:
                                                         hd + (h + 1) * MEM_HEAD_DIM].astype(BF16)
        v_ref[:, (2 * h + 1) * LANES:(2 * h + 2) * LANES] = jnp.ones((kv.shape[0], LANES), BF16)


def _mem_kv(mem2d, g, w, gk, w_in_stack, layer, batch, m_len):
    d = mem2d.shape[1]
    hd = MEM_HEADS * MEM_HEAD_DIM
    _, rows, cols = w_in_stack.shape
    assert rows % (16 * batch) == 0 and (cols - _W1_COLS) % LANES == MLA_ROPE_DIM
    slab = rows // batch
    w2_cols = cols - _W1_COLS + MLA_ROPE_DIM
    return pl.pallas_call(
        _mem_kv_kernel,
        grid=(batch,),
        in_specs=[pl.BlockSpec((m_len, d), lambda b: (b, 0)),
                  _const_spec((1, d)), _const_spec((d, 2 * hd)), _const_spec((1, MEM_HEAD_DIM)),
                  pl.BlockSpec((None, slab, cols), lambda b: (layer, b, 0))],
        out_specs=[pl.BlockSpec((m_len, hd), lambda b: (b, 0)), pl.BlockSpec((m_len, 2 * hd), lambda b: (b, 0)),
                   pl.BlockSpec((slab, w2_cols), lambda b: (b, 0))],
        out_shape=[jax.ShapeDtypeStruct((batch * m_len, hd), BF16),
                   jax.ShapeDtypeStruct((batch * m_len, 2 * hd), BF16),
                   jax.ShapeDtypeStruct((rows, w2_cols), BF16)],
        scratch_shapes=[pltpu.VMEM((d, 2 * hd), BF16)],
        compiler_params=pltpu.CompilerParams(dimension_semantics=("arbitrary",),
                                             vmem_limit_bytes=VMEM_LIMIT_BYTES),
        name="mem_kv",
    )(mem2d, g, w, gk, w_in_stack)


_C_QA = (0, SWA_Q_HEADS * SWA_HEAD_DIM)
_C_KVA = (_C_QA[1], _C_QA[1] + 2 * SWA_KV_HEADS * SWA_HEAD_DIM)
_C_CQ = (_C_KVA[1], _C_KVA[1] + MLA_Q_RANK)
_C_CKV = (_C_CQ[1], _C_CQ[1] + MLA_KV_RANK)
_W1_COLS = _C_CKV[1]
_C_KR = (_W1_COLS, _W1_COLS + 2 * MLA_ROPE_DIM)
_C_QM = (_C_KR[1], _C_KR[1] + MEM_HEADS * MEM_HEAD_DIM)
assert _W1_COLS % LANES == 0 and _C_QA[1] % MXU_DIM == 0 and _C_KVA[1] - _C_KVA[0] == MXU_DIM


def _in_proj_kernel(x_ref, pos_ref, gattn_ref, sp_ref, w1_ref, w2_ref, wq_ref, wkv_ref, gcq_ref, gckv_ref,
                    bd_ref, wg32_ref, wu32_ref, wd32_ref, wo32_ref,
                    qa_ref, ka_ref, va_ref, krd_ref, qm_ref, qb_ref, kn_ref, vb_ref,
                    wg16_ref, wu16_ref, wd16_ref, wo16_ref):
    wg16_ref[...] = wg32_ref[...].astype(BF16)
    wu16_ref[...] = wu32_ref[...].astype(BF16)
    wd16_ref[...] = wd32_ref[...].astype(BF16)
    wo16_ref[...] = wo32_ref[...].astype(BF16)
    sp = sp_ref[...]
    row = lambda r, n=LANES: sp[r:r + 1, :n]
    x = x_ref[...]
    hn = (x * gattn_ref[...]).astype(BF16)
    inv_r2 = jnp.mean(x * x, axis=-1, keepdims=True) + EPS
    r_row = lax.rsqrt(inv_r2)
    eps_z = EPS * inv_r2

    def proj(cols):
        if cols[0] >= _W1_COLS:
            return _dot(hn, w2_ref[:, cols[0] - _W1_COLS:cols[1] - _W1_COLS])
        return _dot(hn, w1_ref[:, cols[0]:cols[1]])

    def rms_z(z, g):
        return z * lax.rsqrt(jnp.mean(z * z, axis=-1, keepdims=True) + eps_z) * g

    def head64_rms(blk, g):
        ss = _dot((blk * blk).astype(BF16), bd_ref[...])
        return blk * lax.rsqrt(ss * (1.0 / SWA_HEAD_DIM) + eps_z) * g

    ang = pos_ref[...].astype(F32) * row(_SP_FREQ)
    cos = jnp.cos(ang)
    sin_signed = jnp.sin(ang) * row(_SP_SIGN)

    def rope_dup(y):
        return y * cos + pltpu.roll(y, MLA_ROPE_DIM // 2, 1) * sin_signed

    cq = proj(_C_CQ)
    ckv = proj(_C_CKV)
    qm = proj(_C_QM)
    kva = proj(_C_KVA)
    kr = proj(_C_KR)
    cqn = rms_z(cq, gcq_ref[...]).astype(BF16)
    ckvn = rms_z(ckv, gckv_ref[...]).astype(BF16)

    for h in range(MEM_HEADS):
        sl = slice(h * MEM_HEAD_DIM, (h + 1) * MEM_HEAD_DIM)
        qm_ref[:, sl] = (rms_z(qm[:, sl], row(_SP_MEMQ)) * (LOG2E * MEM_HEAD_DIM ** -0.5)).astype(BF16)
    lo = lax.broadcasted_iota(jnp.int32, (1, LANES), 1) < SWA_HEAD_DIM
    for src, dst in ((head64_rms(kva, row(_SP_SWA_K, MXU_DIM))[:, :LANES], ka_ref),
                     (kva[:, LANES:] * r_row, va_ref)):
        swapped = pltpu.roll(src, SWA_HEAD_DIM, 1)
        dst[:, :LANES] = jnp.where(lo, src, swapped).astype(BF16)
        dst[:, LANES:] = jnp.where(lo, swapped, src).astype(BF16)
    krd_ref[...] = rope_dup(rms_z(kr, row(_SP_KR))).astype(BF16)

    qb = _dot(cqn, wq_ref[...])
    kvb = _dot(ckvn, wkv_ref[...])
    qa = proj(_C_QA)

    scale = LOG2E * (MLA_NOPE_DIM + MLA_ROPE_DIM) ** -0.5
    for h in range(MLA_HEADS):
        o = h * 2 * LANES
        qb_ref[:, o:o + LANES] = (_rms(qb[:, o:o + LANES], row(_SP_QN)) * scale).astype(BF16)
        qr = rope_dup(_rms(qb[:, o + LANES:o + 2 * LANES], row(_SP_QR)))
        qb_ref[:, o + LANES:o + 2 * LANES] = (qr * (0.5 * scale)).astype(BF16)
    hn_cols = MLA_HEADS * MLA_NOPE_DIM
    for h in range(MLA_HEADS):
        sl = slice(h * MLA_NOPE_DIM, (h + 1) * MLA_NOPE_DIM)
        kn_ref[:, sl] = _rms(kvb[:, sl], row(_SP_KN)).astype(BF16)
        vb_ref[:, 2 * h * LANES:(2 * h + 1) * LANES] = kvb[:, hn_cols + h * MLA_V_DIM:
                                                          hn_cols + (h + 1) * MLA_V_DIM].astype(BF16)
        vb_ref[:, (2 * h + 1) * LANES:(2 * h + 2) * LANES] = jnp.ones((kvb.shape[0], LANES), BF16)

    for c in range(_C_QA[1] // MXU_DIM):
        sl = slice(c * MXU_DIM, (c + 1) * MXU_DIM)
        qa_ref[:, sl] = (head64_rms(qa[:, sl], row(_SP_SWA_Q, MXU_DIM))
                         * (LOG2E * SWA_HEAD_DIM ** -0.5)).astype(BF16)


def _in_proj(x2d, pos_col, gattn, sp, w_in_stack, layer, w2, wq, wkv, gcq, gckv, bd, late_w, tm):
    t, d = x2d.shape
    steps = t // tm
    w1_spec = pl.BlockSpec((None, w_in_stack.shape[1], _W1_COLS), lambda i: (layer, 0, 0),
                           pipeline_mode=pl.Buffered(1))
    widths = (SWA_Q_HEADS * SWA_HEAD_DIM,
              SWA_KV_HEADS * LANES,
              SWA_KV_HEADS * LANES,
              LANES,
              MEM_HEADS * MEM_HEAD_DIM,
              MLA_HEADS * 2 * LANES,
              MLA_HEADS * MLA_NOPE_DIM,
              MLA_HEADS * 2 * LANES)
    rows = lambda w: pl.BlockSpec((tm, w), lambda i: (i, 0))
    slabs = []
    for w in late_w:
        assert w.shape[0] % (16 * steps) == 0, (w.shape, steps)
        slabs.append(pl.BlockSpec((w.shape[0] // steps, w.shape[1]), lambda i: (i, 0)))
    return pl.pallas_call(
        _in_proj_kernel,
        grid=(steps,),
        in_specs=[rows(d), rows(1), _const_spec(gattn.shape), _const_spec(sp.shape),
                  w1_spec, _const_spec(w2.shape), _const_spec(wq.shape),
                  _const_spec(wkv.shape),
                  _const_spec(gcq.shape), _const_spec(gckv.shape), _const_spec(bd.shape)] + slabs,
        out_specs=[rows(w) for w in widths] + slabs,
        out_shape=[jax.ShapeDtypeStruct((t, w), BF16) for w in widths]
        + [jax.ShapeDtypeStruct(w.shape, BF16) for w in late_w],
        compiler_params=pltpu.CompilerParams(dimension_semantics=("arbitrary",),
                                             vmem_limit_bytes=VMEM_LIMIT_BYTES),
        name="in_proj",
    )(x2d, pos_col, gattn, sp, w_in_stack, w2, wq, wkv, gcq, gckv, bd, *late_w)


_SWA_MASK_DIST = -NEG_INF * 2.0 ** 8


def _swa_tile(i, ss_ref, q_ref, kc_ref, kp_ref, vc_ref, vp_ref, pq_ref, pkc_ref, pkp_ref, o_ref, *, tq,
              filler=None):
    g = SWA_Q_HEADS // SWA_KV_HEADS
    kk = lax.broadcasted_iota(jnp.int32, (BLOCK, BLOCK), 0)
    qq = lax.broadcasted_iota(jnp.int32, (BLOCK, BLOCK), 1)
    from_prev = kk > qq
    prev_w = jnp.where(from_prev, 1.0, 0.0).astype(BF16)
    cur_w = jnp.where(from_prev, 0.0, 1.0).astype(BF16)
    lane_lo = lax.broadcasted_iota(jnp.int32, (BLOCK, LANES), 1) < SWA_HEAD_DIM
    lo_w = jnp.where(lane_lo, 1.0, 0.0).astype(BF16)
    hi_w = jnp.where(lane_lo, 0.0, 1.0).astype(BF16)

    for j in range(tq // BLOCK):
        if filler is not None:
            filler(j, tq // BLOCK)
        rows = slice(j * BLOCK, (j + 1) * BLOCK)
        prows = slice((j - 1) * BLOCK, j * BLOCK)
        k_prev, v_prev, pk_prev = ((kp_ref[...], vp_ref[...], pkp_ref[...]) if j == 0 else
                                   (kc_ref[prows, :], vc_ref[prows, :], pkc_ref[prows, :]))
        kj = jnp.concatenate([k_prev, kc_ref[rows, :]], axis=0)
        vj = jnp.concatenate([v_prev, vc_ref[rows, :]], axis=0)
        pq = pq_ref[:, rows]
        dist = jnp.where(from_prev, jnp.abs(pk_prev - pq), jnp.abs(pkc_ref[rows, :] - pq)).astype(F32)
        if j == 0:
            dist = jnp.where(from_prev, jnp.maximum(dist, jnp.where(i == 0, _SWA_MASK_DIST, 0.0)), dist)
        for kv in range(SWA_KV_HEADS):
            kvl = slice(kv * LANES, (kv + 1) * LANES)
            qm = jnp.concatenate(
                [q_ref[rows, (kv * g + hh) // 2 * LANES:((kv * g + hh) // 2 + 1) * LANES]
                 * (lo_w if hh % 2 == 0 else hi_w) for hh in range(g)], axis=0)
            st = _dot_nt(kj[:, kvl], qm)
            pts = []
            for hh in range(g):
                head = kv * g + hh
                cols = slice(hh * BLOCK, (hh + 1) * BLOCK)
                t = jnp.where(from_prev, st[:BLOCK, cols], st[BLOCK:, cols]) - ss_ref[0, head] * dist
                sink = ss_ref[1, head]
                m = jnp.maximum(jnp.max(t, axis=0, keepdims=True), sink)
                e = jnp.exp2(t - m)
                den = jnp.sum(e, axis=0, keepdims=True) + jnp.exp2(sink - m)
                p = (e * (1.0 / den)).astype(BF16)
                pts.append(jnp.concatenate([p * prev_w, p * cur_w], axis=0))
            pt = jnp.concatenate(pts, axis=1)
            o = lax.dot_general(pt, vj[:, kvl], (((0,), (0,)), ((), ())), preferred_element_type=F32)
            for pr in range(g // 2):
                grp = kv * (g // 2) + pr
                even = o[(2 * pr) * BLOCK:(2 * pr + 1) * BLOCK]
                odd = o[(2 * pr + 1) * BLOCK:(2 * pr + 2) * BLOCK]
                o_ref[rows, grp * LANES:(grp + 1) * LANES] = jnp.where(lane_lo, even, odd).astype(BF16)


def _mem_attn_tile(q_ref, k_ref, v_ref, o_ref):
    for h in range(MEM_HEADS):
        sl = slice(h * MEM_HEAD_DIM, (h + 1) * MEM_HEAD_DIM)
        s = _dot_nt(q_ref[:, sl], k_ref[:, sl])
        e = jnp.exp2(s - jnp.max(s, axis=-1, keepdims=True))
        o = _dot(e.astype(BF16), v_ref[:, 2 * h * LANES:(2 * h + 2) * LANES])
        o_ref[:, sl] = (o[:, :MEM_HEAD_DIM] / o[:, MEM_HEAD_DIM:]).astype(BF16)


def _swa_out_kernel(ss_ref, q_ref, kc_ref, kp_ref, vc_ref, vp_ref, pq_ref, pkc_ref, pkp_ref,
                    x_ref, yb_ref, qm_ref, km_ref, vm_ref, w_ref, h_ref, ya_sc, ym_sc, *, tq):
    na = ya_sc.shape[1]
    nb = na + yb_ref.shape[1]
    _mem_attn_tile(qm_ref, km_ref, vm_ref, ym_sc)
    d = h_ref.shape[1]

    def partial_projection(j, n):
        cols = slice(j * d // n, (j + 1) * d // n)
        h_ref[:, cols] = (x_ref[:, cols] + _dot(yb_ref[...], w_ref[na:nb, cols])
                          + _dot(ym_sc[...], w_ref[nb:, cols]))

    _swa_tile(pl.program_id(1), ss_ref, q_ref, kc_ref, kp_ref, vc_ref, vp_ref, pq_ref, pkc_ref, pkp_ref,
              ya_sc, tq=tq, filler=partial_projection)
    h_ref[...] += _dot(ya_sc[...], w_ref[0:na, :])


def _swa_out(slope_sink, qa, ka, va, pos_col, pos_row, x2d, yb, qm, km, vm, w_out, batch, seq, m_len, tq):
    t, d = x2d.shape
    nq = seq // tq
    nb = tq // BLOCK
    na = SWA_Q_HEADS * SWA_HEAD_DIM
    hd = MEM_HEADS * MEM_HEAD_DIM
    cur = lambda w: pl.BlockSpec((tq, w), lambda b, i: (b * nq + i, 0))
    prev = lambda w: pl.BlockSpec(
        (BLOCK, w), lambda b, i: (b * nq * nb + jnp.maximum(i * nb - 1, 0), 0))
    return pl.pallas_call(
        functools.partial(_swa_out_kernel, tq=tq),
        grid=(batch, nq),
        in_specs=[pl.BlockSpec(memory_space=pltpu.SMEM),
                  cur(na), cur(2 * LANES), prev(2 * LANES), cur(2 * LANES), prev(2 * LANES),
                  pl.BlockSpec((None, 1, tq), lambda b, i: (b, 0, i)),
                  cur(1), prev(1),
                  cur(d), cur(yb.shape[1]), cur(hd),
                  pl.BlockSpec((m_len, hd), lambda b, i: (b, 0)),
                  pl.BlockSpec((m_len, 2 * hd), lambda b, i: (b, 0)),
                  _const_spec(w_out.shape)],
        out_specs=cur(d),
        out_shape=jax.ShapeDtypeStruct((t, d), F32),
        scratch_shapes=[pltpu.VMEM((tq, na), BF16), pltpu.VMEM((tq, hd), BF16)],
        compiler_params=pltpu.CompilerParams(dimension_semantics=("arbitrary", "arbitrary"),
                                             vmem_limit_bytes=VMEM_LIMIT_BYTES),
        name="swa_out",
    )(slope_sink, qa, ka, ka, va, va, pos_row, pos_col, pos_col, x2d, yb, qm, km, vm, w_out)


def _mla_kernel(q_ref, kn_ref, kr_ref, v_ref, o_ref, s_sc, m_sc, acc_sc, *, tq, hps):
    qi = pl.program_id(2)
    qw = 2 * LANES
    tk = tq // 2

    all_heads = tuple(range(hps))

    def scores_to(slot, j, row0=0, heads=all_heads):
        k0 = pl.multiple_of(j * tk, tk)
        kr = kr_ref[pl.ds(k0, tk), :]
        for hh in heads:
            k = jnp.concatenate([kn_ref[pl.ds(k0, tk), hh * LANES:(hh + 1) * LANES], kr], axis=1)
            s_sc[slot, hh, row0:, :] = _dot_nt(q_ref[row0:, hh * qw:(hh + 1) * qw], k)

    def update_from(slot, j, row0=0, masked=False, heads=all_heads):
        k0 = pl.multiple_of(j * tk, tk)
        for hh in heads:
            for r0 in range(row0, tq, tk):
                rs = slice(r0, r0 + tk)
                s = s_sc[slot, hh, rs, :]
                if masked and r0 - row0 < tk:
                    r = lax.broadcasted_iota(jnp.int32, s.shape, 0) + (r0 - row0)
                    c = lax.broadcasted_iota(jnp.int32, s.shape, 1)
                    s = jnp.where(c <= r, s, NEG_INF)
                m = m_sc[hh, rs, :]
                m_new = jnp.maximum(m, jnp.max(s, axis=-1, keepdims=True))
                p = jnp.exp2(s - m_new)
                m_sc[hh, rs, :] = m_new
                acc_sc[hh, rs, :] = jnp.exp2(m - m_new) * acc_sc[hh, rs, :] + _dot(
                    p.astype(BF16), v_ref[pl.ds(k0, tk), hh * qw:(hh + 1) * qw])

    m_sc[...] = jnp.full(m_sc.shape, NEG_INF, F32)
    acc_sc[...] = jnp.zeros(acc_sc.shape, F32)

    scores_to(0, 0)

    def pair(p, carry):
        j = 2 * p
        for hh in all_heads:
            scores_to(1, j + 1, heads=(hh,))
            update_from(0, j, heads=(hh,))
        for hh in all_heads:
            scores_to(0, j + 2, heads=(hh,))
            update_from(1, j + 1, heads=(hh,))
        return carry

    lax.fori_loop(0, qi, pair, 0)
    for hh in all_heads:
        scores_to(1, 2 * qi + 1, row0=tk, heads=(hh,))
        update_from(0, 2 * qi, masked=True, heads=(hh,))
    update_from(1, 2 * qi + 1, row0=tk, masked=True)

    for hh in range(hps):
        acc = acc_sc[hh]
        o_ref[:, hh * MLA_V_DIM:(hh + 1) * MLA_V_DIM] = (acc[:, :MLA_V_DIM] / acc[:, MLA_V_DIM:]).astype(BF16)


def _mla_attn(qb, kn, krd, vb, batch, seq, tq, hps):
    t = batch * seq
    nq = seq // tq
    assert tq % 2 == 0 and seq % tq == 0
    return pl.pallas_call(
        functools.partial(_mla_kernel, tq=tq, hps=hps),
        grid=(batch, MLA_HEADS // hps, nq),
        in_specs=[pl.BlockSpec((tq, 2 * LANES * hps), lambda b, h, i: (b * nq + i, h)),
                  pl.BlockSpec((seq, MLA_NOPE_DIM * hps), lambda b, h, i: (b, h)),
                  pl.BlockSpec((seq, LANES), lambda b, h, i: (b, 0)),
                  pl.BlockSpec((seq, 2 * LANES * hps), lambda b, h, i: (b, h))],
        out_specs=pl.BlockSpec((tq, MLA_V_DIM * hps), lambda b, h, i: (b * nq + i, h)),
        out_shape=jax.ShapeDtypeStruct((t, MLA_HEADS * MLA_V_DIM), BF16),
        scratch_shapes=[pltpu.VMEM((2, hps, tq, tq // 2), F32), pltpu.VMEM((hps, tq, 1), F32),
                        pltpu.VMEM((hps, tq, 2 * LANES), F32)],
        compiler_params=pltpu.CompilerParams(dimension_semantics=("arbitrary",) * 3,
                                             vmem_limit_bytes=VMEM_LIMIT_BYTES),
        name="mla_attn",
    )(qb, kn, krd, vb)


def _ffn_kernel(h_ref, g_ref, wg_ref, wu_ref, wd_ref, o_ref, fn_ref):
    j = pl.program_id(1)

    @pl.when(j == 0)
    def _():
        h = h_ref[...]
        fn_ref[...] = _rms(h, g_ref[...]).astype(BF16)
        o_ref[...] = h

    fn = fn_ref[...]
    half = wg_ref.shape[1] // 2
    acts = []
    for c in range(2):
        cols = slice(c * half, (c + 1) * half)
        gate = _dot(fn, wg_ref[:, cols])
        up = _dot(fn, wu_ref[:, cols])
        acts.append((gate * jax.nn.sigmoid(gate) * up).astype(BF16))
    o_ref[...] += _dot(acts[0], wd_ref[:half, :]) + _dot(acts[1], wd_ref[half:, :])


def _ffn(h, g, wg, wu, wd, tm, tf):
    t, d = h.shape
    dff = wg.shape[1]
    return pl.pallas_call(
        _ffn_kernel,
        grid=(t // tm, dff // tf),
        in_specs=[pl.BlockSpec((tm, d), lambda i, j: (i, 0)),
                  _const_spec((1, d)),
                  pl.BlockSpec((d, tf), lambda i, j: (0, j)),
                  pl.BlockSpec((d, tf), lambda i, j: (0, j)),
                  pl.BlockSpec((tf, d), lambda i, j: (j, 0))],
        out_specs=pl.BlockSpec((tm, d), lambda i, j: (i, 0)),
        out_shape=jax.ShapeDtypeStruct((t, d), F32),
        scratch_shapes=[pltpu.VMEM((tm, d), BF16)],
        compiler_params=pltpu.CompilerParams(dimension_semantics=("arbitrary", "arbitrary"),
                                             vmem_limit_bytes=VMEM_LIMIT_BYTES),
        name="ffn",
    )(h, g, wg, wu, wd)


def _tile_row(v, width):
    v = v.astype(F32).reshape(-1)
    return jnp.tile(v, width // v.shape[0])


def _layer(h, mem2d, pos_col, pos_row, batch, seq, m_len, w_in_stack, layer, p):
    (attn_norm_g, swa_q_g, swa_k_g, swa_sinks, cq_g, ckv_g, w_uq, w_ukv, qn_g, qr_g, kn_g, kr_g,
     mem_g, w_mem_kv, memq_g, memk_g, w_out, ffn_g, w_gate, w_up, w_down) = p
    width = 2 * LANES
    inv_freq = ROPE_THETA ** (-jnp.arange(0, MLA_ROPE_DIM, 2, dtype=F32) / MLA_ROPE_DIM)
    sign = jnp.concatenate([-jnp.ones((MLA_ROPE_DIM // 2,), F32), jnp.ones((MLA_ROPE_DIM // 2,), F32)])
    rows = [swa_q_g, swa_k_g, qn_g, qr_g, kn_g, kr_g, memq_g, inv_freq, sign]
    sp = jnp.stack([_tile_row(r, width) for r in rows]
                   + [jnp.zeros((width,), F32)] * (_SP_ROWS - len(rows)))

    qd = MLA_NOPE_DIM + MLA_ROPE_DIM
    wq3 = w_uq.reshape(w_uq.shape[0], MLA_HEADS, qd)
    wq = jnp.concatenate([wq3, wq3[:, :, MLA_NOPE_DIM:]], axis=2).reshape(w_uq.shape[0], -1).astype(BF16)
    wkv3 = w_ukv.reshape(w_ukv.shape[0], MLA_HEADS, MLA_NOPE_DIM + MLA_V_DIM)
    wkv = jnp.concatenate([wkv3[:, :, :MLA_NOPE_DIM].reshape(w_ukv.shape[0], -1),
                           wkv3[:, :, MLA_NOPE_DIM:].reshape(w_ukv.shape[0], -1)], axis=1).astype(BF16)
    idx = jnp.arange(MXU_DIM) // SWA_HEAD_DIM
    bd = (idx[:, None] == idx[None, :]).astype(BF16)
    slopes = 2.0 ** (-8.0 * jnp.arange(1, SWA_Q_HEADS + 1, dtype=F32) / SWA_Q_HEADS)
    slope_sink = jnp.stack([slopes, swa_sinks.astype(F32)]) * LOG2E

    km, vm, w2 = _mem_kv(mem2d, mem_g.reshape(1, -1), w_mem_kv, memk_g.reshape(1, -1), w_in_stack,
                         layer, batch, m_len)
    qa, ka, va, krd, qm, qb, kn, vb, wg16, wu16, wd16, wo16 = _in_proj(
        h, pos_col, attn_norm_g.reshape(1, -1), sp, w_in_stack, layer, w2, wq, wkv, cq_g.reshape(1, -1),
        ckv_g.reshape(1, -1), bd, (w_gate, w_up, w_down, w_out), tm=IN_PROJ_ROWS)
    yb = _mla_attn(qb, kn, krd, vb, batch, seq, tq=MLA_Q_ROWS, hps=MLA_HEADS_PER_STEP)
    h = _swa_out(slope_sink, qa, ka, va, pos_col, pos_row, h, yb, qm, km, vm, wo16, batch, seq, m_len,
                 tq=ATTN_OUT_ROWS)
    return _ffn(h, ffn_g.reshape(1, -1), wg16, wu16, wd16, tm=FFN_ROWS, tf=FFN_COLS)


def kernel(x, mem, positions, attn_norm_g, w_in, swa_q_norm_g, swa_k_norm_g, swa_sinks, mla_cq_norm_g,
           mla_ckv_norm_g, w_uq, w_ukv, mla_qn_norm_g, mla_qr_norm_g, mla_kn_norm_g, mla_kr_norm_g,
           mem_norm_g, w_mem_kv, mem_q_norm_g, mem_k_norm_g, w_out, ffn_norm_g, w_gate, w_up, w_down):
    batch, seq, d = x.shape
    m_len = mem.shape[1]
    stacked = (attn_norm_g, swa_q_norm_g, swa_k_norm_g, swa_sinks, mla_cq_norm_g, mla_ckv_norm_g,
               w_uq, w_ukv, mla_qn_norm_g, mla_qr_norm_g, mla_kn_norm_g, mla_kr_norm_g, mem_norm_g,
               w_mem_kv, mem_q_norm_g, mem_k_norm_g, w_out, ffn_norm_g, w_gate, w_up, w_down)
    h = x.reshape(batch * seq, d)
    mem2d = mem.reshape(batch * m_len, d)
    pos_col = positions.reshape(batch * seq, 1)
    pos_row = positions.reshape(batch, 1, seq)
    w_in16 = w_in.astype(BF16)
    for l in range(attn_norm_g.shape[0]):
        h = _layer(h, mem2d, pos_col, pos_row, batch, seq, m_len, w_in16, l, tuple(a[l] for a in stacked))
    return h.reshape(batch, seq, d)
```

```python
import functools

import jax
import jax.numpy as jnp
from jax import lax
from jax.experimental import pallas as pl
from jax.experimental.pallas import tpu as pltpu

EPS = 1e-6
NEG_INF = -1e30
LOG2E = 1.4426950408889634
BLOCK = 128

SWA_Q_HEADS = 16
SWA_KV_HEADS = 2
SWA_HEAD_DIM = 64

MLA_HEADS = 4
MLA_Q_RANK = 512
MLA_KV_RANK = 512
MLA_NOPE_DIM = 128
MLA_ROPE_DIM = 64
MLA_V_DIM = 128
ROPE_THETA = 10000.0

MEM_HEADS = 4
MEM_HEAD_DIM = 128

LANES = 128
MXU_DIM = 256
VMEM_LIMIT_BYTES = 56 * 1024 * 1024

IN_PROJ_ROWS = 512
ATTN_OUT_ROWS = 512
MLA_Q_ROWS = 1024
MLA_HEADS_PER_STEP = 2
FFN_ROWS = 1024
FFN_COLS = 512

F32 = jnp.float32
BF16 = jnp.bfloat16

_SP_SWA_Q, _SP_SWA_K, _SP_QN, _SP_QR, _SP_KN, _SP_KR, _SP_MEMQ, _SP_FREQ, _SP_SIGN = range(9)
_SP_ROWS = 16


def _dot(a, b):
    return jnp.dot(a, b, preferred_element_type=F32)


def _dot_nt(a, b):
    return lax.dot_general(a, b, (((1,), (1,)), ((), ())), preferred_element_type=F32)


def _rms(x, g):
    return x * lax.rsqrt(jnp.mean(x * x, axis=-1, keepdims=True) + EPS) * g


def _const_spec(shape):
    nd = len(shape)
    return pl.BlockSpec(shape, lambda *_: (0,) * nd, pipeline_mode=pl.Buffered(1))


def _mem_kv_kernel(mem_ref, g_ref, w32_ref, gk_ref, win_ref, k_ref, v_ref, w2_ref, w_ref):
    @pl.when(pl.program_id(0) == 0)
    def _():
        w_ref[...] = w32_ref[...].astype(BF16)

    half = MLA_ROPE_DIM
    lo = lax.broadcasted_iota(jnp.int32, (1, LANES), 1) < half
    n_full = (win_ref.shape[1] - _W1_COLS) // LANES
    tail = [win_ref[:, _W1_COLS + k * LANES:_W1_COLS + (k + 1) * LANES].astype(F32) for k in range(n_full)]
    last = win_ref[:, _W1_COLS + n_full * LANES:].astype(F32)
    tail.append(jnp.concatenate([last, last], axis=1))
    swapped = [pltpu.roll(c, half, 1) for c in tail[:n_full]]
    w2_ref[:, :LANES] = jnp.where(lo, tail[0], swapped[0]).astype(BF16)
    for k in range(1, n_full + 1):
        nxt = swapped[k] if k < n_full else tail[k]
        w2_ref[:, k * LANES:(k + 1) * LANES] = jnp.where(lo, swapped[k - 1], nxt).astype(BF16)

    xn = _rms(mem_ref[...], g_ref[...]).astype(BF16)
    kv = _dot(xn, w_ref[...])
    hd = MEM_HEADS * MEM_HEAD_DIM
    for h in range(MEM_HEADS):
        sl = slice(h * MEM_HEAD_DIM, (h + 1) * MEM_HEAD_DIM)
        k_ref[:, sl] = _rms(kv[:, sl], gk_ref[...]).astype(BF16)
        v_ref[:, 2 * h * LANES:(2 * h + 1) * LANES] = kv[:, hd + h * MEM_HEAD_DIM:
                                                         hd + (h + 1) * MEM_HEAD_DIM].astype(BF16)
        v_ref[:, (2 * h + 1) * LANES:(2 * h + 2) * LANES] = jnp.ones((kv.shape[0], LANES), BF16)


def _mem_kv(mem2d, g, w, gk, w_in_stack, layer, batch, m_len):
    d = mem2d.shape[1]
    hd = MEM_HEADS * MEM_HEAD_DIM
    _, rows, cols = w_in_stack.shape
    assert rows % (16 * batch) == 0 and (cols - _W1_COLS) % LANES == MLA_ROPE_DIM
    slab = rows // batch
    w2_cols = cols - _W1_COLS + MLA_ROPE_DIM
    return pl.pallas_call(
        _mem_kv_kernel,
        grid=(batch,),
        in_specs=[pl.BlockSpec((m_len, d), lambda b: (b, 0)),
                  _const_spec((1, d)), _const_spec((d, 2 * hd)), _const_spec((1, MEM_HEAD_DIM)),
                  pl.BlockSpec((None, slab, cols), lambda b: (layer, b, 0))],
        out_specs=[pl.BlockSpec((m_len, hd), lambda b: (b, 0)), pl.BlockSpec((m_len, 2 * hd), lambda b: (b, 0)),
                   pl.BlockSpec((slab, w2_cols), lambda b: (b, 0))],
        out_shape=[jax.ShapeDtypeStruct((batch * m_len, hd), BF16),
                   jax.ShapeDtypeStruct((batch * m_len, 2 * hd), BF16),
                   jax.ShapeDtypeStruct((rows, w2_cols), BF16)],
        scratch_shapes=[pltpu.VMEM((d, 2 * hd), BF16)],
        compiler_params=pltpu.CompilerParams(dimension_semantics=("arbitrary",),
                                             vmem_limit_bytes=VMEM_LIMIT_BYTES),
        name="mem_kv",
    )(mem2d, g, w, gk, w_in_stack)


_C_QA = (0, SWA_Q_HEADS * SWA_HEAD_DIM)
_C_KVA = (_C_QA[1], _C_QA[1] + 2 * SWA_KV_HEADS * SWA_HEAD_DIM)
_C_CQ = (_C_KVA[1], _C_KVA[1] + MLA_Q_RANK)
_C_CKV = (_C_CQ[1], _C_CQ[1] + MLA_KV_RANK)
_W1_COLS = _C_CKV[1]
_C_KR = (_W1_COLS, _W1_COLS + 2 * MLA_ROPE_DIM)
_C_QM = (_C_KR[1], _C_KR[1] + MEM_HEADS * MEM_HEAD_DIM)
assert _W1_COLS % LANES == 0 and _C_QA[1] % MXU_DIM == 0 and _C_KVA[1] - _C_KVA[0] == MXU_DIM


def _in_proj_kernel(x_ref, pos_ref, gattn_ref, sp_ref, w1_ref, w2_ref, wq_ref, wkv_ref, gcq_ref, gckv_ref,
                    bd_ref, wg32_ref, wu32_ref, wd32_ref, wo32_ref,
                    qa_ref, ka_ref, va_ref, krd_ref, qm_ref, qb_ref, kn_ref, vb_ref,
                    wg16_ref, wu16_ref, wd16_ref, wo16_ref):
    wg16_ref[...] = wg32_ref[...].astype(BF16)
    wu16_ref[...] = wu32_ref[...].astype(BF16)
    wd16_ref[...] = wd32_ref[...].astype(BF16)
    wo16_ref[...] = wo32_ref[...].astype(BF16)
    sp = sp_ref[...]
    row = lambda r, n=LANES: sp[r:r + 1, :n]
    hn = _rms(x_ref[...], gattn_ref[...]).astype(BF16)

    def proj(cols):
        if cols[0] >= _W1_COLS:
            return _dot(hn, w2_ref[:, cols[0] - _W1_COLS:cols[1] - _W1_COLS])
        return _dot(hn, w1_ref[:, cols[0]:cols[1]])

    def head64_rms(blk, g):
        ss = _dot((blk * blk).astype(BF16), bd_ref[...])
        return blk * lax.rsqrt(ss * (1.0 / SWA_HEAD_DIM) + EPS) * g

    ang = pos_ref[...].astype(F32) * row(_SP_FREQ)
    cos = jnp.cos(ang)
    sin_signed = jnp.sin(ang) * row(_SP_SIGN)

    def rope_dup(y):
        return y * cos + pltpu.roll(y, MLA_ROPE_DIM // 2, 1) * sin_signed

    cq = proj(_C_CQ)
    ckv = proj(_C_CKV)
    qm = proj(_C_QM)
    kva = proj(_C_KVA)
    kr = proj(_C_KR)
    cqn = _rms(cq, gcq_ref[...]).astype(BF16)
    ckvn = _rms(ckv, gckv_ref[...]).astype(BF16)

    for h in range(MEM_HEADS):
        sl = slice(h * MEM_HEAD_DIM, (h + 1) * MEM_HEAD_DIM)
        qm_ref[:, sl] = (_rms(qm[:, sl], row(_SP_MEMQ)) * (LOG2E * MEM_HEAD_DIM ** -0.5)).astype(BF16)
    lo = lax.broadcasted_iota(jnp.int32, (1, LANES), 1) < SWA_HEAD_DIM
    for src, dst in ((head64_rms(kva, row(_SP_SWA_K, MXU_DIM))[:, :LANES], ka_ref), (kva[:, LANES:], va_ref)):
        swapped = pltpu.roll(src, SWA_HEAD_DIM, 1)
        dst[:, :LANES] = jnp.where(lo, src, swapped).astype(BF16)
        dst[:, LANES:] = jnp.where(lo, swapped, src).astype(BF16)
    krd_ref[...] = rope_dup(_rms(kr, row(_SP_KR))).astype(BF16)

    qb = _dot(cqn, wq_ref[...])
    kvb = _dot(ckvn, wkv_ref[...])
    qa = proj(_C_QA)

    scale = LOG2E * (MLA_NOPE_DIM + MLA_ROPE_DIM) ** -0.5
    for h in range(MLA_HEADS):
        o = h * 2 * LANES
        qb_ref[:, o:o + LANES] = (_rms(qb[:, o:o + LANES], row(_SP_QN)) * scale).astype(BF16)
        qr = rope_dup(_rms(qb[:, o + LANES:o + 2 * LANES], row(_SP_QR)))
        qb_ref[:, o + LANES:o + 2 * LANES] = (qr * (0.5 * scale)).astype(BF16)
    hn_cols = MLA_HEADS * MLA_NOPE_DIM
    for h in range(MLA_HEADS):
        sl = slice(h * MLA_NOPE_DIM, (h + 1) * MLA_NOPE_DIM)
        kn_ref[:, sl] = _rms(kvb[:, sl], row(_SP_KN)).astype(BF16)
        vb_ref[:, 2 * h * LANES:(2 * h + 1) * LANES] = kvb[:, hn_cols + h * MLA_V_DIM:
                                                          hn_cols + (h + 1) * MLA_V_DIM].astype(BF16)
        vb_ref[:, (2 * h + 1) * LANES:(2 * h + 2) * LANES] = jnp.ones((kvb.shape[0], LANES), BF16)

    for c in range(_C_QA[1] // MXU_DIM):
        sl = slice(c * MXU_DIM, (c + 1) * MXU_DIM)
        qa_ref[:, sl] = (head64_rms(qa[:, sl], row(_SP_SWA_Q, MXU_DIM))
                         * (LOG2E * SWA_HEAD_DIM ** -0.5)).astype(BF16)


def _in_proj(x2d, pos_col, gattn, sp, w_in_stack, layer, w2, wq, wkv, gcq, gckv, bd, late_w, tm):
    t, d = x2d.shape
    steps = t // tm
    w1_spec = pl.BlockSpec((None, w_in_stack.shape[1], _W1_COLS), lambda i: (layer, 0, 0),
                           pipeline_mode=pl.Buffered(1))
    widths = (SWA_Q_HEADS * SWA_HEAD_DIM,
              SWA_KV_HEADS * LANES,
              SWA_KV_HEADS * LANES,
              LANES,
              MEM_HEADS * MEM_HEAD_DIM,
              MLA_HEADS * 2 * LANES,
              MLA_HEADS * MLA_NOPE_DIM,
              MLA_HEADS * 2 * LANES)
    rows = lambda w: pl.BlockSpec((tm, w), lambda i: (i, 0))
    slabs = []
    for w in late_w:
        assert w.shape[0] % (16 * steps) == 0, (w.shape, steps)
        slabs.append(pl.BlockSpec((w.shape[0] // steps, w.shape[1]), lambda i: (i, 0)))
    return pl.pallas_call(
        _in_proj_kernel,
        grid=(steps,),
        in_specs=[rows(d), rows(1), _const_spec(gattn.shape), _const_spec(sp.shape),
                  w1_spec, _const_spec(w2.shape), _const_spec(wq.shape),
                  _const_spec(wkv.shape),
                  _const_spec(gcq.shape), _const_spec(gckv.shape), _const_spec(bd.shape)] + slabs,
        out_specs=[rows(w) for w in widths] + slabs,
        out_shape=[jax.ShapeDtypeStruct((t, w), BF16) for w in widths]
        + [jax.ShapeDtypeStruct(w.shape, BF16) for w in late_w],
        compiler_params=pltpu.CompilerParams(dimension_semantics=("arbitrary",),
                                             vmem_limit_bytes=VMEM_LIMIT_BYTES),
        name="in_proj",
    )(x2d, pos_col, gattn, sp, w_in_stack, w2, wq, wkv, gcq, gckv, bd, *late_w)


_SWA_MASK_DIST = -NEG_INF * 2.0 ** 8


def _swa_tile(i, ss_ref, q_ref, kc_ref, kp_ref, vc_ref, vp_ref, pq_ref, pkc_ref, pkp_ref, o_ref, *, tq,
              filler=None):
    g = SWA_Q_HEADS // SWA_KV_HEADS
    kk = lax.broadcasted_iota(jnp.int32, (BLOCK, BLOCK), 0)
    qq = lax.broadcasted_iota(jnp.int32, (BLOCK, BLOCK), 1)
    from_prev = kk > qq
    prev_w = jnp.where(from_prev, 1.0, 0.0).astype(BF16)
    cur_w = jnp.where(from_prev, 0.0, 1.0).astype(BF16)
    lane_lo = lax.broadcasted_iota(jnp.int32, (BLOCK, LANES), 1) < SWA_HEAD_DIM
    lo_w = jnp.where(lane_lo, 1.0, 0.0).astype(BF16)
    hi_w = jnp.where(lane_lo, 0.0, 1.0).astype(BF16)

    for j in range(tq // BLOCK):
        if filler is not None:
            filler(j, tq // BLOCK)
        rows = slice(j * BLOCK, (j + 1) * BLOCK)
        prows = slice((j - 1) * BLOCK, j * BLOCK)
        k_prev, v_prev, pk_prev = ((kp_ref[...], vp_ref[...], pkp_ref[...]) if j == 0 else
                                   (kc_ref[prows, :], vc_ref[prows, :], pkc_ref[prows, :]))
        kj = jnp.concatenate([k_prev, kc_ref[rows, :]], axis=0)
        vj = jnp.concatenate([v_prev, vc_ref[rows, :]], axis=0)
        pq = pq_ref[:, rows]
        dist = jnp.where(from_prev, jnp.abs(pk_prev - pq), jnp.abs(pkc_ref[rows, :] - pq)).astype(F32)
        if j == 0:
            dist = jnp.where(from_prev, jnp.maximum(dist, jnp.where(i == 0, _SWA_MASK_DIST, 0.0)), dist)
        for kv in range(SWA_KV_HEADS):
            kvl = slice(kv * LANES, (kv + 1) * LANES)
            qm = jnp.concatenate(
                [q_ref[rows, (kv * g + hh) // 2 * LANES:((kv * g + hh) // 2 + 1) * LANES]
                 * (lo_w if hh % 2 == 0 else hi_w) for hh in range(g)], axis=0)
            st = _dot_nt(kj[:, kvl], qm)
            pts = []
            for hh in range(g):
                head = kv * g + hh
                cols = slice(hh * BLOCK, (hh + 1) * BLOCK)
                t = jnp.where(from_prev, st[:BLOCK, cols], st[BLOCK:, cols]) - ss_ref[0, head] * dist
                sink = ss_ref[1, head]
                m = jnp.maximum(jnp.max(t, axis=0, keepdims=True), sink)
                e = jnp.exp2(t - m)
                den = jnp.sum(e, axis=0, keepdims=True) + jnp.exp2(sink - m)
                p = (e * (1.0 / den)).astype(BF16)
                pts.append(jnp.concatenate([p * prev_w, p * cur_w], axis=0))
            pt = jnp.concatenate(pts, axis=1)
            o = lax.dot_general(pt, vj[:, kvl], (((0,), (0,)), ((), ())), preferred_element_type=F32)
            for pr in range(g // 2):
                grp = kv * (g // 2) + pr
                even = o[(2 * pr) * BLOCK:(2 * pr + 1) * BLOCK]
                odd = o[(2 * pr + 1) * BLOCK:(2 * pr + 2) * BLOCK]
                o_ref[rows, grp * LANES:(grp + 1) * LANES] = jnp.where(lane_lo, even, odd).astype(BF16)


def _mem_attn_tile(q_ref, k_ref, v_ref, o_ref):
    for h in range(MEM_HEADS):
        sl = slice(h * MEM_HEAD_DIM, (h + 1) * MEM_HEAD_DIM)
        s = _dot_nt(q_ref[:, sl], k_ref[:, sl])
        e = jnp.exp2(s - jnp.max(s, axis=-1, keepdims=True))
        o = _dot(e.astype(BF16), v_ref[:, 2 * h * LANES:(2 * h + 2) * LANES])
        o_ref[:, sl] = (o[:, :MEM_HEAD_DIM] / o[:, MEM_HEAD_DIM:]).astype(BF16)


def _swa_out_kernel(ss_ref, q_ref, kc_ref, kp_ref, vc_ref, vp_ref, pq_ref, pkc_ref, pkp_ref,
                    x_ref, yb_ref, qm_ref, km_ref, vm_ref, w_ref, h_ref, ya_sc, ym_sc, *, tq):
    na = ya_sc.shape[1]
    nb = na + yb_ref.shape[1]
    _mem_attn_tile(qm_ref, km_ref, vm_ref, ym_sc)
    d = h_ref.shape[1]

    def partial_projection(j, n):
        cols = slice(j * d // n, (j + 1) * d // n)
        h_ref[:, cols] = (x_ref[:, cols] + _dot(yb_ref[...], w_ref[na:nb, cols])
                          + _dot(ym_sc[...], w_ref[nb:, cols]))

    _swa_tile(pl.program_id(1), ss_ref, q_ref, kc_ref, kp_ref, vc_ref, vp_ref, pq_ref, pkc_ref, pkp_ref,
              ya_sc, tq=tq, filler=partial_projection)
    h_ref[...] += _dot(ya_sc[...], w_ref[0:na, :])


def _swa_out(slope_sink, qa, ka, va, pos_col, pos_row, x2d, yb, qm, km, vm, w_out, batch, seq, m_len, tq):
    t, d = x2d.shape
    nq = seq // tq
    nb = tq // BLOCK
    na = SWA_Q_HEADS * SWA_HEAD_DIM
    hd = MEM_HEADS * MEM_HEAD_DIM
    cur = lambda w: pl.BlockSpec((tq, w), lambda b, i: (b * nq + i, 0))
    prev = lambda w: pl.BlockSpec(
        (BLOCK, w), lambda b, i: (b * nq * nb + jnp.maximum(i * nb - 1, 0), 0))
    return pl.pallas_call(
        functools.partial(_swa_out_kernel, tq=tq),
        grid=(batch, nq),
        in_specs=[pl.BlockSpec(memory_space=pltpu.SMEM),
                  cur(na), cur(2 * LANES), prev(2 * LANES), cur(2 * LANES), prev(2 * LANES),
                  pl.BlockSpec((None, 1, tq), lambda b, i: (b, 0, i)),
                  cur(1), prev(1),
                  cur(d), cur(yb.shape[1]), cur(hd),
                  pl.BlockSpec((m_len, hd), lambda b, i: (b, 0)),
                  pl.BlockSpec((m_len, 2 * hd), lambda b, i: (b, 0)),
                  _const_spec(w_out.shape)],
        out_specs=cur(d),
        out_shape=jax.ShapeDtypeStruct((t, d), F32),
        scratch_shapes=[pltpu.VMEM((tq, na), BF16), pltpu.VMEM((tq, hd), BF16)],
        compiler_params=pltpu.CompilerParams(dimension_semantics=("arbitrary", "arbitrary"),
                                             vmem_limit_bytes=VMEM_LIMIT_BYTES),
        name="swa_out",
    )(slope_sink, qa, ka, ka, va, va, pos_row, pos_col, pos_col, x2d, yb, qm, km, vm, w_out)


def _mla_kernel(q_ref, kn_ref, kr_ref, v_ref, o_ref, s_sc, m_sc, acc_sc, *, tq, hps):
    qi = pl.program_id(2)
    qw = 2 * LANES
    tk = tq // 2

    all_heads = tuple(range(hps))

    def scores_to(slot, j, row0=0, heads=all_heads):
        k0 = pl.multiple_of(j * tk, tk)
        kr = kr_ref[pl.ds(k0, tk), :]
        for hh in heads:
            k = jnp.concatenate([kn_ref[pl.ds(k0, tk), hh * LANES:(hh + 1) * LANES], kr], axis=1)
            s_sc[slot, hh, row0:, :] = _dot_nt(q_ref[row0:, hh * qw:(hh + 1) * qw], k)

    def update_from(slot, j, row0=0, masked=False, heads=all_heads):
        k0 = pl.multiple_of(j * tk, tk)
        for hh in heads:
            for r0 in range(row0, tq, tk):
                rs = slice(r0, r0 + tk)
                s = s_sc[slot, hh, rs, :]
                if masked and r0 - row0 < tk:
                    r = lax.broadcasted_iota(jnp.int32, s.shape, 0) + (r0 - row0)
                    c = lax.broadcasted_iota(jnp.int32, s.shape, 1)
                    s = jnp.where(c <= r, s, NEG_INF)
                m = m_sc[hh, rs, :]
                m_new = jnp.maximum(m, jnp.max(s, axis=-1, keepdims=True))
                p = jnp.exp2(s - m_new)
                m_sc[hh, rs, :] = m_new
                acc_sc[hh, rs, :] = jnp.exp2(m - m_new) * acc_sc[hh, rs, :] + _dot(
                    p.astype(BF16), v_ref[pl.ds(k0, tk), hh * qw:(hh + 1) * qw])

    m_sc[...] = jnp.full(m_sc.shape, NEG_INF, F32)
    acc_sc[...] = jnp.zeros(acc_sc.shape, F32)

    scores_to(0, 0)

    def pair(p, carry):
        j = 2 * p
        for hh in all_heads:
            scores_to(1, j + 1, heads=(hh,))
            update_from(0, j, heads=(hh,))
        for hh in all_heads:
            scores_to(0, j + 2, heads=(hh,))
            update_from(1, j + 1, heads=(hh,))
        return carry

    lax.fori_loop(0, qi, pair, 0)
    for hh in all_heads:
        scores_to(1, 2 * qi + 1, row0=tk, heads=(hh,))
        update_from(0, 2 * qi, masked=True, heads=(hh,))
    update_from(1, 2 * qi + 1, row0=tk, masked=True)

    for hh in range(hps):
        acc = acc_sc[hh]
        o_ref[:, hh * MLA_V_DIM:(hh + 1) * MLA_V_DIM] = (acc[:, :MLA_V_DIM] / acc[:, MLA_V_DIM:]).astype(BF16)


def _mla_attn(qb, kn, krd, vb, batch, seq, tq, hps):
    t = batch * seq
    nq = seq // tq
    assert tq % 2 == 0 and seq % tq == 0
    return pl.pallas_call(
        functools.partial(_mla_kernel, tq=tq, hps=hps),
        grid=(batch, MLA_HEADS // hps, nq),
        in_specs=[pl.BlockSpec((tq, 2 * LANES * hps), lambda b, h, i: (b * nq + i, h)),
                  pl.BlockSpec((seq, MLA_NOPE_DIM * hps), lambda b, h, i: (b, h)),
                  pl.BlockSpec((seq, LANES), lambda b, h, i: (b, 0)),
                  pl.BlockSpec((seq, 2 * LANES * hps), lambda b, h, i: (b, h))],
        out_specs=pl.BlockSpec((tq, MLA_V_DIM * hps), lambda b, h, i: (b * nq + i, h)),
        out_shape=jax.ShapeDtypeStruct((t, MLA_HEADS * MLA_V_DIM), BF16),
        scratch_shapes=[pltpu.VMEM((2, hps, tq, tq // 2), F32), pltpu.VMEM((hps, tq, 1), F32),
                        pltpu.VMEM((hps, tq, 2 * LANES), F32)],
        compiler_params=pltpu.CompilerParams(dimension_semantics=("arbitrary",) * 3,
                                             vmem_limit_bytes=VMEM_LIMIT_BYTES),
        name="mla_attn",
    )(qb, kn, krd, vb)


def _ffn_kernel(h_ref, g_ref, wg_ref, wu_ref, wd_ref, o_ref, fn_ref):
    j = pl.program_id(1)

    @pl.when(j == 0)
    def _():
        h = h_ref[...]
        fn_ref[...] = _rms(h, g_ref[...]).astype(BF16)
        o_ref[...] = h

    fn = fn_ref[...]
    half = wg_ref.shape[1] // 2
    acts = []
    for c in range(2):
        cols = slice(c * half, (c + 1) * half)
        gate = _dot(fn, wg_ref[:, cols])
        up = _dot(fn, wu_ref[:, cols])
        acts.append((gate * jax.nn.sigmoid(gate) * up).astype(BF16))
    o_ref[...] += _dot(acts[0], wd_ref[:half, :]) + _dot(acts[1], wd_ref[half:, :])


def _ffn(h, g, wg, wu, wd, tm, tf):
    t, d = h.shape
    dff = wg.shape[1]
    return pl.pallas_call(
        _ffn_kernel,
        grid=(t // tm, dff // tf),
        in_specs=[pl.BlockSpec((tm, d), lambda i, j: (i, 0)),
                  _const_spec((1, d)),
                  pl.BlockSpec((d, tf), lambda i, j: (0, j)),
                  pl.BlockSpec((d, tf), lambda i, j: (0, j)),
                  pl.BlockSpec((tf, d), lambda i, j: (j, 0))],
        out_specs=pl.BlockSpec((tm, d), lambda i, j: (i, 0)),
        out_shape=jax.ShapeDtypeStruct((t, d), F32),
        scratch_shapes=[pltpu.VMEM((tm, d), BF16)],
        compiler_params=pltpu.CompilerParams(dimension_semantics=("arbitrary", "arbitrary"),
                                             vmem_limit_bytes=VMEM_LIMIT_BYTES),
        name="ffn",
    )(h, g, wg, wu, wd)


def _tile_row(v, width):
    v = v.astype(F32).reshape(-1)
    return jnp.tile(v, width // v.shape[0])


def _layer(h, mem2d, pos_col, pos_row, batch, seq, m_len, w_in_stack, layer, p):
    (attn_norm_g, swa_q_g, swa_k_g, swa_sinks, cq_g, ckv_g, w_uq, w_ukv, qn_g, qr_g, kn_g, kr_g,
     mem_g, w_mem_kv, memq_g, memk_g, w_out, ffn_g, w_gate, w_up, w_down) = p
    width = 2 * LANES
    inv_freq = ROPE_THETA ** (-jnp.arange(0, MLA_ROPE_DIM, 2, dtype=F32) / MLA_ROPE_DIM)
    sign = jnp.concatenate([-jnp.ones((MLA_ROPE_DIM // 2,), F32), jnp.ones((MLA_ROPE_DIM // 2,), F32)])
    rows = [swa_q_g, swa_k_g, qn_g, qr_g, kn_g, kr_g, memq_g, inv_freq, sign]
    sp = jnp.stack([_tile_row(r, width) for r in rows]
                   + [jnp.zeros((width,), F32)] * (_SP_ROWS - len(rows)))

    qd = MLA_NOPE_DIM + MLA_ROPE_DIM
    wq3 = w_uq.reshape(w_uq.shape[0], MLA_HEADS, qd)
    wq = jnp.concatenate([wq3, wq3[:, :, MLA_NOPE_DIM:]], axis=2).reshape(w_uq.shape[0], -1).astype(BF16)
    wkv3 = w_ukv.reshape(w_ukv.shape[0], MLA_HEADS, MLA_NOPE_DIM + MLA_V_DIM)
    wkv = jnp.concatenate([wkv3[:, :, :MLA_NOPE_DIM].reshape(w_ukv.shape[0], -1),
                           wkv3[:, :, MLA_NOPE_DIM:].reshape(w_ukv.shape[0], -1)], axis=1).astype(BF16)
    idx = jnp.arange(MXU_DIM) // SWA_HEAD_DIM
    bd = (idx[:, None] == idx[None, :]).astype(BF16)
    slopes = 2.0 ** (-8.0 * jnp.arange(1, SWA_Q_HEADS + 1, dtype=F32) / SWA_Q_HEADS)
    slope_sink = jnp.stack([slopes, swa_sinks.astype(F32)]) * LOG2E

    km, vm, w2 = _mem_kv(mem2d, mem_g.reshape(1, -1), w_mem_kv, memk_g.reshape(1, -1), w_in_stack,
                         layer, batch, m_len)
    qa, ka, va, krd, qm, qb, kn, vb, wg16, wu16, wd16, wo16 = _in_proj(
        h, pos_col, attn_norm_g.reshape(1, -1), sp, w_in_stack, layer, w2, wq, wkv, cq_g.reshape(1, -1),
        ckv_g.reshape(1, -1), bd, (w_gate, w_up, w_down, w_out), tm=IN_PROJ_ROWS)
    yb = _mla_attn(qb, kn, krd, vb, batch, seq, tq=MLA_Q_ROWS, hps=MLA_HEADS_PER_STEP)
    h = _swa_out(slope_sink, qa, ka, va, pos_col, pos_row, h, yb, qm, km, vm, wo16, batch, seq, m_len,
                 tq=ATTN_OUT_ROWS)
    return _ffn(h, ffn_g.reshape(1, -1), wg16, wu16, wd16, tm=FFN_ROWS, tf=FFN_COLS)


def kernel(x, mem, positions, attn_norm_g, w_in, swa_q_norm_g, swa_k_norm_g, swa_sinks, mla_cq_norm_g,
           mla_ckv_norm_g, w_uq, w_ukv, mla_qn_norm_g, mla_qr_norm_g, mla_kn_norm_g, mla_kr_norm_g,
           mem_norm_g, w_mem_kv, mem_q_norm_g, mem_k_norm_g, w_out, ffn_norm_g, w_gate, w_up, w_down):
    batch, seq, d = x.shape
    m_len = mem.shape[1]
    stacked = (attn_norm_g, swa_q_norm_g, swa_k_norm_g, swa_sinks, mla_cq_norm_g, mla_ckv_norm_g,
               w_uq, w_ukv, mla_qn_norm_g, mla_qr_norm_g, mla_kn_norm_g, mla_kr_norm_g, mem_norm_g,
               w_mem_kv, mem_q_norm_g, mem_k_norm_g, w_out, ffn_norm_g, w_gate, w_up, w_down)
    h = x.reshape(batch * seq, d)
    mem2d = mem.reshape(batch * m_len, d)
    pos_col = positions.reshape(batch * seq, 1)
    pos_row = positions.reshape(batch, 1, seq)
    w_in16 = w_in.astype(BF16)
    for l in range(attn_norm_g.shape[0]):
        h = _layer(h, mem2d, pos_col, pos_row, batch, seq, m_len, w_in16, l, tuple(a[l] for a in stacked))
    return h.reshape(batch, seq, d)
```

```python
import functools

import jax
import jax.numpy as jnp
from jax import lax
from jax.experimental import pallas as pl
from jax.experimental.pallas import tpu as pltpu

EPS = 1e-6
NEG_INF = -1e30
LOG2E = 1.4426950408889634
BLOCK = 128

SWA_Q_HEADS = 16
SWA_KV_HEADS = 2
SWA_HEAD_DIM = 64

MLA_HEADS = 4
MLA_Q_RANK = 512
MLA_KV_RANK = 512
MLA_NOPE_DIM = 128
MLA_ROPE_DIM = 64
MLA_V_DIM = 128
ROPE_THETA = 10000.0

MEM_HEADS = 4
MEM_HEAD_DIM = 128

LANES = 128
MXU_DIM = 256
VMEM_LIMIT_BYTES = 56 * 1024 * 1024

IN_PROJ_ROWS = 512
ATTN_OUT_ROWS = 512
MLA_Q_ROWS = 1024
MLA_HEADS_PER_STEP = 2
FFN_ROWS = 1024
FFN_COLS = 512

F32 = jnp.float32
BF16 = jnp.bfloat16

_SP_SWA_Q, _SP_SWA_K, _SP_QN, _SP_QR, _SP_KN, _SP_KR, _SP_MEMQ, _SP_FREQ, _SP_SIGN = range(9)
_SP_ROWS = 16


def _dot(a, b):
    return jnp.dot(a, b, preferred_element_type=F32)


def _dot_nt(a, b):
    return lax.dot_general(a, b, (((1,), (1,)), ((), ())), preferred_element_type=F32)


def _rms(x, g):
    return x * lax.rsqrt(jnp.mean(x * x, axis=-1, keepdims=True) + EPS) * g


def _const_spec(shape):
    nd = len(shape)
    return pl.BlockSpec(shape, lambda *_: (0,) * nd, pipeline_mode=pl.Buffered(1))


def _mem_kv_kernel(mem_ref, g_ref, w32_ref, gk_ref, win_ref, k_ref, v_ref, w2_ref, w_ref):
    @pl.when(pl.program_id(0) == 0)
    def _():
        w_ref[...] = w32_ref[...].astype(BF16)

    half = MLA_ROPE_DIM
    lo = lax.broadcasted_iota(jnp.int32, (1, LANES), 1) < half
    n_full = (win_ref.shape[1] - _W1_COLS) // LANES
    tail = [win_ref[:, _W1_COLS + k * LANES:_W1_COLS + (k + 1) * LANES].astype(F32) for k in range(n_full)]
    last = win_ref[:, _W1_COLS + n_full * LANES:].astype(F32)
    tail.append(jnp.concatenate([last, last], axis=1))
    swapped = [pltpu.roll(c, half, 1) for c in tail[:n_full]]
    w2_ref[:, :LANES] = jnp.where(lo, tail[0], swapped[0]).astype(BF16)
    for k in range(1, n_full + 1):
        nxt = swapped[k] if k < n_full else tail[k]
        w2_ref[:, k * LANES:(k + 1) * LANES] = jnp.where(lo, swapped[k - 1], nxt).astype(BF16)

    xn = _rms(mem_ref[...], g_ref[...]).astype(BF16)
    kv = _dot(xn, w_ref[...])
    hd = MEM_HEADS * MEM_HEAD_DIM
    for h in range(MEM_HEADS):
        sl = slice(h * MEM_HEAD_DIM, (h + 1) * MEM_HEAD_DIM)
        k_ref[:, sl] = _rms(kv[:, sl], gk_ref[...]).astype(BF16)
        v_ref[:, 2 * h * LANES:(2 * h + 1) * LANES] = kv[:, hd + h * MEM_HEAD_DIM:
                                                         hd + (h + 1) * MEM_HEAD_DIM].astype(BF16)
        v_ref[:, (2 * h + 1) * LANES:(2 * h + 2) * LANES] = jnp.ones((kv.shape[0], LANES), BF16)


def _mem_kv(mem2d, g, w, gk, w_in_stack, layer, batch, m_len):
    d = mem2d.shape[1]
    hd = MEM_HEADS * MEM_HEAD_DIM
    _, rows, cols = w_in_stack.shape
    assert rows % (16 * batch) == 0 and (cols - _W1_COLS) % LANES == MLA_ROPE_DIM
    slab = rows // batch
    w2_cols = cols - _W1_COLS + MLA_ROPE_DIM
    return pl.pallas_call(
        _mem_kv_kernel,
        grid=(batch,),
        in_specs=[pl.BlockSpec((m_len, d), lambda b: (b, 0)),
                  _const_spec((1, d)), _const_spec((d, 2 * hd)), _const_spec((1, MEM_HEAD_DIM)),
                  pl.BlockSpec((None, slab, cols), lambda b: (layer, b, 0))],
        out_specs=[pl.BlockSpec((m_len, hd), lambda b: (b, 0)), pl.BlockSpec((m_len, 2 * hd), lambda b: (b, 0)),
                   pl.BlockSpec((slab, w2_cols), lambda b: (b, 0))],
        out_shape=[jax.ShapeDtypeStruct((batch * m_len, hd), BF16),
                   jax.ShapeDtypeStruct((batch * m_len, 2 * hd), BF16),
                   jax.ShapeDtypeStruct((rows, w2_cols), BF16)],
        scratch_shapes=[pltpu.VMEM((d, 2 * hd), BF16)],
        compiler_params=pltpu.CompilerParams(dimension_semantics=("arbitrary",),
                                             vmem_limit_bytes=VMEM_LIMIT_BYTES),
        name="mem_kv",
    )(mem2d, g, w, gk, w_in_stack)


_C_QA = (0, SWA_Q_HEADS * SWA_HEAD_DIM)
_C_KVA = (_C_QA[1], _C_QA[1] + 2 * SWA_KV_HEADS * SWA_HEAD_DIM)
_C_CQ = (_C_KVA[1], _C_KVA[1] + MLA_Q_RANK)
_C_CKV = (_C_CQ[1], _C_CQ[1] + MLA_KV_RANK)
_W1_COLS = _C_CKV[1]
_C_KR = (_W1_COLS, _W1_COLS + 2 * MLA_ROPE_DIM)
_C_QM = (_C_KR[1], _C_KR[1] + MEM_HEADS * MEM_HEAD_DIM)
assert _W1_COLS % LANES == 0 and _C_QA[1] % MXU_DIM == 0 and _C_KVA[1] - _C_KVA[0] == MXU_DIM


def _in_proj_kernel(x_ref, pos_ref, gattn_ref, sp_ref, w1_ref, w2_ref, wq_ref, wkv_ref, gcq_ref, gckv_ref,
                    bd_ref, wg32_ref, wu32_ref, wd32_ref, wo32_ref,
                    qa_ref, ka_ref, va_ref, krd_ref, qm_ref, qb_ref, kn_ref, vb_ref,
                    wg16_ref, wu16_ref, wd16_ref, wo16_ref):
    wg16_ref[...] = wg32_ref[...].astype(BF16)
    wu16_ref[...] = wu32_ref[...].astype(BF16)
    wd16_ref[...] = wd32_ref[...].astype(BF16)
    wo16_ref[...] = wo32_ref[...].astype(BF16)
    sp = sp_ref[...]
    row = lambda r, n=LANES: sp[r:r + 1, :n]
    hn = _rms(x_ref[...], gattn_ref[...]).astype(BF16)

    def proj(cols):
        if cols[0] >= _W1_COLS:
            return _dot(hn, w2_ref[:, cols[0] - _W1_COLS:cols[1] - _W1_COLS])
        return _dot(hn, w1_ref[:, cols[0]:cols[1]])

    def head64_rms(blk, g):
        ss = _dot((blk * blk).astype(BF16), bd_ref[...])
        return blk * lax.rsqrt(ss * (1.0 / SWA_HEAD_DIM) + EPS) * g

    ang = pos_ref[...].astype(F32) * row(_SP_FREQ)
    cos = jnp.cos(ang)
    sin_signed = jnp.sin(ang) * row(_SP_SIGN)

    def rope_dup(y):
        return y * cos + pltpu.roll(y, MLA_ROPE_DIM // 2, 1) * sin_signed

    cq = proj(_C_CQ)
    ckv = proj(_C_CKV)
    qm = proj(_C_QM)
    kva = proj(_C_KVA)
    kr = proj(_C_KR)
    cqn = _rms(cq, gcq_ref[...]).astype(BF16)
    ckvn = _rms(ckv, gckv_ref[...]).astype(BF16)

    for h in range(MEM_HEADS):
        sl = slice(h * MEM_HEAD_DIM, (h + 1) * MEM_HEAD_DIM)
        qm_ref[:, sl] = (_rms(qm[:, sl], row(_SP_MEMQ)) * (LOG2E * MEM_HEAD_DIM ** -0.5)).astype(BF16)
    lo = lax.broadcasted_iota(jnp.int32, (1, LANES), 1) < SWA_HEAD_DIM
    for src, dst in ((head64_rms(kva, row(_SP_SWA_K, MXU_DIM))[:, :LANES], ka_ref), (kva[:, LANES:], va_ref)):
        swapped = pltpu.roll(src, SWA_HEAD_DIM, 1)
        dst[:, :LANES] = jnp.where(lo, src, swapped).astype(BF16)
        dst[:, LANES:] = jnp.where(lo, swapped, src).astype(BF16)
    krd_ref[...] = rope_dup(_rms(kr, row(_SP_KR))).astype(BF16)

    qb = _dot(cqn, wq_ref[...])
    kvb = _dot(ckvn, wkv_ref[...])
    qa = proj(_C_QA)

    scale = LOG2E * (MLA_NOPE_DIM + MLA_ROPE_DIM) ** -0.5
    for h in range(MLA_HEADS):
        o = h * 2 * LANES
        qb_ref[:, o:o + LANES] = (_rms(qb[:, o:o + LANES], row(_SP_QN)) * scale).astype(BF16)
        qr = rope_dup(_rms(qb[:, o + LANES:o + 2 * LANES], row(_SP_QR)))
        qb_ref[:, o + LANES:o + 2 * LANES] = (qr * (0.5 * scale)).astype(BF16)
    hn_cols = MLA_HEADS * MLA_NOPE_DIM
    for h in range(MLA_HEADS):
        sl = slice(h * MLA_NOPE_DIM, (h + 1) * MLA_NOPE_DIM)
        kn_ref[:, sl] = _rms(kvb[:, sl], row(_SP_KN)).astype(BF16)
        vb_ref[:, 2 * h * LANES:(2 * h + 1) * LANES] = kvb[:, hn_cols + h * MLA_V_DIM:
                                                          hn_cols + (h + 1) * MLA_V_DIM].astype(BF16)
        vb_ref[:, (2 * h + 1) * LANES:(2 * h + 2) * LANES] = jnp.ones((kvb.shape[0], LANES), BF16)

    for c in range(_C_QA[1] // MXU_DIM):
        sl = slice(c * MXU_DIM, (c + 1) * MXU_DIM)
        qa_ref[:, sl] = (head64_rms(qa[:, sl], row(_SP_SWA_Q, MXU_DIM))
                         * (LOG2E * SWA_HEAD_DIM ** -0.5)).astype(BF16)


def _in_proj(x2d, pos_col, gattn, sp, w_in_stack, layer, w2, wq, wkv, gcq, gckv, bd, late_w, tm):
    t, d = x2d.shape
    steps = t // tm
    w1_spec = pl.BlockSpec((None, w_in_stack.shape[1], _W1_COLS), lambda i: (layer, 0, 0),
                           pipeline_mode=pl.Buffered(1))
    widths = (SWA_Q_HEADS * SWA_HEAD_DIM,
              SWA_KV_HEADS * LANES,
              SWA_KV_HEADS * LANES,
              LANES,
              MEM_HEADS * MEM_HEAD_DIM,
              MLA_HEADS * 2 * LANES,
              MLA_HEADS * MLA_NOPE_DIM,
              MLA_HEADS * 2 * LANES)
    rows = lambda w: pl.BlockSpec((tm, w), lambda i: (i, 0))
    slabs = []
    for w in late_w:
        assert w.shape[0] % (16 * steps) == 0, (w.shape, steps)
        slabs.append(pl.BlockSpec((w.shape[0] // steps, w.shape[1]), lambda i: (i, 0)))
    return pl.pallas_call(
        _in_proj_kernel,
        grid=(steps,),
        in_specs=[rows(d), rows(1), _const_spec(gattn.shape), _const_spec(sp.shape),
                  w1_spec, _const_spec(w2.shape), _const_spec(wq.shape),
                  _const_spec(wkv.shape),
                  _const_spec(gcq.shape), _const_spec(gckv.shape), _const_spec(bd.shape)] + slabs,
        out_specs=[rows(w) for w in widths] + slabs,
        out_shape=[jax.ShapeDtypeStruct((t, w), BF16) for w in widths]
        + [jax.ShapeDtypeStruct(w.shape, BF16) for w in late_w],
        compiler_params=pltpu.CompilerParams(dimension_semantics=("arbitrary",),
                                             vmem_limit_bytes=VMEM_LIMIT_BYTES),
        name="in_proj",
    )(x2d, pos_col, gattn, sp, w_in_stack, w2, wq, wkv, gcq, gckv, bd, *late_w)


_SWA_MASK_DIST = -NEG_INF * 2.0 ** 8


def _swa_tile(i, ss_ref, q_ref, kc_ref, kp_ref, vc_ref, vp_ref, pq_ref, pkc_ref, pkp_ref, o_ref, *, tq,
              filler=None):
    g = SWA_Q_HEADS // SWA_KV_HEADS
    kk = lax.broadcasted_iota(jnp.int32, (BLOCK, BLOCK), 0)
    qq = lax.broadcasted_iota(jnp.int32, (BLOCK, BLOCK), 1)
    from_prev = kk > qq
    prev_w = jnp.where(from_prev, 1.0, 0.0).astype(BF16)
    cur_w = jnp.where(from_prev, 0.0, 1.0).astype(BF16)
    lane_lo = lax.broadcasted_iota(jnp.int32, (BLOCK, LANES), 1) < SWA_HEAD_DIM
    lo_w = jnp.where(lane_lo, 1.0, 0.0).astype(BF16)
    hi_w = jnp.where(lane_lo, 0.0, 1.0).astype(BF16)

    for j in range(tq // BLOCK):
        if filler is not None:
            filler(j, tq // BLOCK)
        rows = slice(j * BLOCK, (j + 1) * BLOCK)
        prows = slice((j - 1) * BLOCK, j * BLOCK)
        k_prev, v_prev, pk_prev = ((kp_ref[...], vp_ref[...], pkp_ref[...]) if j == 0 else
                                   (kc_ref[prows, :], vc_ref[prows, :], pkc_ref[prows, :]))
        kj = jnp.concatenate([k_prev, kc_ref[rows, :]], axis=0)
        vj = jnp.concatenate([v_prev, vc_ref[rows, :]], axis=0)
        pq = pq_ref[:, rows]
        dist = jnp.where(from_prev, jnp.abs(pk_prev - pq), jnp.abs(pkc_ref[rows, :] - pq)).astype(F32)
        if j == 0:
            dist = jnp.where(from_prev, jnp.maximum(dist, jnp.where(i == 0, _SWA_MASK_DIST, 0.0)), dist)
        for kv in range(SWA_KV_HEADS):
            kvl = slice(kv * LANES, (kv + 1) * LANES)
            qm = jnp.concatenate(
                [q_ref[rows, (kv * g + hh) // 2 * LANES:((kv * g + hh) // 2 + 1) * LANES]
                 * (lo_w if hh % 2 == 0 else hi_w) for hh in range(g)], axis=0)
            st = _dot_nt(kj[:, kvl], qm)
            pts = []
            for hh in range(g):
                head = kv * g + hh
                cols = slice(hh * BLOCK, (hh + 1) * BLOCK)
                t = jnp.where(from_prev, st[:BLOCK, cols], st[BLOCK:, cols]) - ss_ref[0, head] * dist
                sink = ss_ref[1, head]
                m = jnp.maximum(jnp.max(t, axis=0, keepdims=True), sink)
                e = jnp.exp2(t - m)
                den = jnp.sum(e, axis=0, keepdims=True) + jnp.exp2(sink - m)
                p = (e * (1.0 / den)).astype(BF16)
                pts.append(jnp.concatenate([p * prev_w, p * cur_w], axis=0))
            pt = jnp.concatenate(pts, axis=1)
            o = lax.dot_general(pt, vj[:, kvl], (((0,), (0,)), ((), ())), preferred_element_type=F32)
            for pr in range(g // 2):
                grp = kv * (g // 2) + pr
                even = o[(2 * pr) * BLOCK:(2 * pr + 1) * BLOCK]
                odd = o[(2 * pr + 1) * BLOCK:(2 * pr + 2) * BLOCK]
                o_ref[rows, grp * LANES:(grp + 1) * LANES] = jnp.where(lane_lo, even, odd).astype(BF16)


def _mem_attn_tile(q_ref, k_ref, v_ref, o_ref):
    for h in range(MEM_HEADS):
        sl = slice(h * MEM_HEAD_DIM, (h + 1) * MEM_HEAD_DIM)
        s = _dot_nt(q_ref[:, sl], k_ref[:, sl])
        e = jnp.exp2(s - jnp.max(s, axis=-1, keepdims=True))
        o = _dot(e.astype(BF16), v_ref[:, 2 * h * LANES:(2 * h + 2) * LANES])
        o_ref[:, sl] = (o[:, :MEM_HEAD_DIM] / o[:, MEM_HEAD_DIM:]).astype(BF16)


def _swa_out_kernel(ss_ref, q_ref, kc_ref, kp_ref, vc_ref, vp_ref, pq_ref, pkc_ref, pkp_ref,
                    x_ref, yb_ref, qm_ref, km_ref, vm_ref, w_ref, h_ref, ya_sc, ym_sc, *, tq):
    na = ya_sc.shape[1]
    nb = na + yb_ref.shape[1]
    _mem_attn_tile(qm_ref, km_ref, vm_ref, ym_sc)
    d = h_ref.shape[1]

    def partial_projection(j, n):
        cols = slice(j * d // n, (j + 1) * d // n)
        h_ref[:, cols] = (x_ref[:, cols] + _dot(yb_ref[...], w_ref[na:nb, cols])
                          + _dot(ym_sc[...], w_ref[nb:, cols]))

    _swa_tile(pl.program_id(1), ss_ref, q_ref, kc_ref, kp_ref, vc_ref, vp_ref, pq_ref, pkc_ref, pkp_ref,
              ya_sc, tq=tq, filler=partial_projection)
    h_ref[...] += _dot(ya_sc[...], w_ref[0:na, :])


def _swa_out(slope_sink, qa, ka, va, pos_col, pos_row, x2d, yb, qm, km, vm, w_out, batch, seq, m_len, tq):
    t, d = x2d.shape
    nq = seq // tq
    nb = tq // BLOCK
    na = SWA_Q_HEADS * SWA_HEAD_DIM
    hd = MEM_HEADS * MEM_HEAD_DIM
    cur = lambda w: pl.BlockSpec((tq, w), lambda b, i: (b * nq + i, 0))
    prev = lambda w: pl.BlockSpec(
        (BLOCK, w), lambda b, i: (b * nq * nb + jnp.maximum(i * nb - 1, 0), 0))
    return pl.pallas_call(
        functools.partial(_swa_out_kernel, tq=tq),
        grid=(batch, nq),
        in_specs=[pl.BlockSpec(memory_space=pltpu.SMEM),
                  cur(na), cur(2 * LANES), prev(2 * LANES), cur(2 * LANES), prev(2 * LANES),
                  pl.BlockSpec((None, 1, tq), lambda b, i: (b, 0, i)),
                  cur(1), prev(1),
                  cur(d), cur(yb.shape[1]), cur(hd),
                  pl.BlockSpec((m_len, hd), lambda b, i: (b, 0)),
                  pl.BlockSpec((m_len, 2 * hd), lambda b, i: (b, 0)),
                  _const_spec(w_out.shape)],
        out_specs=cur(d),
        out_shape=jax.ShapeDtypeStruct((t, d), F32),
        scratch_shapes=[pltpu.VMEM((tq, na), BF16), pltpu.VMEM((tq, hd), BF16)],
        compiler_params=pltpu.CompilerParams(dimension_semantics=("arbitrary", "arbitrary"),
                                             vmem_limit_bytes=VMEM_LIMIT_BYTES),
        name="swa_out",
    )(slope_sink, qa, ka, ka, va, va, pos_row, pos_col, pos_col, x2d, yb, qm, km, vm, w_out)


def _mla_kernel(q_ref, kn_ref, kr_ref, v_ref, o_ref, s_sc, m_sc, acc_sc, *, tq, hps):
    qi = pl.program_id(2)
    qw = 2 * LANES
    tk = tq // 2

    all_heads = tuple(range(hps))

    def scores_to(slot, j, row0=0, heads=all_heads):
        k0 = pl.multiple_of(j * tk, tk)
        kr = kr_ref[pl.ds(k0, tk), :]
        for hh in heads:
            k = jnp.concatenate([kn_ref[pl.ds(k0, tk), hh * LANES:(hh + 1) * LANES], kr], axis=1)
            s_sc[slot, hh, row0:, :] = _dot_nt(q_ref[row0:, hh * qw:(hh + 1) * qw], k)

    def update_from(slot, j, row0=0, masked=False, heads=all_heads):
        k0 = pl.multiple_of(j * tk, tk)
        for hh in heads:
            for r0 in range(row0, tq, tk):
                rs = slice(r0, r0 + tk)
                s = s_sc[slot, hh, rs, :]
                if masked and r0 - row0 < tk:
                    r = lax.broadcasted_iota(jnp.int32, s.shape, 0) + (r0 - row0)
                    c = lax.broadcasted_iota(jnp.int32, s.shape, 1)
                    s = jnp.where(c <= r, s, NEG_INF)
                m = m_sc[hh, rs, :]
                m_new = jnp.maximum(m, jnp.max(s, axis=-1, keepdims=True))
                p = jnp.exp2(s - m_new)
                m_sc[hh, rs, :] = m_new
                acc_sc[hh, rs, :] = jnp.exp2(m - m_new) * acc_sc[hh, rs, :] + _dot(
                    p.astype(BF16), v_ref[pl.ds(k0, tk), hh * qw:(hh + 1) * qw])

    m_sc[...] = jnp.full(m_sc.shape, NEG_INF, F32)
    acc_sc[...] = jnp.zeros(acc_sc.shape, F32)

    scores_to(0, 0)

    def pair(p, carry):
        j = 2 * p
        for hh in all_heads:
            scores_to(1, j + 1, heads=(hh,))
            update_from(0, j, heads=(hh,))
        for hh in all_heads:
            scores_to(0, j + 2, heads=(hh,))
            update_from(1, j + 1, heads=(hh,))
        return carry

    lax.fori_loop(0, qi, pair, 0)
    for hh in all_heads:
        scores_to(1, 2 * qi + 1, row0=tk, heads=(hh,))
        update_from(0, 2 * qi, masked=True, heads=(hh,))
    update_from(1, 2 * qi + 1, row0=tk, masked=True)

    for hh in range(hps):
        acc = acc_sc[hh]
        o_ref[:, hh * MLA_V_DIM:(hh + 1) * MLA_V_DIM] = (acc[:, :MLA_V_DIM] / acc[:, MLA_V_DIM:]).astype(BF16)


def _mla_attn(qb, kn, krd, vb, batch, seq, tq, hps):
    t = batch * seq
    nq = seq // tq
    assert tq % 2 == 0 and seq % tq == 0
    return pl.pallas_call(
        functools.partial(_mla_kernel, tq=tq, hps=hps),
        grid=(batch, MLA_HEADS // hps, nq),
        in_specs=[pl.BlockSpec((tq, 2 * LANES * hps), lambda b, h, i: (b * nq + i, h)),
                  pl.BlockSpec((seq, MLA_NOPE_DIM * hps), lambda b, h, i: (b, h)),
                  pl.BlockSpec((seq, LANES), lambda b, h, i: (b, 0)),
                  pl.BlockSpec((seq, 2 * LANES * hps), lambda b, h, i: (b, h))],
        out_specs=pl.BlockSpec((tq, MLA_V_DIM * hps), lambda b, h, i: (b * nq + i, h)),
        out_shape=jax.ShapeDtypeStruct((t, MLA_HEADS * MLA_V_DIM), BF16),
        scratch_shapes=[pltpu.VMEM((2, hps, tq, tq // 2), F32), pltpu.VMEM((hps, tq, 1), F32),
                        pltpu.VMEM((hps, tq, 2 * LANES), F32)],
        compiler_params=pltpu.CompilerParams(dimension_semantics=("arbitrary",) * 3,
                                             vmem_limit_bytes=VMEM_LIMIT_BYTES),
        name="mla_attn",
    )(qb, kn, krd, vb)


def _ffn_kernel(h_ref, g_ref, wg_hbm, wu_hbm, wd_hbm, o_ref, fn_ref, wg_buf, wu_buf, wd_buf, sem, *, tf):
    tiles = wg_hbm.shape[1] // tf

    def copies(j, slot):
        return (pltpu.make_async_copy(wg_hbm.at[:, pl.ds(j * tf, tf)], wg_buf.at[slot], sem.at[0, slot]),
                pltpu.make_async_copy(wu_hbm.at[:, pl.ds(j * tf, tf)], wu_buf.at[slot], sem.at[1, slot]),
                pltpu.make_async_copy(wd_hbm.at[pl.ds(j * tf, tf), :], wd_buf.at[slot], sem.at[2, slot]))

    for j in range(min(2, tiles)):
        for cp in copies(j, j % 2):
            cp.start()
    fn_ref[...] = _rms(h_ref[...], g_ref[...]).astype(BF16)

    half = tf // 2
    for j in range(tiles):
        slot = j % 2
        for cp in copies(j, slot):
            cp.wait()
        fn = fn_ref[...]
        acts = []
        for c in range(2):
            cols = slice(c * half, (c + 1) * half)
            gate = _dot(fn, wg_buf[slot, :, cols])
            up = _dot(fn, wu_buf[slot, :, cols])
            acts.append((gate * jax.nn.sigmoid(gate) * up).astype(BF16))
        down = _dot(acts[0], wd_buf[slot, :half, :]) + _dot(acts[1], wd_buf[slot, half:, :])
        if j == 0:
            o_ref[...] = h_ref[...] + down
        else:
            o_ref[...] += down
        if j + 2 < tiles:
            for cp in copies(j + 2, slot):
                cp.start()


def _ffn(h, g, wg, wu, wd, tm, tf):
    t, d = h.shape
    dff = wg.shape[1]
    assert dff % tf == 0
    hbm = pl.BlockSpec(memory_space=pl.ANY)
    return pl.pallas_call(
        functools.partial(_ffn_kernel, tf=tf),
        grid=(t // tm,),
        in_specs=[pl.BlockSpec((tm, d), lambda i: (i, 0)), _const_spec((1, d)), hbm, hbm, hbm],
        out_specs=pl.BlockSpec((tm, d), lambda i: (i, 0)),
        out_shape=jax.ShapeDtypeStruct((t, d), F32),
        scratch_shapes=[pltpu.VMEM((tm, d), BF16),
                        pltpu.VMEM((2, d, tf), BF16), pltpu.VMEM((2, d, tf), BF16),
                        pltpu.VMEM((2, tf, d), BF16), pltpu.SemaphoreType.DMA((3, 2))],
        compiler_params=pltpu.CompilerParams(dimension_semantics=("arbitrary",),
                                             vmem_limit_bytes=VMEM_LIMIT_BYTES),
        name="ffn",
    )(h, g, wg, wu, wd)


def _tile_row(v, width):
    v = v.astype(F32).reshape(-1)
    return jnp.tile(v, width // v.shape[0])


def _layer(h, mem2d, pos_col, pos_row, batch, seq, m_len, w_in_stack, layer, p):
    (attn_norm_g, swa_q_g, swa_k_g, swa_sinks, cq_g, ckv_g, w_uq, w_ukv, qn_g, qr_g, kn_g, kr_g,
     mem_g, w_mem_kv, memq_g, memk_g, w_out, ffn_g, w_gate, w_up, w_down) = p
    width = 2 * LANES
    inv_freq = ROPE_THETA ** (-jnp.arange(0, MLA_ROPE_DIM, 2, dtype=F32) / MLA_ROPE_DIM)
    sign = jnp.concatenate([-jnp.ones((MLA_ROPE_DIM // 2,), F32), jnp.ones((MLA_ROPE_DIM // 2,), F32)])
    rows = [swa_q_g, swa_k_g, qn_g, qr_g, kn_g, kr_g, memq_g, inv_freq, sign]
    sp = jnp.stack([_tile_row(r, width) for r in rows]
                   + [jnp.zeros((width,), F32)] * (_SP_ROWS - len(rows)))

    qd = MLA_NOPE_DIM + MLA_ROPE_DIM
    wq3 = w_uq.reshape(w_uq.shape[0], MLA_HEADS, qd)
    wq = jnp.concatenate([wq3, wq3[:, :, MLA_NOPE_DIM:]], axis=2).reshape(w_uq.shape[0], -1).astype(BF16)
    wkv3 = w_ukv.reshape(w_ukv.shape[0], MLA_HEADS, MLA_NOPE_DIM + MLA_V_DIM)
    wkv = jnp.concatenate([wkv3[:, :, :MLA_NOPE_DIM].reshape(w_ukv.shape[0], -1),
                           wkv3[:, :, MLA_NOPE_DIM:].reshape(w_ukv.shape[0], -1)], axis=1).astype(BF16)
    idx = jnp.arange(MXU_DIM) // SWA_HEAD_DIM
    bd = (idx[:, None] == idx[None, :]).astype(BF16)
    slopes = 2.0 ** (-8.0 * jnp.arange(1, SWA_Q_HEADS + 1, dtype=F32) / SWA_Q_HEADS)
    slope_sink = jnp.stack([slopes, swa_sinks.astype(F32)]) * LOG2E

    km, vm, w2 = _mem_kv(mem2d, mem_g.reshape(1, -1), w_mem_kv, memk_g.reshape(1, -1), w_in_stack,
                         layer, batch, m_len)
    qa, ka, va, krd, qm, qb, kn, vb, wg16, wu16, wd16, wo16 = _in_proj(
        h, pos_col, attn_norm_g.reshape(1, -1), sp, w_in_stack, layer, w2, wq, wkv, cq_g.reshape(1, -1),
        ckv_g.reshape(1, -1), bd, (w_gate, w_up, w_down, w_out), tm=IN_PROJ_ROWS)
    yb = _mla_attn(qb, kn, krd, vb, batch, seq, tq=MLA_Q_ROWS, hps=MLA_HEADS_PER_STEP)
    h = _swa_out(slope_sink, qa, ka, va, pos_col, pos_row, h, yb, qm, km, vm, wo16, batch, seq, m_len,
                 tq=ATTN_OUT_ROWS)
    return _ffn(h, ffn_g.reshape(1, -1), wg16, wu16, wd16, tm=FFN_ROWS, tf=FFN_COLS)


def kernel(x, mem, positions, attn_norm_g, w_in, swa_q_norm_g, swa_k_norm_g, swa_sinks, mla_cq_norm_g,
           mla_ckv_norm_g, w_uq, w_ukv, mla_qn_norm_g, mla_qr_norm_g, mla_kn_norm_g, mla_kr_norm_g,
           mem_norm_g, w_mem_kv, mem_q_norm_g, mem_k_norm_g, w_out, ffn_norm_g, w_gate, w_up, w_down):
    batch, seq, d = x.shape
    m_len = mem.shape[1]
    stacked = (attn_norm_g, swa_q_norm_g, swa_k_norm_g, swa_sinks, mla_cq_norm_g, mla_ckv_norm_g,
               w_uq, w_ukv, mla_qn_norm_g, mla_qr_norm_g, mla_kn_norm_g, mla_kr_norm_g, mem_norm_g,
               w_mem_kv, mem_q_norm_g, mem_k_norm_g, w_out, ffn_norm_g, w_gate, w_up, w_down)
    h = x.reshape(batch * seq, d)
    mem2d = mem.reshape(batch * m_len, d)
    pos_col = positions.reshape(batch * seq, 1)
    pos_row = positions.reshape(batch, 1, seq)
    w_in16 = w_in.astype(BF16)
    for l in range(attn_norm_g.shape[0]):
        h = _layer(h, mem2d, pos_col, pos_row, batch, seq, m_len, w_in16, l, tuple(a[l] for a in stacked))
    return h.reshape(batch, seq, d)
```

```python
import functools

import jax
import jax.numpy as jnp
from jax import lax
from jax.experimental import pallas as pl
from jax.experimental.pallas import tpu as pltpu

EPS = 1e-6
NEG_INF = -1e30
LOG2E = 1.4426950408889634
BLOCK = 128

SWA_Q_HEADS = 16
SWA_KV_HEADS = 2
SWA_HEAD_DIM = 64

MLA_HEADS = 4
MLA_Q_RANK = 512
MLA_KV_RANK = 512
MLA_NOPE_DIM = 128
MLA_ROPE_DIM = 64
MLA_V_DIM = 128
ROPE_THETA = 10000.0

MEM_HEADS = 4
MEM_HEAD_DIM = 128

LANES = 128
MXU_DIM = 256
VMEM_LIMIT_BYTES = 56 * 1024 * 1024

IN_PROJ_ROWS = 512
ATTN_OUT_ROWS = 512
MLA_Q_ROWS = 1024
MLA_HEADS_PER_STEP = 2
FFN_ROWS = 1024
FFN_COLS = 512

F32 = jnp.float32
BF16 = jnp.bfloat16

_SP_SWA_Q, _SP_SWA_K, _SP_QN, _SP_QR, _SP_KN, _SP_KR, _SP_MEMQ, _SP_FREQ, _SP_SIGN = range(9)
_SP_ROWS = 16


def _dot(a, b):
    return jnp.dot(a, b, preferred_element_type=F32)


def _dot_nt(a, b):
    return lax.dot_general(a, b, (((1,), (1,)), ((), ())), preferred_element_type=F32)


def _rms(x, g):
    return x * lax.rsqrt(jnp.mean(x * x, axis=-1, keepdims=True) + EPS) * g


def _const_spec(shape):
    nd = len(shape)
    return pl.BlockSpec(shape, lambda *_: (0,) * nd, pipeline_mode=pl.Buffered(1))


def _mem_kv_kernel(mem_ref, g_ref, w32_ref, gk_ref, win_ref, k_ref, v_ref, w2_ref, w_ref):
    @pl.when(pl.program_id(0) == 0)
    def _():
        w_ref[...] = w32_ref[...].astype(BF16)

    half = MLA_ROPE_DIM
    lo = lax.broadcasted_iota(jnp.int32, (1, LANES), 1) < half
    n_full = (win_ref.shape[1] - _W1_COLS) // LANES
    tail = [win_ref[:, _W1_COLS + k * LANES:_W1_COLS + (k + 1) * LANES].astype(F32) for k in range(n_full)]
    last = win_ref[:, _W1_COLS + n_full * LANES:].astype(F32)
    tail.append(jnp.concatenate([last, last], axis=1))
    swapped = [pltpu.roll(c, half, 1) for c in tail[:n_full]]
    w2_ref[:, :LANES] = jnp.where(lo, tail[0], swapped[0]).astype(BF16)
    for k in range(1, n_full + 1):
        nxt = swapped[k] if k < n_full else tail[k]
        w2_ref[:, k * LANES:(k + 1) * LANES] = jnp.where(lo, swapped[k - 1], nxt).astype(BF16)

    xn = _rms(mem_ref[...], g_ref[...]).astype(BF16)
    kv = _dot(xn, w_ref[...])
    hd = MEM_HEADS * MEM_HEAD_DIM
    for h in range(MEM_HEADS):
        sl = slice(h * MEM_HEAD_DIM, (h + 1) * MEM_HEAD_DIM)
        k_ref[:, sl] = _rms(kv[:, sl], gk_ref[...]).astype(BF16)
        v_ref[:, 2 * h * LANES:(2 * h + 1) * LANES] = kv[:, hd + h * MEM_HEAD_DIM:
                                                         hd + (h + 1) * MEM_HEAD_DIM].astype(BF16)
        v_ref[:, (2 * h + 1) * LANES:(2 * h + 2) * LANES] = jnp.ones((kv.shape[0], LANES), BF16)


def _mem_kv(mem2d, g, w, gk, w_in_stack, layer, batch, m_len):
    d = mem2d.shape[1]
    hd = MEM_HEADS * MEM_HEAD_DIM
    _, rows, cols = w_in_stack.shape
    assert rows % (16 * batch) == 0 and (cols - _W1_COLS) % LANES == MLA_ROPE_DIM
    slab = rows // batch
    w2_cols = cols - _W1_COLS + MLA_ROPE_DIM
    return pl.pallas_call(
        _mem_kv_kernel,
        grid=(batch,),
        in_specs=[pl.BlockSpec((m_len, d), lambda b: (b, 0)),
                  _const_spec((1, d)), _const_spec((d, 2 * hd)), _const_spec((1, MEM_HEAD_DIM)),
                  pl.BlockSpec((None, slab, cols), lambda b: (layer, b, 0))],
        out_specs=[pl.BlockSpec((m_len, hd), lambda b: (b, 0)), pl.BlockSpec((m_len, 2 * hd), lambda b: (b, 0)),
                   pl.BlockSpec((slab, w2_cols), lambda b: (b, 0))],
        out_shape=[jax.ShapeDtypeStruct((batch * m_len, hd), BF16),
                   jax.ShapeDtypeStruct((batch * m_len, 2 * hd), BF16),
                   jax.ShapeDtypeStruct((rows, w2_cols), BF16)],
        scratch_shapes=[pltpu.VMEM((d, 2 * hd), BF16)],
        compiler_params=pltpu.CompilerParams(dimension_semantics=("arbitrary",),
                                             vmem_limit_bytes=VMEM_LIMIT_BYTES),
        name="mem_kv",
    )(mem2d, g, w, gk, w_in_stack)


_C_QA = (0, SWA_Q_HEADS * SWA_HEAD_DIM)
_C_KVA = (_C_QA[1], _C_QA[1] + 2 * SWA_KV_HEADS * SWA_HEAD_DIM)
_C_CQ = (_C_KVA[1], _C_KVA[1] + MLA_Q_RANK)
_C_CKV = (_C_CQ[1], _C_CQ[1] + MLA_KV_RANK)
_W1_COLS = _C_CKV[1]
_C_KR = (_W1_COLS, _W1_COLS + 2 * MLA_ROPE_DIM)
_C_QM = (_C_KR[1], _C_KR[1] + MEM_HEADS * MEM_HEAD_DIM)
assert _W1_COLS % LANES == 0 and _C_QA[1] % MXU_DIM == 0 and _C_KVA[1] - _C_KVA[0] == MXU_DIM


def _in_proj_kernel(x_ref, pos_ref, gattn_ref, sp_ref, w1_ref, w2_ref, wq_ref, wkv_ref, gcq_ref, gckv_ref,
                    bd_ref, wg32_ref, wu32_ref, wd32_ref, wo32_ref,
                    qa_ref, ka_ref, va_ref, krd_ref, qm_ref, qb_ref, kn_ref, vb_ref,
                    wg16_ref, wu16_ref, wd16_ref, wo16_ref):
    for c in range(wg16_ref.shape[0]):
        cols = slice(c * FFN_COLS, (c + 1) * FFN_COLS)
        wg16_ref[c] = wg32_ref[:, cols].astype(BF16)
        wu16_ref[c] = wu32_ref[:, cols].astype(BF16)
    wd16_ref[...] = wd32_ref[...].astype(BF16)
    wo16_ref[...] = wo32_ref[...].astype(BF16)
    sp = sp_ref[...]
    row = lambda r, n=LANES: sp[r:r + 1, :n]
    hn = _rms(x_ref[...], gattn_ref[...]).astype(BF16)

    def proj(cols):
        if cols[0] >= _W1_COLS:
            return _dot(hn, w2_ref[:, cols[0] - _W1_COLS:cols[1] - _W1_COLS])
        return _dot(hn, w1_ref[:, cols[0]:cols[1]])

    def head64_rms(blk, g):
        ss = _dot((blk * blk).astype(BF16), bd_ref[...])
        return blk * lax.rsqrt(ss * (1.0 / SWA_HEAD_DIM) + EPS) * g

    ang = pos_ref[...].astype(F32) * row(_SP_FREQ)
    cos = jnp.cos(ang)
    sin_signed = jnp.sin(ang) * row(_SP_SIGN)

    def rope_dup(y):
        return y * cos + pltpu.roll(y, MLA_ROPE_DIM // 2, 1) * sin_signed

    cq = proj(_C_CQ)
    ckv = proj(_C_CKV)
    qm = proj(_C_QM)
    kva = proj(_C_KVA)
    kr = proj(_C_KR)
    cqn = _rms(cq, gcq_ref[...]).astype(BF16)
    ckvn = _rms(ckv, gckv_ref[...]).astype(BF16)

    for h in range(MEM_HEADS):
        sl = slice(h * MEM_HEAD_DIM, (h + 1) * MEM_HEAD_DIM)
        qm_ref[:, sl] = (_rms(qm[:, sl], row(_SP_MEMQ)) * (LOG2E * MEM_HEAD_DIM ** -0.5)).astype(BF16)
    lo = lax.broadcasted_iota(jnp.int32, (1, LANES), 1) < SWA_HEAD_DIM
    for src, dst in ((head64_rms(kva, row(_SP_SWA_K, MXU_DIM))[:, :LANES], ka_ref), (kva[:, LANES:], va_ref)):
        swapped = pltpu.roll(src, SWA_HEAD_DIM, 1)
        dst[:, :LANES] = jnp.where(lo, src, swapped).astype(BF16)
        dst[:, LANES:] = jnp.where(lo, swapped, src).astype(BF16)
    krd_ref[...] = rope_dup(_rms(kr, row(_SP_KR))).astype(BF16)

    qb = _dot(cqn, wq_ref[...])
    kvb = _dot(ckvn, wkv_ref[...])
    qa = proj(_C_QA)

    scale = LOG2E * (MLA_NOPE_DIM + MLA_ROPE_DIM) ** -0.5
    for h in range(MLA_HEADS):
        o = h * 2 * LANES
        qb_ref[:, o:o + LANES] = (_rms(qb[:, o:o + LANES], row(_SP_QN)) * scale).astype(BF16)
        qr = rope_dup(_rms(qb[:, o + LANES:o + 2 * LANES], row(_SP_QR)))
        qb_ref[:, o + LANES:o + 2 * LANES] = (qr * (0.5 * scale)).astype(BF16)
    hn_cols = MLA_HEADS * MLA_NOPE_DIM
    for h in range(MLA_HEADS):
        sl = slice(h * MLA_NOPE_DIM, (h + 1) * MLA_NOPE_DIM)
        kn_ref[:, sl] = _rms(kvb[:, sl], row(_SP_KN)).astype(BF16)
        vb_ref[:, 2 * h * LANES:(2 * h + 1) * LANES] = kvb[:, hn_cols + h * MLA_V_DIM:
                                                          hn_cols + (h + 1) * MLA_V_DIM].astype(BF16)
        vb_ref[:, (2 * h + 1) * LANES:(2 * h + 2) * LANES] = jnp.ones((kvb.shape[0], LANES), BF16)

    for c in range(_C_QA[1] // MXU_DIM):
        sl = slice(c * MXU_DIM, (c + 1) * MXU_DIM)
        qa_ref[:, sl] = (head64_rms(qa[:, sl], row(_SP_SWA_Q, MXU_DIM))
                         * (LOG2E * SWA_HEAD_DIM ** -0.5)).astype(BF16)


def _in_proj(x2d, pos_col, gattn, sp, w_in_stack, layer, w2, wq, wkv, gcq, gckv, bd, late_w, tm):
    t, d = x2d.shape
    steps = t // tm
    w1_spec = pl.BlockSpec((None, w_in_stack.shape[1], _W1_COLS), lambda i: (layer, 0, 0),
                           pipeline_mode=pl.Buffered(1))
    widths = (SWA_Q_HEADS * SWA_HEAD_DIM,
              SWA_KV_HEADS * LANES,
              SWA_KV_HEADS * LANES,
              LANES,
              MEM_HEADS * MEM_HEAD_DIM,
              MLA_HEADS * 2 * LANES,
              MLA_HEADS * MLA_NOPE_DIM,
              MLA_HEADS * 2 * LANES)
    rows = lambda w: pl.BlockSpec((tm, w), lambda i: (i, 0))
    slabs, out_slabs, out_shapes = [], [], []
    for k, w in enumerate(late_w):
        assert w.shape[0] % (16 * steps) == 0, (w.shape, steps)
        slab = w.shape[0] // steps
        slabs.append(pl.BlockSpec((slab, w.shape[1]), lambda i: (i, 0)))
        if k < 2:
            tiles = w.shape[1] // FFN_COLS
            out_slabs.append(pl.BlockSpec((tiles, slab, FFN_COLS), lambda i: (0, i, 0)))
            out_shapes.append(jax.ShapeDtypeStruct((tiles, w.shape[0], FFN_COLS), BF16))
        else:
            out_slabs.append(slabs[-1])
            out_shapes.append(jax.ShapeDtypeStruct(w.shape, BF16))
    return pl.pallas_call(
        _in_proj_kernel,
        grid=(steps,),
        in_specs=[rows(d), rows(1), _const_spec(gattn.shape), _const_spec(sp.shape),
                  w1_spec, _const_spec(w2.shape), _const_spec(wq.shape),
                  _const_spec(wkv.shape),
                  _const_spec(gcq.shape), _const_spec(gckv.shape), _const_spec(bd.shape)] + slabs,
        out_specs=[rows(w) for w in widths] + out_slabs,
        out_shape=[jax.ShapeDtypeStruct((t, w), BF16) for w in widths] + out_shapes,
        compiler_params=pltpu.CompilerParams(dimension_semantics=("arbitrary",),
                                             vmem_limit_bytes=VMEM_LIMIT_BYTES),
        name="in_proj",
    )(x2d, pos_col, gattn, sp, w_in_stack, w2, wq, wkv, gcq, gckv, bd, *late_w)


_SWA_MASK_DIST = -NEG_INF * 2.0 ** 8


def _swa_tile(i, ss_ref, q_ref, kc_ref, kp_ref, vc_ref, vp_ref, pq_ref, pkc_ref, pkp_ref, o_ref, *, tq,
              filler=None):
    g = SWA_Q_HEADS // SWA_KV_HEADS
    kk = lax.broadcasted_iota(jnp.int32, (BLOCK, BLOCK), 0)
    qq = lax.broadcasted_iota(jnp.int32, (BLOCK, BLOCK), 1)
    from_prev = kk > qq
    prev_w = jnp.where(from_prev, 1.0, 0.0).astype(BF16)
    cur_w = jnp.where(from_prev, 0.0, 1.0).astype(BF16)
    lane_lo = lax.broadcasted_iota(jnp.int32, (BLOCK, LANES), 1) < SWA_HEAD_DIM
    lo_w = jnp.where(lane_lo, 1.0, 0.0).astype(BF16)
    hi_w = jnp.where(lane_lo, 0.0, 1.0).astype(BF16)

    for j in range(tq // BLOCK):
        if filler is not None:
            filler(j, tq // BLOCK)
        rows = slice(j * BLOCK, (j + 1) * BLOCK)
        prows = slice((j - 1) * BLOCK, j * BLOCK)
        k_prev, v_prev, pk_prev = ((kp_ref[...], vp_ref[...], pkp_ref[...]) if j == 0 else
                                   (kc_ref[prows, :], vc_ref[prows, :], pkc_ref[prows, :]))
        kj = jnp.concatenate([k_prev, kc_ref[rows, :]], axis=0)
        vj = jnp.concatenate([v_prev, vc_ref[rows, :]], axis=0)
        pq = pq_ref[:, rows]
        dist = jnp.where(from_prev, jnp.abs(pk_prev - pq), jnp.abs(pkc_ref[rows, :] - pq)).astype(F32)
        if j == 0:
            dist = jnp.where(from_prev, jnp.maximum(dist, jnp.where(i == 0, _SWA_MASK_DIST, 0.0)), dist)
        for kv in range(SWA_KV_HEADS):
            kvl = slice(kv * LANES, (kv + 1) * LANES)
            qm = jnp.concatenate(
                [q_ref[rows, (kv * g + hh) // 2 * LANES:((kv * g + hh) // 2 + 1) * LANES]
                 * (lo_w if hh % 2 == 0 else hi_w) for hh in range(g)], axis=0)
            st = _dot_nt(kj[:, kvl], qm)
            pts = []
            for hh in range(g):
                head = kv * g + hh
                cols = slice(hh * BLOCK, (hh + 1) * BLOCK)
                t = jnp.where(from_prev, st[:BLOCK, cols], st[BLOCK:, cols]) - ss_ref[0, head] * dist
                sink = ss_ref[1, head]
                m = jnp.maximum(jnp.max(t, axis=0, keepdims=True), sink)
                e = jnp.exp2(t - m)
                den = jnp.sum(e, axis=0, keepdims=True) + jnp.exp2(sink - m)
                p = (e * (1.0 / den)).astype(BF16)
                pts.append(jnp.concatenate([p * prev_w, p * cur_w], axis=0))
            pt = jnp.concatenate(pts, axis=1)
            o = lax.dot_general(pt, vj[:, kvl], (((0,), (0,)), ((), ())), preferred_element_type=F32)
            for pr in range(g // 2):
                grp = kv * (g // 2) + pr
                even = o[(2 * pr) * BLOCK:(2 * pr + 1) * BLOCK]
                odd = o[(2 * pr + 1) * BLOCK:(2 * pr + 2) * BLOCK]
                o_ref[rows, grp * LANES:(grp + 1) * LANES] = jnp.where(lane_lo, even, odd).astype(BF16)


def _mem_attn_tile(q_ref, k_ref, v_ref, o_ref):
    for h in range(MEM_HEADS):
        sl = slice(h * MEM_HEAD_DIM, (h + 1) * MEM_HEAD_DIM)
        s = _dot_nt(q_ref[:, sl], k_ref[:, sl])
        e = jnp.exp2(s - jnp.max(s, axis=-1, keepdims=True))
        o = _dot(e.astype(BF16), v_ref[:, 2 * h * LANES:(2 * h + 2) * LANES])
        o_ref[:, sl] = (o[:, :MEM_HEAD_DIM] / o[:, MEM_HEAD_DIM:]).astype(BF16)


def _swa_out_kernel(ss_ref, q_ref, kc_ref, kp_ref, vc_ref, vp_ref, pq_ref, pkc_ref, pkp_ref,
                    x_ref, yb_ref, qm_ref, km_ref, vm_ref, w_ref, h_ref, ya_sc, ym_sc, *, tq):
    na = ya_sc.shape[1]
    nb = na + yb_ref.shape[1]
    _mem_attn_tile(qm_ref, km_ref, vm_ref, ym_sc)
    d = h_ref.shape[1]

    def partial_projection(j, n):
        cols = slice(j * d // n, (j + 1) * d // n)
        h_ref[:, cols] = (x_ref[:, cols] + _dot(yb_ref[...], w_ref[na:nb, cols])
                          + _dot(ym_sc[...], w_ref[nb:, cols]))

    _swa_tile(pl.program_id(1), ss_ref, q_ref, kc_ref, kp_ref, vc_ref, vp_ref, pq_ref, pkc_ref, pkp_ref,
              ya_sc, tq=tq, filler=partial_projection)
    h_ref[...] += _dot(ya_sc[...], w_ref[0:na, :])


def _swa_out(slope_sink, qa, ka, va, pos_col, pos_row, x2d, yb, qm, km, vm, w_out, batch, seq, m_len, tq):
    t, d = x2d.shape
    nq = seq // tq
    nb = tq // BLOCK
    na = SWA_Q_HEADS * SWA_HEAD_DIM
    hd = MEM_HEADS * MEM_HEAD_DIM
    cur = lambda w: pl.BlockSpec((tq, w), lambda b, i: (b * nq + i, 0))
    prev = lambda w: pl.BlockSpec(
        (BLOCK, w), lambda b, i: (b * nq * nb + jnp.maximum(i * nb - 1, 0), 0))
    return pl.pallas_call(
        functools.partial(_swa_out_kernel, tq=tq),
        grid=(batch, nq),
        in_specs=[pl.BlockSpec(memory_space=pltpu.SMEM),
                  cur(na), cur(2 * LANES), prev(2 * LANES), cur(2 * LANES), prev(2 * LANES),
                  pl.BlockSpec((None, 1, tq), lambda b, i: (b, 0, i)),
                  cur(1), prev(1),
                  cur(d), cur(yb.shape[1]), cur(hd),
                  pl.BlockSpec((m_len, hd), lambda b, i: (b, 0)),
                  pl.BlockSpec((m_len, 2 * hd), lambda b, i: (b, 0)),
                  _const_spec(w_out.shape)],
        out_specs=cur(d),
        out_shape=jax.ShapeDtypeStruct((t, d), F32),
        scratch_shapes=[pltpu.VMEM((tq, na), BF16), pltpu.VMEM((tq, hd), BF16)],
        compiler_params=pltpu.CompilerParams(dimension_semantics=("arbitrary", "arbitrary"),
                                             vmem_limit_bytes=VMEM_LIMIT_BYTES),
        name="swa_out",
    )(slope_sink, qa, ka, ka, va, va, pos_row, pos_col, pos_col, x2d, yb, qm, km, vm, w_out)


def _mla_kernel(q_ref, kn_ref, kr_ref, v_ref, o_ref, s_sc, m_sc, acc_sc, *, tq, hps):
    qi = pl.program_id(2)
    qw = 2 * LANES
    tk = tq // 2

    all_heads = tuple(range(hps))

    def scores_to(slot, j, row0=0, heads=all_heads):
        k0 = pl.multiple_of(j * tk, tk)
        kr = kr_ref[pl.ds(k0, tk), :]
        for hh in heads:
            k = jnp.concatenate([kn_ref[pl.ds(k0, tk), hh * LANES:(hh + 1) * LANES], kr], axis=1)
            s_sc[slot, hh, row0:, :] = _dot_nt(q_ref[row0:, hh * qw:(hh + 1) * qw], k)

    def update_from(slot, j, row0=0, masked=False, heads=all_heads):
        k0 = pl.multiple_of(j * tk, tk)
        for hh in heads:
            for r0 in range(row0, tq, tk):
                rs = slice(r0, r0 + tk)
                s = s_sc[slot, hh, rs, :]
                if masked and r0 - row0 < tk:
                    r = lax.broadcasted_iota(jnp.int32, s.shape, 0) + (r0 - row0)
                    c = lax.broadcasted_iota(jnp.int32, s.shape, 1)
                    s = jnp.where(c <= r, s, NEG_INF)
                m = m_sc[hh, rs, :]
                m_new = jnp.maximum(m, jnp.max(s, axis=-1, keepdims=True))
                p = jnp.exp2(s - m_new)
                m_sc[hh, rs, :] = m_new
                acc_sc[hh, rs, :] = jnp.exp2(m - m_new) * acc_sc[hh, rs, :] + _dot(
                    p.astype(BF16), v_ref[pl.ds(k0, tk), hh * qw:(hh + 1) * qw])

    m_sc[...] = jnp.full(m_sc.shape, NEG_INF, F32)
    acc_sc[...] = jnp.zeros(acc_sc.shape, F32)

    scores_to(0, 0)

    def pair(p, carry):
        j = 2 * p
        for hh in all_heads:
            scores_to(1, j + 1, heads=(hh,))
            update_from(0, j, heads=(hh,))
        for hh in all_heads:
            scores_to(0, j + 2, heads=(hh,))
            update_from(1, j + 1, heads=(hh,))
        return carry

    lax.fori_loop(0, qi, pair, 0)
    for hh in all_heads:
        scores_to(1, 2 * qi + 1, row0=tk, heads=(hh,))
        update_from(0, 2 * qi, masked=True, heads=(hh,))
    update_from(1, 2 * qi + 1, row0=tk, masked=True)

    for hh in range(hps):
        acc = acc_sc[hh]
        o_ref[:, hh * MLA_V_DIM:(hh + 1) * MLA_V_DIM] = (acc[:, :MLA_V_DIM] / acc[:, MLA_V_DIM:]).astype(BF16)


def _mla_attn(qb, kn, krd, vb, batch, seq, tq, hps):
    t = batch * seq
    nq = seq // tq
    assert tq % 2 == 0 and seq % tq == 0
    return pl.pallas_call(
        functools.partial(_mla_kernel, tq=tq, hps=hps),
        grid=(batch, MLA_HEADS // hps, nq),
        in_specs=[pl.BlockSpec((tq, 2 * LANES * hps), lambda b, h, i: (b * nq + i, h)),
                  pl.BlockSpec((seq, MLA_NOPE_DIM * hps), lambda b, h, i: (b, h)),
                  pl.BlockSpec((seq, LANES), lambda b, h, i: (b, 0)),
                  pl.BlockSpec((seq, 2 * LANES * hps), lambda b, h, i: (b, h))],
        out_specs=pl.BlockSpec((tq, MLA_V_DIM * hps), lambda b, h, i: (b * nq + i, h)),
        out_shape=jax.ShapeDtypeStruct((t, MLA_HEADS * MLA_V_DIM), BF16),
        scratch_shapes=[pltpu.VMEM((2, hps, tq, tq // 2), F32), pltpu.VMEM((hps, tq, 1), F32),
                        pltpu.VMEM((hps, tq, 2 * LANES), F32)],
        compiler_params=pltpu.CompilerParams(dimension_semantics=("arbitrary",) * 3,
                                             vmem_limit_bytes=VMEM_LIMIT_BYTES),
        name="mla_attn",
    )(qb, kn, krd, vb)


def _ffn_kernel(h_ref, g_ref, wg_hbm, wu_hbm, wd_hbm, o_ref, fn_ref, wg_buf, wu_buf, wd_buf, sem, *, tf):
    tiles = wg_hbm.shape[0]

    def copies(j, slot):
        return (pltpu.make_async_copy(wg_hbm.at[j], wg_buf.at[slot], sem.at[0, slot]),
                pltpu.make_async_copy(wu_hbm.at[j], wu_buf.at[slot], sem.at[1, slot]),
                pltpu.make_async_copy(wd_hbm.at[pl.ds(j * tf, tf), :], wd_buf.at[slot], sem.at[2, slot]))

    for j in range(min(2, tiles)):
        for cp in copies(j, j % 2):
            cp.start()
    fn_ref[...] = _rms(h_ref[...], g_ref[...]).astype(BF16)

    half = tf // 2
    for j in range(tiles):
        slot = j % 2
        for cp in copies(j, slot):
            cp.wait()
        fn = fn_ref[...]
        acts = []
        for c in range(2):
            cols = slice(c * half, (c + 1) * half)
            gate = _dot(fn, wg_buf[slot, :, cols])
            up = _dot(fn, wu_buf[slot, :, cols])
            acts.append((gate * jax.nn.sigmoid(gate) * up).astype(BF16))
        down = _dot(acts[0], wd_buf[slot, :half, :]) + _dot(acts[1], wd_buf[slot, half:, :])
        if j == 0:
            o_ref[...] = h_ref[...] + down
        else:
            o_ref[...] += down
        if j + 2 < tiles:
            for cp in copies(j + 2, slot):
                cp.start()


def _ffn(h, g, wg, wu, wd, tm, tf):
    t, d = h.shape
    assert wg.shape[1:] == (d, tf) and wd.shape[0] == wg.shape[0] * tf
    hbm = pl.BlockSpec(memory_space=pl.ANY)
    return pl.pallas_call(
        functools.partial(_ffn_kernel, tf=tf),
        grid=(t // tm,),
        in_specs=[pl.BlockSpec((tm, d), lambda i: (i, 0)), _const_spec((1, d)), hbm, hbm, hbm],
        out_specs=pl.BlockSpec((tm, d), lambda i: (i, 0)),
        out_shape=jax.ShapeDtypeStruct((t, d), F32),
        scratch_shapes=[pltpu.VMEM((tm, d), BF16),
                        pltpu.VMEM((2, d, tf), BF16), pltpu.VMEM((2, d, tf), BF16),
                        pltpu.VMEM((2, tf, d), BF16), pltpu.SemaphoreType.DMA((3, 2))],
        compiler_params=pltpu.CompilerParams(dimension_semantics=("arbitrary",),
                                             vmem_limit_bytes=VMEM_LIMIT_BYTES),
        name="ffn",
    )(h, g, wg, wu, wd)


def _tile_row(v, width):
    v = v.astype(F32).reshape(-1)
    return jnp.tile(v, width // v.shape[0])


def _layer(h, mem2d, pos_col, pos_row, batch, seq, m_len, w_in_stack, layer, p):
    (attn_norm_g, swa_q_g, swa_k_g, swa_sinks, cq_g, ckv_g, w_uq, w_ukv, qn_g, qr_g, kn_g, kr_g,
     mem_g, w_mem_kv, memq_g, memk_g, w_out, ffn_g, w_gate, w_up, w_down) = p
    width = 2 * LANES
    inv_freq = ROPE_THETA ** (-jnp.arange(0, MLA_ROPE_DIM, 2, dtype=F32) / MLA_ROPE_DIM)
    sign = jnp.concatenate([-jnp.ones((MLA_ROPE_DIM // 2,), F32), jnp.ones((MLA_ROPE_DIM // 2,), F32)])
    rows = [swa_q_g, swa_k_g, qn_g, qr_g, kn_g, kr_g, memq_g, inv_freq, sign]
    sp = jnp.stack([_tile_row(r, width) for r in rows]
                   + [jnp.zeros((width,), F32)] * (_SP_ROWS - len(rows)))

    qd = MLA_NOPE_DIM + MLA_ROPE_DIM
    wq3 = w_uq.reshape(w_uq.shape[0], MLA_HEADS, qd)
    wq = jnp.concatenate([wq3, wq3[:, :, MLA_NOPE_DIM:]], axis=2).reshape(w_uq.shape[0], -1).astype(BF16)
    wkv3 = w_ukv.reshape(w_ukv.shape[0], MLA_HEADS, MLA_NOPE_DIM + MLA_V_DIM)
    wkv = jnp.concatenate([wkv3[:, :, :MLA_NOPE_DIM].reshape(w_ukv.shape[0], -1),
                           wkv3[:, :, MLA_NOPE_DIM:].reshape(w_ukv.shape[0], -1)], axis=1).astype(BF16)
    idx = jnp.arange(MXU_DIM) // SWA_HEAD_DIM
    bd = (idx[:, None] == idx[None, :]).astype(BF16)
    slopes = 2.0 ** (-8.0 * jnp.arange(1, SWA_Q_HEADS + 1, dtype=F32) / SWA_Q_HEADS)
    slope_sink = jnp.stack([slopes, swa_sinks.astype(F32)]) * LOG2E

    km, vm, w2 = _mem_kv(mem2d, mem_g.reshape(1, -1), w_mem_kv, memk_g.reshape(1, -1), w_in_stack,
                         layer, batch, m_len)
    qa, ka, va, krd, qm, qb, kn, vb, wg16, wu16, wd16, wo16 = _in_proj(
        h, pos_col, attn_norm_g.reshape(1, -1), sp, w_in_stack, layer, w2, wq, wkv, cq_g.reshape(1, -1),
        ckv_g.reshape(1, -1), bd, (w_gate, w_up, w_down, w_out), tm=IN_PROJ_ROWS)
    yb = _mla_attn(qb, kn, krd, vb, batch, seq, tq=MLA_Q_ROWS, hps=MLA_HEADS_PER_STEP)
    h = _swa_out(slope_sink, qa, ka, va, pos_col, pos_row, h, yb, qm, km, vm, wo16, batch, seq, m_len,
                 tq=ATTN_OUT_ROWS)
    return _ffn(h, ffn_g.reshape(1, -1), wg16, wu16, wd16, tm=FFN_ROWS, tf=FFN_COLS)


def kernel(x, mem, positions, attn_norm_g, w_in, swa_q_norm_g, swa_k_norm_g, swa_sinks, mla_cq_norm_g,
           mla_ckv_norm_g, w_uq, w_ukv, mla_qn_norm_g, mla_qr_norm_g, mla_kn_norm_g, mla_kr_norm_g,
           mem_norm_g, w_mem_kv, mem_q_norm_g, mem_k_norm_g, w_out, ffn_norm_g, w_gate, w_up, w_down):
    batch, seq, d = x.shape
    m_len = mem.shape[1]
    stacked = (attn_norm_g, swa_q_norm_g, swa_k_norm_g, swa_sinks, mla_cq_norm_g, mla_ckv_norm_g,
               w_uq, w_ukv, mla_qn_norm_g, mla_qr_norm_g, mla_kn_norm_g, mla_kr_norm_g, mem_norm_g,
               w_mem_kv, mem_q_norm_g, mem_k_norm_g, w_out, ffn_norm_g, w_gate, w_up, w_down)
    h = x.reshape(batch * seq, d)
    mem2d = mem.reshape(batch * m_len, d)
    pos_col = positions.reshape(batch * seq, 1)
    pos_row = positions.reshape(batch, 1, seq)
    w_in16 = w_in.astype(BF16)
    for l in range(attn_norm_g.shape[0]):
        h = _layer(h, mem2d, pos_col, pos_row, batch, seq, m_len, w_in16, l, tuple(a[l] for a in stacked))
    return h.reshape(batch, seq, d)
```

```python
import functools

import jax
import jax.numpy as jnp
from jax import lax
from jax.experimental import pallas as pl
from jax.experimental.pallas import tpu as pltpu

EPS = 1e-6
NEG_INF = -1e30
LOG2E = 1.4426950408889634
BLOCK = 128

SWA_Q_HEADS = 16
SWA_KV_HEADS = 2
SWA_HEAD_DIM = 64

MLA_HEADS = 4
MLA_Q_RANK = 512
MLA_KV_RANK = 512
MLA_NOPE_DIM = 128
MLA_ROPE_DIM = 64
MLA_V_DIM = 128
ROPE_THETA = 10000.0

MEM_HEADS = 4
MEM_HEAD_DIM = 128

LANES = 128
MXU_DIM = 256
VMEM_LIMIT_BYTES = 56 * 1024 * 1024

IN_PROJ_ROWS = 512
ATTN_OUT_ROWS = 512
MLA_Q_ROWS = 1024
MLA_HEADS_PER_STEP = 2
FFN_ROWS = 1024
FFN_COLS = 512

F32 = jnp.float32
BF16 = jnp.bfloat16

_SP_SWA_Q, _SP_SWA_K, _SP_QN, _SP_QR, _SP_KN, _SP_KR, _SP_MEMQ, _SP_FREQ, _SP_SIGN = range(9)
_SP_ROWS = 16


def _dot(a, b):
    return jnp.dot(a, b, preferred_element_type=F32)


def _dot_nt(a, b):
    return lax.dot_general(a, b, (((1,), (1,)), ((), ())), preferred_element_type=F32)


def _rms(x, g):
    return x * lax.rsqrt(jnp.mean(x * x, axis=-1, keepdims=True) + EPS) * g


def _const_spec(shape):
    nd = len(shape)
    return pl.BlockSpec(shape, lambda *_: (0,) * nd, pipeline_mode=pl.Buffered(1))


def _mem_kv_kernel(mem_ref, g_ref, w32_ref, gk_ref, win_ref, k_ref, v_ref, w2_ref, w_ref):
    @pl.when(pl.program_id(0) == 0)
    def _():
        w_ref[...] = w32_ref[...].astype(BF16)

    half = MLA_ROPE_DIM
    lo = lax.broadcasted_iota(jnp.int32, (1, LANES), 1) < half
    n_full = (win_ref.shape[1] - _W1_COLS) // LANES
    tail = [win_ref[:, _W1_COLS + k * LANES:_W1_COLS + (k + 1) * LANES].astype(F32) for k in range(n_full)]
    last = win_ref[:, _W1_COLS + n_full * LANES:].astype(F32)
    tail.append(jnp.concatenate([last, last], axis=1))
    swapped = [pltpu.roll(c, half, 1) for c in tail[:n_full]]
    w2_ref[:, :LANES] = jnp.where(lo, tail[0], swapped[0]).astype(BF16)
    for k in range(1, n_full + 1):
        nxt = swapped[k] if k < n_full else tail[k]
        w2_ref[:, k * LANES:(k + 1) * LANES] = jnp.where(lo, swapped[k - 1], nxt).astype(BF16)

    xn = _rms(mem_ref[...], g_ref[...]).astype(BF16)
    kv = _dot(xn, w_ref[...])
    hd = MEM_HEADS * MEM_HEAD_DIM
    for h in range(MEM_HEADS):
        sl = slice(h * MEM_HEAD_DIM, (h + 1) * MEM_HEAD_DIM)
        k_ref[:, sl] = _rms(kv[:, sl], gk_ref[...]).astype(BF16)
        v_ref[:, 2 * h * LANES:(2 * h + 1) * LANES] = kv[:, hd + h * MEM_HEAD_DIM:
                                                         hd + (h + 1) * MEM_HEAD_DIM].astype(BF16)
        v_ref[:, (2 * h + 1) * LANES:(2 * h + 2) * LANES] = jnp.ones((kv.shape[0], LANES), BF16)


def _mem_kv(mem2d, g, w, gk, w_in_stack, layer, batch, m_len):
    d = mem2d.shape[1]
    hd = MEM_HEADS * MEM_HEAD_DIM
    _, rows, cols = w_in_stack.shape
    assert rows % (16 * batch) == 0 and (cols - _W1_COLS) % LANES == MLA_ROPE_DIM
    slab = rows // batch
    w2_cols = cols - _W1_COLS + MLA_ROPE_DIM
    return pl.pallas_call(
        _mem_kv_kernel,
        grid=(batch,),
        in_specs=[pl.BlockSpec((m_len, d), lambda b: (b, 0)),
                  _const_spec((1, d)), _const_spec((d, 2 * hd)), _const_spec((1, MEM_HEAD_DIM)),
                  pl.BlockSpec((None, slab, cols), lambda b: (layer, b, 0))],
        out_specs=[pl.BlockSpec((m_len, hd), lambda b: (b, 0)), pl.BlockSpec((m_len, 2 * hd), lambda b: (b, 0)),
                   pl.BlockSpec((slab, w2_cols), lambda b: (b, 0))],
        out_shape=[jax.ShapeDtypeStruct((batch * m_len, hd), BF16),
                   jax.ShapeDtypeStruct((batch * m_len, 2 * hd), BF16),
                   jax.ShapeDtypeStruct((rows, w2_cols), BF16)],
        scratch_shapes=[pltpu.VMEM((d, 2 * hd), BF16)],
        compiler_params=pltpu.CompilerParams(dimension_semantics=("arbitrary",),
                                             vmem_limit_bytes=VMEM_LIMIT_BYTES),
        name="mem_kv",
    )(mem2d, g, w, gk, w_in_stack)


_C_QA = (0, SWA_Q_HEADS * SWA_HEAD_DIM)
_C_KVA = (_C_QA[1], _C_QA[1] + 2 * SWA_KV_HEADS * SWA_HEAD_DIM)
_C_CQ = (_C_KVA[1], _C_KVA[1] + MLA_Q_RANK)
_C_CKV = (_C_CQ[1], _C_CQ[1] + MLA_KV_RANK)
_W1_COLS = _C_CKV[1]
_C_KR = (_W1_COLS, _W1_COLS + 2 * MLA_ROPE_DIM)
_C_QM = (_C_KR[1], _C_KR[1] + MEM_HEADS * MEM_HEAD_DIM)
assert _W1_COLS % LANES == 0 and _C_QA[1] % MXU_DIM == 0 and _C_KVA[1] - _C_KVA[0] == MXU_DIM


def _in_proj_kernel(x_ref, pos_ref, gattn_ref, sp_ref, w1_ref, w2_ref, wq_ref, wkv_ref, gcq_ref, gckv_ref,
                    bd_ref, wg32_ref, wu32_ref, wd32_ref, wo32_ref,
                    qa_ref, ka_ref, va_ref, krd_ref, qm_ref, qb_ref, kn_ref, vb_ref,
                    wg16_ref, wu16_ref, wd16_ref, wo16_ref):
    hc = FFN_COLS // 2
    for c in range(wg16_ref.shape[1] // FFN_COLS):
        lo, mid, hi = c * FFN_COLS, c * FFN_COLS + hc, (c + 1) * FFN_COLS
        wg16_ref[:, lo:mid] = wg32_ref[:, lo:mid].astype(BF16)
        wg16_ref[:, mid:hi] = wu32_ref[:, lo:mid].astype(BF16)
        wu16_ref[:, lo:mid] = wg32_ref[:, mid:hi].astype(BF16)
        wu16_ref[:, mid:hi] = wu32_ref[:, mid:hi].astype(BF16)
    wd16_ref[...] = wd32_ref[...].astype(BF16)
    wo16_ref[...] = wo32_ref[...].astype(BF16)
    sp = sp_ref[...]
    row = lambda r, n=LANES: sp[r:r + 1, :n]
    hn = _rms(x_ref[...], gattn_ref[...]).astype(BF16)

    def proj(cols):
        if cols[0] >= _W1_COLS:
            return _dot(hn, w2_ref[:, cols[0] - _W1_COLS:cols[1] - _W1_COLS])
        return _dot(hn, w1_ref[:, cols[0]:cols[1]])

    def head64_rms(blk, g):
        ss = _dot((blk * blk).astype(BF16), bd_ref[...])
        return blk * lax.rsqrt(ss * (1.0 / SWA_HEAD_DIM) + EPS) * g

    ang = pos_ref[...].astype(F32) * row(_SP_FREQ)
    cos = jnp.cos(ang)
    sin_signed = jnp.sin(ang) * row(_SP_SIGN)

    def rope_dup(y):
        return y * cos + pltpu.roll(y, MLA_ROPE_DIM // 2, 1) * sin_signed

    cq = proj(_C_CQ)
    ckv = proj(_C_CKV)
    qm = proj(_C_QM)
    kva = proj(_C_KVA)
    kr = proj(_C_KR)
    cqn = _rms(cq, gcq_ref[...]).astype(BF16)
    ckvn = _rms(ckv, gckv_ref[...]).astype(BF16)

    for h in range(MEM_HEADS):
        sl = slice(h * MEM_HEAD_DIM, (h + 1) * MEM_HEAD_DIM)
        qm_ref[:, sl] = (_rms(qm[:, sl], row(_SP_MEMQ)) * (LOG2E * MEM_HEAD_DIM ** -0.5)).astype(BF16)
    lo = lax.broadcasted_iota(jnp.int32, (1, LANES), 1) < SWA_HEAD_DIM
    for src, dst in ((head64_rms(kva, row(_SP_SWA_K, MXU_DIM))[:, :LANES], ka_ref), (kva[:, LANES:], va_ref)):
        swapped = pltpu.roll(src, SWA_HEAD_DIM, 1)
        dst[:, :LANES] = jnp.where(lo, src, swapped).astype(BF16)
        dst[:, LANES:] = jnp.where(lo, swapped, src).astype(BF16)
    krd_ref[...] = rope_dup(_rms(kr, row(_SP_KR))).astype(BF16)

    qb = _dot(cqn, wq_ref[...])
    kvb = _dot(ckvn, wkv_ref[...])
    qa = proj(_C_QA)

    scale = LOG2E * (MLA_NOPE_DIM + MLA_ROPE_DIM) ** -0.5
    for h in range(MLA_HEADS):
        o = h * 2 * LANES
        qb_ref[:, o:o + LANES] = (_rms(qb[:, o:o + LANES], row(_SP_QN)) * scale).astype(BF16)
        qr = rope_dup(_rms(qb[:, o + LANES:o + 2 * LANES], row(_SP_QR)))
        qb_ref[:, o + LANES:o + 2 * LANES] = (qr * (0.5 * scale)).astype(BF16)
    hn_cols = MLA_HEADS * MLA_NOPE_DIM
    for h in range(MLA_HEADS):
        sl = slice(h * MLA_NOPE_DIM, (h + 1) * MLA_NOPE_DIM)
        kn_ref[:, sl] = _rms(kvb[:, sl], row(_SP_KN)).astype(BF16)
        vb_ref[:, 2 * h * LANES:(2 * h + 1) * LANES] = kvb[:, hn_cols + h * MLA_V_DIM:
                                                          hn_cols + (h + 1) * MLA_V_DIM].astype(BF16)
        vb_ref[:, (2 * h + 1) * LANES:(2 * h + 2) * LANES] = jnp.ones((kvb.shape[0], LANES), BF16)

    for c in range(_C_QA[1] // MXU_DIM):
        sl = slice(c * MXU_DIM, (c + 1) * MXU_DIM)
        qa_ref[:, sl] = (head64_rms(qa[:, sl], row(_SP_SWA_Q, MXU_DIM))
                         * (LOG2E * SWA_HEAD_DIM ** -0.5)).astype(BF16)


def _in_proj(x2d, pos_col, gattn, sp, w_in_stack, layer, w2, wq, wkv, gcq, gckv, bd, late_w, tm):
    t, d = x2d.shape
    steps = t // tm
    w1_spec = pl.BlockSpec((None, w_in_stack.shape[1], _W1_COLS), lambda i: (layer, 0, 0),
                           pipeline_mode=pl.Buffered(1))
    widths = (SWA_Q_HEADS * SWA_HEAD_DIM,
              SWA_KV_HEADS * LANES,
              SWA_KV_HEADS * LANES,
              LANES,
              MEM_HEADS * MEM_HEAD_DIM,
              MLA_HEADS * 2 * LANES,
              MLA_HEADS * MLA_NOPE_DIM,
              MLA_HEADS * 2 * LANES)
    rows = lambda w: pl.BlockSpec((tm, w), lambda i: (i, 0))
    slabs = []
    for w in late_w:
        assert w.shape[0] % (16 * steps) == 0, (w.shape, steps)
        slabs.append(pl.BlockSpec((w.shape[0] // steps, w.shape[1]), lambda i: (i, 0)))
    return pl.pallas_call(
        _in_proj_kernel,
        grid=(steps,),
        in_specs=[rows(d), rows(1), _const_spec(gattn.shape), _const_spec(sp.shape),
                  w1_spec, _const_spec(w2.shape), _const_spec(wq.shape),
                  _const_spec(wkv.shape),
                  _const_spec(gcq.shape), _const_spec(gckv.shape), _const_spec(bd.shape)] + slabs,
        out_specs=[rows(w) for w in widths] + slabs,
        out_shape=[jax.ShapeDtypeStruct((t, w), BF16) for w in widths]
        + [jax.ShapeDtypeStruct(w.shape, BF16) for w in late_w],
        compiler_params=pltpu.CompilerParams(dimension_semantics=("arbitrary",),
                                             vmem_limit_bytes=VMEM_LIMIT_BYTES),
        name="in_proj",
    )(x2d, pos_col, gattn, sp, w_in_stack, w2, wq, wkv, gcq, gckv, bd, *late_w)


_SWA_MASK_DIST = -NEG_INF * 2.0 ** 8


def _swa_tile(i, ss_ref, q_ref, kc_ref, kp_ref, vc_ref, vp_ref, pq_ref, pkc_ref, pkp_ref, o_ref, *, tq,
              filler=None):
    g = SWA_Q_HEADS // SWA_KV_HEADS
    kk = lax.broadcasted_iota(jnp.int32, (BLOCK, BLOCK), 0)
    qq = lax.broadcasted_iota(jnp.int32, (BLOCK, BLOCK), 1)
    from_prev = kk > qq
    prev_w = jnp.where(from_prev, 1.0, 0.0).astype(BF16)
    cur_w = jnp.where(from_prev, 0.0, 1.0).astype(BF16)
    lane_lo = lax.broadcasted_iota(jnp.int32, (BLOCK, LANES), 1) < SWA_HEAD_DIM
    lo_w = jnp.where(lane_lo, 1.0, 0.0).astype(BF16)
    hi_w = jnp.where(lane_lo, 0.0, 1.0).astype(BF16)

    for j in range(tq // BLOCK):
        if filler is not None:
            filler(j, tq // BLOCK)
        rows = slice(j * BLOCK, (j + 1) * BLOCK)
        prows = slice((j - 1) * BLOCK, j * BLOCK)
        k_prev, v_prev, pk_prev = ((kp_ref[...], vp_ref[...], pkp_ref[...]) if j == 0 else
                                   (kc_ref[prows, :], vc_ref[prows, :], pkc_ref[prows, :]))
        kj = jnp.concatenate([k_prev, kc_ref[rows, :]], axis=0)
        vj = jnp.concatenate([v_prev, vc_ref[rows, :]], axis=0)
        pq = pq_ref[:, rows]
        dist = jnp.where(from_prev, jnp.abs(pk_prev - pq), jnp.abs(pkc_ref[rows, :] - pq)).astype(F32)
        if j == 0:
            dist = jnp.where(from_prev, jnp.maximum(dist, jnp.where(i == 0, _SWA_MASK_DIST, 0.0)), dist)
        for kv in range(SWA_KV_HEADS):
            kvl = slice(kv * LANES, (kv + 1) * LANES)
            qm = jnp.concatenate(
                [q_ref[rows, (kv * g + hh) // 2 * LANES:((kv * g + hh) // 2 + 1) * LANES]
                 * (lo_w if hh % 2 == 0 else hi_w) for hh in range(g)], axis=0)
            st = _dot_nt(kj[:, kvl], qm)
            pts = []
            for hh in range(g):
                head = kv * g + hh
                cols = slice(hh * BLOCK, (hh + 1) * BLOCK)
                t = jnp.where(from_prev, st[:BLOCK, cols], st[BLOCK:, cols]) - ss_ref[0, head] * dist
                sink = ss_ref[1, head]
                m = jnp.maximum(jnp.max(t, axis=0, keepdims=True), sink)
                e = jnp.exp2(t - m)
                den = jnp.sum(e, axis=0, keepdims=True) + jnp.exp2(sink - m)
                p = (e * (1.0 / den)).astype(BF16)
                pts.append(jnp.concatenate([p * prev_w, p * cur_w], axis=0))
            pt = jnp.concatenate(pts, axis=1)
            o = lax.dot_general(pt, vj[:, kvl], (((0,), (0,)), ((), ())), preferred_element_type=F32)
            for pr in range(g // 2):
                grp = kv * (g // 2) + pr
                even = o[(2 * pr) * BLOCK:(2 * pr + 1) * BLOCK]
                odd = o[(2 * pr + 1) * BLOCK:(2 * pr + 2) * BLOCK]
                o_ref[rows, grp * LANES:(grp + 1) * LANES] = jnp.where(lane_lo, even, odd).astype(BF16)


def _mem_attn_tile(q_ref, k_ref, v_ref, o_ref):
    for h in range(MEM_HEADS):
        sl = slice(h * MEM_HEAD_DIM, (h + 1) * MEM_HEAD_DIM)
        s = _dot_nt(q_ref[:, sl], k_ref[:, sl])
        e = jnp.exp2(s - jnp.max(s, axis=-1, keepdims=True))
        o = _dot(e.astype(BF16), v_ref[:, 2 * h * LANES:(2 * h + 2) * LANES])
        o_ref[:, sl] = (o[:, :MEM_HEAD_DIM] / o[:, MEM_HEAD_DIM:]).astype(BF16)


def _swa_out_kernel(ss_ref, q_ref, kc_ref, kp_ref, vc_ref, vp_ref, pq_ref, pkc_ref, pkp_ref,
                    x_ref, yb_ref, qm_ref, km_ref, vm_ref, w_ref, h_ref, ya_sc, ym_sc, *, tq):
    na = ya_sc.shape[1]
    nb = na + yb_ref.shape[1]
    _mem_attn_tile(qm_ref, km_ref, vm_ref, ym_sc)
    d = h_ref.shape[1]

    def partial_projection(j, n):
        cols = slice(j * d // n, (j + 1) * d // n)
        h_ref[:, cols] = (x_ref[:, cols] + _dot(yb_ref[...], w_ref[na:nb, cols])
                          + _dot(ym_sc[...], w_ref[nb:, cols]))

    _swa_tile(pl.program_id(1), ss_ref, q_ref, kc_ref, kp_ref, vc_ref, vp_ref, pq_ref, pkc_ref, pkp_ref,
              ya_sc, tq=tq, filler=partial_projection)
    h_ref[...] += _dot(ya_sc[...], w_ref[0:na, :])


def _swa_out(slope_sink, qa, ka, va, pos_col, pos_row, x2d, yb, qm, km, vm, w_out, batch, seq, m_len, tq):
    t, d = x2d.shape
    nq = seq // tq
    nb = tq // BLOCK
    na = SWA_Q_HEADS * SWA_HEAD_DIM
    hd = MEM_HEADS * MEM_HEAD_DIM
    cur = lambda w: pl.BlockSpec((tq, w), lambda b, i: (b * nq + i, 0))
    prev = lambda w: pl.BlockSpec(
        (BLOCK, w), lambda b, i: (b * nq * nb + jnp.maximum(i * nb - 1, 0), 0))
    return pl.pallas_call(
        functools.partial(_swa_out_kernel, tq=tq),
        grid=(batch, nq),
        in_specs=[pl.BlockSpec(memory_space=pltpu.SMEM),
                  cur(na), cur(2 * LANES), prev(2 * LANES), cur(2 * LANES), prev(2 * LANES),
                  pl.BlockSpec((None, 1, tq), lambda b, i: (b, 0, i)),
                  cur(1), prev(1),
                  cur(d), cur(yb.shape[1]), cur(hd),
                  pl.BlockSpec((m_len, hd), lambda b, i: (b, 0)),
                  pl.BlockSpec((m_len, 2 * hd), lambda b, i: (b, 0)),
                  _const_spec(w_out.shape)],
        out_specs=cur(d),
        out_shape=jax.ShapeDtypeStruct((t, d), F32),
        scratch_shapes=[pltpu.VMEM((tq, na), BF16), pltpu.VMEM((tq, hd), BF16)],
        compiler_params=pltpu.CompilerParams(dimension_semantics=("arbitrary", "arbitrary"),
                                             vmem_limit_bytes=VMEM_LIMIT_BYTES),
        name="swa_out",
    )(slope_sink, qa, ka, ka, va, va, pos_row, pos_col, pos_col, x2d, yb, qm, km, vm, w_out)


def _mla_kernel(q_ref, kn_ref, kr_ref, v_ref, o_ref, s_sc, m_sc, acc_sc, *, tq, hps):
    qi = pl.program_id(2)
    qw = 2 * LANES
    tk = tq // 2

    all_heads = tuple(range(hps))

    def scores_to(slot, j, row0=0, heads=all_heads):
        k0 = pl.multiple_of(j * tk, tk)
        kr = kr_ref[pl.ds(k0, tk), :]
        for hh in heads:
            k = jnp.concatenate([kn_ref[pl.ds(k0, tk), hh * LANES:(hh + 1) * LANES], kr], axis=1)
            s_sc[slot, hh, row0:, :] = _dot_nt(q_ref[row0:, hh * qw:(hh + 1) * qw], k)

    def update_from(slot, j, row0=0, masked=False, heads=all_heads):
        k0 = pl.multiple_of(j * tk, tk)
        for hh in heads:
            for r0 in range(row0, tq, tk):
                rs = slice(r0, r0 + tk)
                s = s_sc[slot, hh, rs, :]
                if masked and r0 - row0 < tk:
                    r = lax.broadcasted_iota(jnp.int32, s.shape, 0) + (r0 - row0)
                    c = lax.broadcasted_iota(jnp.int32, s.shape, 1)
                    s = jnp.where(c <= r, s, NEG_INF)
                m = m_sc[hh, rs, :]
                m_new = jnp.maximum(m, jnp.max(s, axis=-1, keepdims=True))
                p = jnp.exp2(s - m_new)
                m_sc[hh, rs, :] = m_new
                acc_sc[hh, rs, :] = jnp.exp2(m - m_new) * acc_sc[hh, rs, :] + _dot(
                    p.astype(BF16), v_ref[pl.ds(k0, tk), hh * qw:(hh + 1) * qw])

    m_sc[...] = jnp.full(m_sc.shape, NEG_INF, F32)
    acc_sc[...] = jnp.zeros(acc_sc.shape, F32)

    scores_to(0, 0)

    def pair(p, carry):
        j = 2 * p
        for hh in all_heads:
            scores_to(1, j + 1, heads=(hh,))
            update_from(0, j, heads=(hh,))
        for hh in all_heads:
            scores_to(0, j + 2, heads=(hh,))
            update_from(1, j + 1, heads=(hh,))
        return carry

    lax.fori_loop(0, qi, pair, 0)
    for hh in all_heads:
        scores_to(1, 2 * qi + 1, row0=tk, heads=(hh,))
        update_from(0, 2 * qi, masked=True, heads=(hh,))
    update_from(1, 2 * qi + 1, row0=tk, masked=True)

    for hh in range(hps):
        acc = acc_sc[hh]
        o_ref[:, hh * MLA_V_DIM:(hh + 1) * MLA_V_DIM] = (acc[:, :MLA_V_DIM] / acc[:, MLA_V_DIM:]).astype(BF16)


def _mla_attn(qb, kn, krd, vb, batch, seq, tq, hps):
    t = batch * seq
    nq = seq // tq
    assert tq % 2 == 0 and seq % tq == 0
    return pl.pallas_call(
        functools.partial(_mla_kernel, tq=tq, hps=hps),
        grid=(batch, MLA_HEADS // hps, nq),
        in_specs=[pl.BlockSpec((tq, 2 * LANES * hps), lambda b, h, i: (b * nq + i, h)),
                  pl.BlockSpec((seq, MLA_NOPE_DIM * hps), lambda b, h, i: (b, h)),
                  pl.BlockSpec((seq, LANES), lambda b, h, i: (b, 0)),
                  pl.BlockSpec((seq, 2 * LANES * hps), lambda b, h, i: (b, h))],
        out_specs=pl.BlockSpec((tq, MLA_V_DIM * hps), lambda b, h, i: (b * nq + i, h)),
        out_shape=jax.ShapeDtypeStruct((t, MLA_HEADS * MLA_V_DIM), BF16),
        scratch_shapes=[pltpu.VMEM((2, hps, tq, tq // 2), F32), pltpu.VMEM((hps, tq, 1), F32),
                        pltpu.VMEM((hps, tq, 2 * LANES), F32)],
        compiler_params=pltpu.CompilerParams(dimension_semantics=("arbitrary",) * 3,
                                             vmem_limit_bytes=VMEM_LIMIT_BYTES),
        name="mla_attn",
    )(qb, kn, krd, vb)


def _ffn_kernel(h_ref, g_ref, wg_ref, wu_ref, wd_ref, o_ref, fn_ref):
    j = pl.program_id(1)

    @pl.when(j == 0)
    def _():
        h = h_ref[...]
        fn_ref[...] = _rms(h, g_ref[...]).astype(BF16)
        o_ref[...] = h

    fn = fn_ref[...]
    half = wg_ref.shape[1] // 2
    acts = []
    for w_ref in (wg_ref, wu_ref):
        gu = _dot(fn, w_ref[...])
        gate, up = gu[:, :half], gu[:, half:]
        acts.append((gate * jax.nn.sigmoid(gate) * up).astype(BF16))
    o_ref[...] += _dot(acts[0], wd_ref[:half, :]) + _dot(acts[1], wd_ref[half:, :])


def _ffn(h, g, wg, wu, wd, tm, tf):
    t, d = h.shape
    dff = wg.shape[1]
    return pl.pallas_call(
        _ffn_kernel,
        grid=(t // tm, dff // tf),
        in_specs=[pl.BlockSpec((tm, d), lambda i, j: (i, 0)),
                  _const_spec((1, d)),
                  pl.BlockSpec((d, tf), lambda i, j: (0, j)),
                  pl.BlockSpec((d, tf), lambda i, j: (0, j)),
                  pl.BlockSpec((tf, d), lambda i, j: (j, 0))],
        out_specs=pl.BlockSpec((tm, d), lambda i, j: (i, 0)),
        out_shape=jax.ShapeDtypeStruct((t, d), F32),
        scratch_shapes=[pltpu.VMEM((tm, d), BF16)],
        compiler_params=pltpu.CompilerParams(dimension_semantics=("arbitrary", "arbitrary"),
                                             vmem_limit_bytes=VMEM_LIMIT_BYTES),
        name="ffn",
    )(h, g, wg, wu, wd)


def _tile_row(v, width):
    v = v.astype(F32).reshape(-1)
    return jnp.tile(v, width // v.shape[0])


def _layer(h, mem2d, pos_col, pos_row, batch, seq, m_len, w_in_stack, layer, p):
    (attn_norm_g, swa_q_g, swa_k_g, swa_sinks, cq_g, ckv_g, w_uq, w_ukv, qn_g, qr_g, kn_g, kr_g,
     mem_g, w_mem_kv, memq_g, memk_g, w_out, ffn_g, w_gate, w_up, w_down) = p
    width = 2 * LANES
    inv_freq = ROPE_THETA ** (-jnp.arange(0, MLA_ROPE_DIM, 2, dtype=F32) / MLA_ROPE_DIM)
    sign = jnp.concatenate([-jnp.ones((MLA_ROPE_DIM // 2,), F32), jnp.ones((MLA_ROPE_DIM // 2,), F32)])
    rows = [swa_q_g, swa_k_g, qn_g, qr_g, kn_g, kr_g, memq_g, inv_freq, sign]
    sp = jnp.stack([_tile_row(r, width) for r in rows]
                   + [jnp.zeros((width,), F32)] * (_SP_ROWS - len(rows)))

    qd = MLA_NOPE_DIM + MLA_ROPE_DIM
    wq3 = w_uq.reshape(w_uq.shape[0], MLA_HEADS, qd)
    wq = jnp.concatenate([wq3, wq3[:, :, MLA_NOPE_DIM:]], axis=2).reshape(w_uq.shape[0], -1).astype(BF16)
    wkv3 = w_ukv.reshape(w_ukv.shape[0], MLA_HEADS, MLA_NOPE_DIM + MLA_V_DIM)
    wkv = jnp.concatenate([wkv3[:, :, :MLA_NOPE_DIM].reshape(w_ukv.shape[0], -1),
                           wkv3[:, :, MLA_NOPE_DIM:].reshape(w_ukv.shape[0], -1)], axis=1).astype(BF16)
    idx = jnp.arange(MXU_DIM) // SWA_HEAD_DIM
    bd = (idx[:, None] == idx[None, :]).astype(BF16)
    slopes = 2.0 ** (-8.0 * jnp.arange(1, SWA_Q_HEADS + 1, dtype=F32) / SWA_Q_HEADS)
    slope_sink = jnp.stack([slopes, swa_sinks.astype(F32)]) * LOG2E

    km, vm, w2 = _mem_kv(mem2d, mem_g.reshape(1, -1), w_mem_kv, memk_g.reshape(1, -1), w_in_stack,
                         layer, batch, m_len)
    qa, ka, va, krd, qm, qb, kn, vb, wg16, wu16, wd16, wo16 = _in_proj(
        h, pos_col, attn_norm_g.reshape(1, -1), sp, w_in_stack, layer, w2, wq, wkv, cq_g.reshape(1, -1),
        ckv_g.reshape(1, -1), bd, (w_gate, w_up, w_down, w_out), tm=IN_PROJ_ROWS)
    yb = _mla_attn(qb, kn, krd, vb, batch, seq, tq=MLA_Q_ROWS, hps=MLA_HEADS_PER_STEP)
    h = _swa_out(slope_sink, qa, ka, va, pos_col, pos_row, h, yb, qm, km, vm, wo16, batch, seq, m_len,
                 tq=ATTN_OUT_ROWS)
    return _ffn(h, ffn_g.reshape(1, -1), wg16, wu16, wd16, tm=FFN_ROWS, tf=FFN_COLS)


def kernel(x, mem, positions, attn_norm_g, w_in, swa_q_norm_g, swa_k_norm_g, swa_sinks, mla_cq_norm_g,
           mla_ckv_norm_g, w_uq, w_ukv, mla_qn_norm_g, mla_qr_norm_g, mla_kn_norm_g, mla_kr_norm_g,
           mem_norm_g, w_mem_kv, mem_q_norm_g, mem_k_norm_g, w_out, ffn_norm_g, w_gate, w_up, w_down):
    batch, seq, d = x.shape
    m_len = mem.shape[1]
    stacked = (attn_norm_g, swa_q_norm_g, swa_k_norm_g, swa_sinks, mla_cq_norm_g, mla_ckv_norm_g,
               w_uq, w_ukv, mla_qn_norm_g, mla_qr_norm_g, mla_kn_norm_g, mla_kr_norm_g, mem_norm_g,
               w_mem_kv, mem_q_norm_g, mem_k_norm_g, w_out, ffn_norm_g, w_gate, w_up, w_down)
    h = x.reshape(batch * seq, d)
    mem2d = mem.reshape(batch * m_len, d)
    pos_col = positions.reshape(batch * seq, 1)
    pos_row = positions.reshape(batch, 1, seq)
    w_in16 = w_in.astype(BF16)
    for l in range(attn_norm_g.shape[0]):
        h = _layer(h, mem2d, pos_col, pos_row, batch, seq, m_len, w_in16, l, tuple(a[l] for a in stacked))
    return h.reshape(batch, seq, d)
```
